```python
import math
import jax, jax.numpy as jnp
from jax import lax
import numpy as np

D_MODEL = 2048
BATCH = 4
SEQ = 2048
DEPTH = 4

N_MIXERS = 2
N_CONV_LAYERS = (DEPTH + N_MIXERS - 1) // N_MIXERS
N_FOX_LAYERS = DEPTH // N_MIXERS
MEM_LEN = 256
HEAD_DIM = 128
N_MEM_HEADS = 4
MEM_WIDTH = N_MEM_HEADS * HEAD_DIM
PRIMARY_WIDTH = D_MODEL - MEM_WIDTH
CONV_CH = PRIMARY_WIDTH
CONV_WIDTH = 31
N_FOX_HEADS = PRIMARY_WIDTH // HEAD_DIM
Q_BLOCK = 128
CONV_IN = 2 * CONV_CH + MEM_WIDTH
FOX_IN = 3 * PRIMARY_WIDTH + N_FOX_HEADS + MEM_WIDTH
N_EXPERTS = 64
TOP_K = 8
N_GROUPS = 8
TOPK_GROUPS = 4
D_EXPERT = 512
D_SHARED = 512
ROUTED_SCALE = 2.5
MOE_BLOCK = 128
LN_EPS = 1e-5
DEEPNORM_ALPHA = (2 * DEPTH) ** 0.25
DEEPNORM_BETA = (8 * DEPTH) ** -0.25

kernel_name = "hybrid_conv_fox_memxattn_moe_deepnorm"


def layer_norm(x, g, b):
    xf = x.astype(jnp.float32)
    mu = xf.mean(-1, keepdims=True)
    var = jnp.square(xf - mu).mean(-1, keepdims=True)
    y = (xf - mu) * lax.rsqrt(var + LN_EPS) * g.astype(jnp.float32) + b.astype(jnp.float32)
    return y.astype(x.dtype)


def conformer_conv(a, g, dw, dw_b, ln_g, ln_b):
    u = a * jax.nn.sigmoid(g)
    y = lax.conv_general_dilated(
        u, dw[:, None, :].astype(u.dtype), window_strides=(1,),
        padding=[(CONV_WIDTH - 1, 0)], dimension_numbers=('NWC', 'WIO', 'NWC'),
        feature_group_count=CONV_CH) + dw_b
    return jax.nn.silu(layer_norm(y, ln_g, ln_b))


def forgetting_attention(q, k, v, f_logit):
    B, S, H, Dh = q.shape
    c = jnp.cumsum(jax.nn.log_sigmoid(f_logit.astype(jnp.float32)), axis=1)
    c = jnp.transpose(c, (0, 2, 1))
    scale = Dh ** -0.5
    outs = []
    for blk in range(S // Q_BLOCK):
        q0, q1 = blk * Q_BLOCK, (blk + 1) * Q_BLOCK
        s = jnp.einsum('bqhd,bkhd->bhqk', q[:, q0:q1], k[:, :q1]).astype(jnp.float32) * scale
        bias = c[:, :, q0:q1, None] - c[:, :, None, :q1]
        causal = (q0 + jnp.arange(Q_BLOCK))[:, None] >= jnp.arange(q1)[None, :]
        p = jax.nn.softmax(jnp.where(causal, s + bias, -jnp.inf), axis=-1).astype(v.dtype)
        outs.append(jnp.einsum('bhqk,bkhd->bqhd', p, v[:, :q1]))
    return jnp.concatenate(outs, axis=1)


def memory_attention(q_mem, mem, w_kv):
    B, S, _ = q_mem.shape
    kv = mem @ w_kv
    k = kv[..., :MEM_WIDTH].reshape(B, -1, N_MEM_HEADS, HEAD_DIM)
    v = kv[..., MEM_WIDTH:].reshape(B, -1, N_MEM_HEADS, HEAD_DIM)
    q = q_mem.reshape(B, S, N_MEM_HEADS, HEAD_DIM)
    s = jnp.einsum('bshd,bmhd->bhsm', q, k).astype(jnp.float32) * (HEAD_DIM ** -0.5)
    p = jax.nn.softmax(s, axis=-1).astype(v.dtype)
    return jnp.einsum('bhsm,bmhd->bshd', p, v).reshape(B, S, MEM_WIDTH)


def route(x_flat, w_r, r_bias):
    N = x_flat.shape[0]
    scores = jax.nn.sigmoid((x_flat @ w_r).astype(jnp.float32))
    choice = scores + r_bias.astype(jnp.float32)
    grp_score = lax.top_k(choice.reshape(N, N_GROUPS, -1), 2)[0].sum(-1)
    _, top_g = lax.top_k(grp_score, TOPK_GROUPS)
    gmask = jnp.any(top_g[..., None] == jnp.arange(N_GROUPS), axis=-2)
    emask = jnp.repeat(gmask, N_EXPERTS // N_GROUPS, axis=-1)
    _, idx = lax.top_k(jnp.where(emask, choice, -jnp.inf), TOP_K)
    w = jnp.take_along_axis(scores, idx, axis=-1)
    w = w / w.sum(-1, keepdims=True) * ROUTED_SCALE
    return idx, w


def swiglu(x, w_up, w_down):
    h = x @ w_up
    f = w_up.shape[-1] // 2
    return (jax.nn.silu(h[..., :f]) * h[..., f:]) @ w_down


def routed_experts(x_flat, idx, gate, w_up, w_down):
    N, D = x_flat.shape
    A = N * TOP_K
    n_blocks = -(-A // MOE_BLOCK) + N_EXPERTS
    P = n_blocks * MOE_BLOCK
    e_flat = idx.reshape(-1)
    order = jnp.argsort(e_flat)
    e_sorted = e_flat[order]
    counts = jnp.bincount(e_flat, length=N_EXPERTS)
    starts = jnp.cumsum(counts) - counts
    padded = (counts + MOE_BLOCK - 1) // MOE_BLOCK * MOE_BLOCK
    pad_end = jnp.cumsum(padded)
    pad_start = pad_end - padded
    dest = pad_start[e_sorted] + (jnp.arange(A) - starts[e_sorted])
    slot_token = jnp.full((P,), N, jnp.int32).at[dest].set((order // TOP_K).astype(jnp.int32))
    slot_gate = jnp.zeros((P,), x_flat.dtype).at[dest].set(gate.reshape(-1)[order].astype(x_flat.dtype))
    block_expert = jnp.minimum(
        jnp.searchsorted(pad_end, jnp.arange(n_blocks) * MOE_BLOCK, side='right'), N_EXPERTS - 1)
    x_pad = jnp.concatenate([x_flat, jnp.zeros((1, D), x_flat.dtype)], axis=0)

    def block_fn(args):
        tok, g, e = args
        return swiglu(x_pad[tok], w_up[e], w_down[e]) * g[:, None]

    y = lax.map(block_fn, (slot_token.reshape(n_blocks, MOE_BLOCK),
                           slot_gate.reshape(n_blocks, MOE_BLOCK), block_expert))
    out = jnp.zeros((N + 1, D), x_flat.dtype).at[slot_token].add(y.reshape(P, D))
    return out[:N]


def moe_ffn(h, w_r, r_bias, w_up, w_down, sh_up, sh_down):
    B, S, D = h.shape
    x_flat = h.reshape(B * S, D)
    idx, gate = route(x_flat, w_r, r_bias)
    y = routed_experts(x_flat, idx, gate, w_up, w_down) + swiglu(x_flat, sh_up, sh_down)
    return y.reshape(B, S, D)


def setup_inputs(seed: int = 0) -> dict:
    key = jax.random.key(seed)
    ks = jax.random.split(key, 20)
    D = D_MODEL
    nrm = lambda k, shape, scale: jax.random.normal(k, shape, jnp.float32) * scale
    beta = DEEPNORM_BETA
    fox_col_scale = jnp.concatenate([
        jnp.ones((2 * PRIMARY_WIDTH,), jnp.float32),
        jnp.full((PRIMARY_WIDTH,), beta, jnp.float32),
        jnp.ones((N_FOX_HEADS + MEM_WIDTH,), jnp.float32)])
    kv_col_scale = jnp.concatenate([
        jnp.ones((MEM_WIDTH,), jnp.float32), jnp.full((MEM_WIDTH,), beta, jnp.float32)])
    return {
        "x": nrm(ks[0], (BATCH, SEQ, D), 1.0),
        "mem": nrm(ks[1], (BATCH, MEM_LEN, D), 1.0),
        "conv_w_in": nrm(ks[2], (N_CONV_LAYERS, D, CONV_IN), D ** -0.5),
        "conv_dw": nrm(ks[3], (N_CONV_LAYERS, CONV_WIDTH, CONV_CH), CONV_WIDTH ** -0.5),
        "conv_dw_b": nrm(ks[4], (N_CONV_LAYERS, CONV_CH), 0.01),
        "conv_ln_g": 1.0 + nrm(ks[5], (N_CONV_LAYERS, CONV_CH), 0.02),
        "conv_ln_b": nrm(ks[6], (N_CONV_LAYERS, CONV_CH), 0.01),
        "fox_w_in": nrm(ks[7], (N_FOX_LAYERS, D, FOX_IN), D ** -0.5) * fox_col_scale,
        "fox_b_f": 4.0 + nrm(ks[8], (N_FOX_LAYERS, N_FOX_HEADS), 0.5),
        "mem_w_kv": nrm(ks[9], (DEPTH, D, 2 * MEM_WIDTH), D ** -0.5) * kv_col_scale,
        "w_out": nrm(ks[10], (DEPTH, D, D), beta * D ** -0.5),
        "ln_g": 1.0 + nrm(ks[11], (DEPTH, 2, D), 0.02),
        "ln_b": nrm(ks[12], (DEPTH, 2, D), 0.01),
        "router_w": nrm(ks[13], (DEPTH, D, N_EXPERTS), D ** -0.5),
        "router_bias": nrm(ks[14], (DEPTH, N_EXPERTS), 0.01),
        "exp_w_up": nrm(ks[15], (DEPTH, N_EXPERTS, D, 2 * D_EXPERT), D ** -0.5),
        "exp_w_down": nrm(ks[16], (DEPTH, N_EXPERTS, D_EXPERT, D), beta * D_EXPERT ** -0.5),
        "shared_w_up": nrm(ks[17], (DEPTH, D, 2 * D_SHARED), D ** -0.5),
        "shared_w_down": nrm(ks[18], (DEPTH, D_SHARED, D), beta * D_SHARED ** -0.5),
    }


def reference(x, mem, conv_w_in, conv_dw, conv_dw_b, conv_ln_g, conv_ln_b, fox_w_in, fox_b_f,
              mem_w_kv, w_out, ln_g, ln_b, router_w, router_bias, exp_w_up, exp_w_down,
              shared_w_up, shared_w_down):
    B, S, D = x.shape
    h = x
    for i in range(DEPTH):
        j = i // N_MIXERS
        if i % N_MIXERS == 0:
            proj = h @ conv_w_in[j]
            prim = conformer_conv(proj[..., :CONV_CH], proj[..., CONV_CH:2 * CONV_CH],
                                  conv_dw[j], conv_dw_b[j], conv_ln_g[j], conv_ln_b[j])
            q_mem = proj[..., 2 * CONV_CH:]
        else:
            proj = h @ fox_w_in[j]
            shp = (B, S, N_FOX_HEADS, HEAD_DIM)
            q = proj[..., :PRIMARY_WIDTH].reshape(shp)
            k = proj[..., PRIMARY_WIDTH:2 * PRIMARY_WIDTH].reshape(shp)
            v = proj[..., 2 * PRIMARY_WIDTH:3 * PRIMARY_WIDTH].reshape(shp)
            f_logit = proj[..., 3 * PRIMARY_WIDTH:3 * PRIMARY_WIDTH + N_FOX_HEADS] + fox_b_f[j]
            prim = forgetting_attention(q, k, v, f_logit).reshape(B, S, PRIMARY_WIDTH)
            q_mem = proj[..., 3 * PRIMARY_WIDTH + N_FOX_HEADS:]
        mem_o = memory_attention(q_mem, mem, mem_w_kv[i])
        mix = jnp.concatenate([prim, mem_o], axis=-1) @ w_out[i]
        h = layer_norm(DEEPNORM_ALPHA * h + mix, ln_g[i, 0], ln_b[i, 0])
        ffn = moe_ffn(h, router_w[i], router_bias[i], exp_w_up[i], exp_w_down[i],
                      shared_w_up[i], shared_w_down[i])
        h = layer_norm(DEEPNORM_ALPHA * h + ffn, ln_g[i, 1], ln_b[i, 1])
    return h
```

```python
import functools

import jax
import jax.numpy as jnp
from jax import lax
from jax.experimental import pallas as pl
from jax.experimental.pallas import tpu as pltpu

F32 = jnp.float32
BF16 = jnp.bfloat16

D_MODEL = 2048
DEPTH = 4
N_MIXERS = 2
MEM_LEN = 256
HEAD_DIM = 128
N_MEM_HEADS = 4
MEM_WIDTH = N_MEM_HEADS * HEAD_DIM
PRIMARY_WIDTH = D_MODEL - MEM_WIDTH
CONV_CH = PRIMARY_WIDTH
CONV_WIDTH = 31
N_FOX_HEADS = PRIMARY_WIDTH // HEAD_DIM
N_EXPERTS = 64
TOP_K = 8
N_GROUPS = 8
TOPK_GROUPS = 4
D_EXPERT = 512
ROUTED_SCALE = 2.5
LN_EPS = 1e-5
DEEPNORM_ALPHA = (2 * DEPTH) ** 0.25

LANES = 128
VMEM_LIMIT = 56 * 1024 * 1024
MM_TM, MM_TN = 1024, 512
ROW_TILE = 256
CONV_TS = 256
CONV_HALO = 32
CONV_SUB = 64
FOX_TQ = 256
MEM_TQ = 512
EXP_TM = 256


def _cparams(*sem):
    return pltpu.CompilerParams(dimension_semantics=sem, vmem_limit_bytes=VMEM_LIMIT)


def _ln_rows(y, g, b):
    mu = jnp.mean(y, axis=-1, keepdims=True)
    d = y - mu
    var = jnp.mean(d * d, axis=-1, keepdims=True)
    return d * lax.rsqrt(var + LN_EPS) * g + b


def _mm_kernel(a_ref, w_ref, o_ref):
    o_ref[...] = jnp.dot(a_ref[...], w_ref[...], preferred_element_type=F32).astype(o_ref.dtype)


def matmul(a, w, out_dtype, tm=MM_TM, tn=MM_TN):
    M, K = a.shape
    N = w.shape[1]
    tm, tn = min(tm, M), min(tn, N)
    assert M % tm == 0 and N % tn == 0
    return pl.pallas_call(
        _mm_kernel,
        grid=(M // tm, N // tn),
        in_specs=[pl.BlockSpec((tm, K), lambda i, j: (i, 0)),
                  pl.BlockSpec((K, tn), lambda i, j: (0, j))],
        out_specs=pl.BlockSpec((tm, tn), lambda i, j: (i, j)),
        out_shape=jax.ShapeDtypeStruct((M, N), out_dtype),
        compiler_params=_cparams("arbitrary", "arbitrary"),
        name="matmul",
    )(a, w)


def _conv_kernel(a_ref, g_ref, ah_ref, gh_ref, dw_ref, dwb_ref, lg_ref, lb_ref, o_ref, u_scr, y_scr):
    ts = a_ref.shape[0]
    nchunk = CONV_CH // LANES
    first = pl.program_id(1) == 0
    uh = ah_ref[...].astype(F32) * jax.nn.sigmoid(gh_ref[...].astype(F32))
    uh = jnp.where(first, 0.0, uh)
    u = a_ref[...].astype(F32) * jax.nn.sigmoid(g_ref[...].astype(F32))
    for c in range(nchunk):
        sl = slice(c * LANES, (c + 1) * LANES)
        u_scr[c, 0:CONV_HALO, :] = uh[:, sl]
        u_scr[c, CONV_HALO:CONV_HALO + ts, :] = u[:, sl]

    off = CONV_HALO - (CONV_WIDTH - 1)

    def chunk_body(c, carry):
        w = dw_ref[c]
        bias = dwb_ref[c]
        for r in range(ts // CONV_SUB):
            acc = jnp.broadcast_to(bias, (CONV_SUB, LANES))
            for j in range(CONV_WIDTH):
                acc = acc + w[j:j + 1, :] * u_scr[c, pl.ds(off + j + r * CONV_SUB, CONV_SUB), :]
            y_scr[c, r * CONV_SUB:(r + 1) * CONV_SUB, :] = acc
        return carry

    lax.fori_loop(0, nchunk, chunk_body, 0)

    s1 = jnp.zeros((ts, 1), F32)
    for c in range(nchunk):
        s1 = s1 + jnp.sum(y_scr[c], axis=-1, keepdims=True)
    mu = s1 * (1.0 / CONV_CH)
    s2 = jnp.zeros((ts, 1), F32)
    for c in range(nchunk):
        d = y_scr[c] - mu
        s2 = s2 + jnp.sum(d * d, axis=-1, keepdims=True)
    rstd = lax.rsqrt(s2 * (1.0 / CONV_CH) + LN_EPS)
    for c in range(nchunk):
        sl = slice(c * LANES, (c + 1) * LANES)
        z = (y_scr[c] - mu) * rstd * lg_ref[:, sl] + lb_ref[:, sl]
        o_ref[:, sl] = (z * jax.nn.sigmoid(z)).astype(o_ref.dtype)


def conv_mixer(proj, dw, dw_b, ln_g, ln_b):
    B, S, _ = proj.shape
    ts = min(CONV_TS, S)
    nchunk = CONV_CH // LANES
    hb = ts // CONV_HALO
    dw_p = jnp.zeros((CONV_HALO, CONV_CH), F32).at[:CONV_WIDTH].set(dw.astype(F32))
    dw_c = dw_p.reshape(CONV_HALO, nchunk, LANES).transpose(1, 0, 2)
    dwb_c = dw_b.astype(F32).reshape(nchunk, 1, LANES)
    halo_idx = lambda b, i: (b, jnp.maximum(i * hb - 1, 0), 0)
    halo_idx_g = lambda b, i: (b, jnp.maximum(i * hb - 1, 0), 1)
    return pl.pallas_call(
        _conv_kernel,
        grid=(B, S // ts),
        in_specs=[
            pl.BlockSpec((None, ts, CONV_CH), lambda b, i: (b, i, 0)),
            pl.BlockSpec((None, ts, CONV_CH), lambda b, i: (b, i, 1)),
            pl.BlockSpec((None, CONV_HALO, CONV_CH), halo_idx),
            pl.BlockSpec((None, CONV_HALO, CONV_CH), halo_idx_g),
            pl.BlockSpec((nchunk, CONV_HALO, LANES), lambda b, i: (0, 0, 0)),
            pl.BlockSpec((nchunk, 1, LANES), lambda b, i: (0, 0, 0)),
            pl.BlockSpec((1, CONV_CH), lambda b, i: (0, 0)),
            pl.BlockSpec((1, CONV_CH), lambda b, i: (0, 0)),
        ],
        out_specs=pl.BlockSpec((None, ts, CONV_CH), lambda b, i: (b, i, 0)),
        out_shape=jax.ShapeDtypeStruct((B, S, CONV_CH), BF16),
        scratch_shapes=[pltpu.VMEM((nchunk, CONV_HALO + ts, LANES), F32),
                        pltpu.VMEM((nchunk, ts, LANES), F32)],
        compiler_params=_cparams("arbitrary", "arbitrary"),
        name="conv_mixer",
    )(proj, proj, proj, proj, dw_c, dwb_c, ln_g.astype(F32).reshape(1, -1), ln_b.astype(F32).reshape(1, -1))


def _fgate_kernel(h_ref, w_ref, b_ref, c_ref, carry_ref):
    ts = h_ref.shape[0]

    @pl.when(pl.program_id(1) == 0)
    def _():
        carry_ref[...] = jnp.zeros_like(carry_ref)

    f = jnp.dot(h_ref[...], w_ref[...], preferred_element_type=F32) + b_ref[...]
    ls = jax.nn.log_sigmoid(f)
    row = lax.broadcasted_iota(jnp.int32, (ts, ts), 0)
    col = lax.broadcasted_iota(jnp.int32, (ts, ts), 1)
    tri = jnp.where(row >= col, 1.0, 0.0).astype(BF16)
    hi = ls.astype(BF16)
    r1 = ls - hi.astype(F32)
    mid = r1.astype(BF16)
    lo = (r1 - mid.astype(F32)).astype(BF16)
    cs = (jnp.dot(tri, hi, preferred_element_type=F32)
          + jnp.dot(tri, mid, preferred_element_type=F32)
          + jnp.dot(tri, lo, preferred_element_type=F32))
    c = cs + carry_ref[...]
    c_ref[...] = c
    carry_ref[...] = c[ts - 1:ts, :]


def forget_cumlog(h_bf, w_f, b_f):
    B, S, D = h_bf.shape
    H = w_f.shape[1]
    ts = min(256, S)
    w_p = jnp.zeros((D, LANES), BF16).at[:, :H].set(w_f.astype(BF16))
    b_p = jnp.zeros((1, LANES), F32).at[0, :H].set(b_f.astype(F32))
    return pl.pallas_call(
        _fgate_kernel,
        grid=(B, S // ts),
        in_specs=[pl.BlockSpec((None, ts, D), lambda b, i: (b, i, 0)),
                  pl.BlockSpec((D, LANES), lambda b, i: (0, 0)),
                  pl.BlockSpec((1, LANES), lambda b, i: (0, 0))],
        out_specs=pl.BlockSpec((None, ts, LANES), lambda b, i: (b, i, 0)),
        out_shape=jax.ShapeDtypeStruct((B, S, LANES), F32),
        scratch_shapes=[pltpu.VMEM((1, LANES), F32)],
        compiler_params=_cparams("arbitrary", "arbitrary"),
        name="forget_cumlog",
    )(h_bf, w_p, b_p)


def _fox_kernel(q_ref, k_ref, v_ref, cq_ref, ck_ref, o_ref):
    tq = q_ref.shape[0]
    i = pl.program_id(2)
    q = q_ref[...]
    cq = cq_ref[...]
    scale = HEAD_DIM ** -0.5

    def step(j, carry, masked):
        m, l, acc = carry
        start = pl.multiple_of(j * tq, tq)
        k = k_ref[pl.ds(start, tq), :]
        v = v_ref[pl.ds(start, tq), :]
        s = lax.dot_general(q, k, (((1,), (1,)), ((), ())), preferred_element_type=F32) * scale
        s = s + (cq - ck_ref[j])
        if masked:
            row = lax.broadcasted_iota(jnp.int32, (tq, tq), 0)
            col = lax.broadcasted_iota(jnp.int32, (tq, tq), 1)
            s = jnp.where(row >= col, s, -jnp.inf)
        m_new = jnp.maximum(m, jnp.max(s, axis=-1, keepdims=True))
        alpha = jnp.exp(m - m_new)
        p = jnp.exp(s - m_new)
        l = alpha * l + jnp.sum(p, axis=-1, keepdims=True)
        acc = alpha * acc + jnp.dot(p.astype(v.dtype), v, preferred_element_type=F32)
        return m_new, l, acc

    init = (jnp.full((tq, 1), -jnp.inf, F32), jnp.zeros((tq, 1), F32), jnp.zeros((tq, HEAD_DIM), F32))
    carry = lax.fori_loop(0, i, lambda j, c: step(j, c, False), init)
    m, l, acc = step(i, carry, True)
    o_ref[...] = (acc / l).astype(o_ref.dtype)


def fox_attention(proj, c):
    B, S, _ = proj.shape
    H = N_FOX_HEADS
    tq = min(FOX_TQ, S)
    nq = S // tq
    c_h = jnp.transpose(c[:, :, :H], (0, 2, 1))
    c_col = c_h.reshape(B, H, S, 1)
    c_row = c_h.reshape(B, H, nq, 1, tq)
    return pl.pallas_call(
        _fox_kernel,
        grid=(B, H, nq),
        in_specs=[
            pl.BlockSpec((None, tq, HEAD_DIM), lambda b, h, i: (b, i, h)),
            pl.BlockSpec((None, S, HEAD_DIM), lambda b, h, i: (b, 0, H + h)),
            pl.BlockSpec((None, S, HEAD_DIM), lambda b, h, i: (b, 0, 2 * H + h)),
            pl.BlockSpec((None, None, tq, 1), lambda b, h, i: (b, h, i, 0)),
            pl.BlockSpec((None, None, nq, 1, tq), lambda b, h, i: (b, h, 0, 0, 0)),
        ],
        out_specs=pl.BlockSpec((None, tq, HEAD_DIM), lambda b, h, i: (b, i, h)),
        out_shape=jax.ShapeDtypeStruct((B, S, PRIMARY_WIDTH), BF16),
        compiler_params=_cparams("arbitrary", "arbitrary", "arbitrary"),
        name="fox_attention",
    )(proj, proj, proj, c_col, c_row)


def _mem_attn_kernel(q_ref, kv_ref, o_ref):
    scale = HEAD_DIM ** -0.5
    for h in range(N_MEM_HEADS):
        sl = slice(h * HEAD_DIM, (h + 1) * HEAD_DIM)
        slv = slice(MEM_WIDTH + h * HEAD_DIM, MEM_WIDTH + (h + 1) * HEAD_DIM)
        s = lax.dot_general(q_ref[:, sl], kv_ref[:, sl], (((1,), (1,)), ((), ())),
                            preferred_element_type=F32) * scale
        m = jnp.max(s, axis=-1, keepdims=True)
        p = jnp.exp(s - m)
        l = jnp.sum(p, axis=-1, keepdims=True)
        o = jnp.dot(p.astype(BF16), kv_ref[:, slv], preferred_element_type=F32)
        o_ref[:, sl] = (o / l).astype(o_ref.dtype)


def memory_attention(proj, q_block, kv):
    B, S, _ = proj.shape
    M = kv.shape[1]
    tq = min(MEM_TQ, S)
    return pl.pallas_call(
        _mem_attn_kernel,
        grid=(B, S // tq),
        in_specs=[pl.BlockSpec((None, tq, MEM_WIDTH), lambda b, i: (b, i, q_block)),
                  pl.BlockSpec((None, M, 2 * MEM_WIDTH), lambda b, i: (b, 0, 0))],
        out_specs=pl.BlockSpec((None, tq, MEM_WIDTH), lambda b, i: (b, i, 0)),
        out_shape=jax.ShapeDtypeStruct((B, S, MEM_WIDTH), BF16),
        compiler_params=_cparams("arbitrary", "arbitrary"),
        name="memory_attention",
    )(proj, kv)


def _outproj_ln_kernel(p_ref, m_ref, wt_ref, wb_ref, h_ref, g_ref, b_ref, of_ref, ob_ref):
    mix = (jnp.dot(p_ref[...], wt_ref[...], preferred_element_type=F32)
           + jnp.dot(m_ref[...], wb_ref[...], preferred_element_type=F32))
    y = _ln_rows(DEEPNORM_ALPHA * h_ref[...] + mix, g_ref[...], b_ref[...])
    of_ref[...] = y
    ob_ref[...] = y.astype(BF16)


def outproj_ln(prim, memo, w_out, h, g, b):
    N, D = h.shape
    tm = min(ROW_TILE, N)
    wt = w_out[:PRIMARY_WIDTH].astype(BF16)
    wb = w_out[PRIMARY_WIDTH:].astype(BF16)
    row = lambda i: (i, 0)
    const = lambda i: (0, 0)
    return pl.pallas_call(
        _outproj_ln_kernel,
        grid=(N // tm,),
        in_specs=[pl.BlockSpec((tm, PRIMARY_WIDTH), row), pl.BlockSpec((tm, MEM_WIDTH), row),
                  pl.BlockSpec((PRIMARY_WIDTH, D), const), pl.BlockSpec((MEM_WIDTH, D), const),
                  pl.BlockSpec((tm, D), row), pl.BlockSpec((1, D), const), pl.BlockSpec((1, D), const)],
        out_specs=[pl.BlockSpec((tm, D), row), pl.BlockSpec((tm, D), row)],
        out_shape=[jax.ShapeDtypeStruct((N, D), F32), jax.ShapeDtypeStruct((N, D), BF16)],
        compiler_params=_cparams("arbitrary"),
        name="outproj_ln",
    )(prim, memo, wt, wb, h, g.astype(F32).reshape(1, D), b.astype(F32).reshape(1, D))


def _router_kernel(h_ref, w_ref, o_ref):
    logits = jnp.dot(h_ref[...], w_ref[...], preferred_element_type=F32, precision=lax.Precision.HIGHEST)
    o_ref[...] = jax.nn.sigmoid(logits)


def router_scores(h, w_r):
    N, D = h.shape
    E = w_r.shape[1]
    tm = min(ROW_TILE, N)
    return pl.pallas_call(
        _router_kernel,
        grid=(N // tm,),
        in_specs=[pl.BlockSpec((tm, D), lambda i: (i, 0)), pl.BlockSpec((D, E), lambda i: (0, 0))],
        out_specs=pl.BlockSpec((tm, E), lambda i: (i, 0)),
        out_shape=jax.ShapeDtypeStruct((N, E), F32),
        compiler_params=_cparams("arbitrary"),
        name="router_scores",
    )(h, w_r.astype(F32))


def _expert_kernel(be_ref, nu_ref, x_ref, wu_ref, wd_ref, o_ref, wu_bf, wd_bf):
    i = pl.program_id(0)
    f = wd_ref.shape[0]

    @pl.when(i < nu_ref[0])
    def _():
        changed = jnp.logical_or(i == 0, be_ref[i] != be_ref[jnp.maximum(i - 1, 0)])

        @pl.when(changed)
        def _():
            wu_bf[...] = wu_ref[...].astype(BF16)
            wd_bf[...] = wd_ref[...].astype(BF16)

        h = jnp.dot(x_ref[...], wu_bf[...], preferred_element_type=F32)
        a = h[:, :f]
        act = (a * jax.nn.sigmoid(a)) * h[:, f:]
        o_ref[...] = jnp.dot(act.astype(BF16), wd_bf[...], preferred_element_type=F32).astype(o_ref.dtype)

    @pl.when(i >= nu_ref[0])
    def _():
        o_ref[...] = jnp.zeros_like(o_ref)


def expert_ffn(x, block_expert, n_used, w_up, w_down):
    P, D = x.shape
    tm = min(EXP_TM, P)
    nb = P // tm
    F2 = w_up.shape[-1]
    Fd = w_down.shape[1]
    grid_spec = pltpu.PrefetchScalarGridSpec(
        num_scalar_prefetch=2,
        grid=(nb,),
        in_specs=[pl.BlockSpec((tm, D), lambda i, be, nu: (i, 0)),
                  pl.BlockSpec((None, D, F2), lambda i, be, nu: (be[i], 0, 0)),
                  pl.BlockSpec((None, Fd, D), lambda i, be, nu: (be[i], 0, 0))],
        out_specs=pl.BlockSpec((tm, D), lambda i, be, nu: (i, 0)),
        scratch_shapes=[pltpu.VMEM((D, F2), BF16), pltpu.VMEM((Fd, D), BF16)],
    )
    return pl.pallas_call(
        _expert_kernel,
        grid_spec=grid_spec,
        out_shape=jax.ShapeDtypeStruct((P, D), BF16),
        compiler_params=_cparams("arbitrary"),
        name="expert_ffn",
    )(block_expert.astype(jnp.int32), n_used.astype(jnp.int32).reshape(1), x, w_up, w_down)


def _ffn_ln_kernel(h_ref, r_ref, s_ref, g_ref, b_ref, of_ref, ob_ref):
    y = DEEPNORM_ALPHA * h_ref[...] + r_ref[...] + s_ref[...].astype(F32)
    y = _ln_rows(y, g_ref[...], b_ref[...])
    of_ref[...] = y
    ob_ref[...] = y.astype(BF16)


def ffn_ln(h, routed, shared, g, b):
    N, D = h.shape
    tm = min(ROW_TILE, N)
    row = lambda i: (i, 0)
    const = lambda i: (0, 0)
    return pl.pallas_call(
        _ffn_ln_kernel,
        grid=(N // tm,),
        in_specs=[pl.BlockSpec((tm, D), row), pl.BlockSpec((tm, D), row), pl.BlockSpec((tm, D), row),
                  pl.BlockSpec((1, D), const), pl.BlockSpec((1, D), const)],
        out_specs=[pl.BlockSpec((tm, D), row), pl.BlockSpec((tm, D), row)],
        out_shape=[jax.ShapeDtypeStruct((N, D), F32), jax.ShapeDtypeStruct((N, D), BF16)],
        compiler_params=_cparams("arbitrary"),
        name="ffn_ln",
    )(h, routed, shared, g.astype(F32).reshape(1, D), b.astype(F32).reshape(1, D))


def _route_select(scores, r_bias):
    N = scores.shape[0]
    choice = scores + r_bias.astype(F32)
    grp_score = lax.top_k(choice.reshape(N, N_GROUPS, -1), 2)[0].sum(-1)
    _, top_g = lax.top_k(grp_score, TOPK_GROUPS)
    gmask = jnp.any(top_g[..., None] == jnp.arange(N_GROUPS), axis=-2)
    emask = jnp.repeat(gmask, N_EXPERTS // N_GROUPS, axis=-1)
    _, idx = lax.top_k(jnp.where(emask, choice, -jnp.inf), TOP_K)
    w = jnp.take_along_axis(scores, idx, axis=-1)
    w = w / w.sum(-1, keepdims=True) * ROUTED_SCALE
    return idx, w


def _dispatch_tables(idx, tm):
    N = idx.shape[0]
    A = N * TOP_K
    n_blocks = A // tm + N_EXPERTS
    P = n_blocks * tm
    e_flat = idx.reshape(-1)
    order = jnp.argsort(e_flat)
    e_sorted = e_flat[order]
    counts = jnp.bincount(e_flat, length=N_EXPERTS)
    starts = jnp.cumsum(counts) - counts
    padded = (counts + tm - 1) // tm * tm
    pad_end = jnp.cumsum(padded)
    pad_start = pad_end - padded
    dest = (pad_start[e_sorted] + (jnp.arange(A) - starts[e_sorted])).astype(jnp.int32)
    slot_token = jnp.zeros((P,), jnp.int32).at[dest].set((order // TOP_K).astype(jnp.int32))
    pos = jnp.zeros((A,), jnp.int32).at[order].set(dest).reshape(N, TOP_K)
    block_expert = jnp.minimum(
        jnp.searchsorted(pad_end, jnp.arange(n_blocks) * tm, side='right'), N_EXPERTS - 1)
    n_used = pad_end[-1] // tm
    return slot_token, pos, block_expert, n_used


def moe_ffn_ln(h, h_bf, w_r, r_bias, w_up, w_down, sh_up, sh_down, g, b):
    N, D = h.shape
    scores = router_scores(h, w_r)
    idx, gate = _route_select(scores, r_bias)
    tm = min(EXP_TM, N)
    slot_token, pos, block_expert, n_used = _dispatch_tables(idx, tm)
    xs = jnp.take(h_bf, slot_token, axis=0)
    y = expert_ffn(xs, block_expert, n_used, w_up, w_down)
    routed = jnp.einsum('nk,nkd->nd', gate, jnp.take(y, pos, axis=0).astype(F32))
    nb = N // tm
    shared = expert_ffn(h_bf, jnp.zeros((nb,), jnp.int32), jnp.array(nb, jnp.int32), sh_up[None], sh_down[None])
    return ffn_ln(h, routed, shared, g, b)


def kernel(x, mem, conv_w_in, conv_dw, conv_dw_b, conv_ln_g, conv_ln_b, fox_w_in, fox_b_f, mem_w_kv, w_out,
           ln_g, ln_b, router_w, router_bias, exp_w_up, exp_w_down, shared_w_up, shared_w_down):
    B, S, D = x.shape
    N = B * S
    h = x.reshape(N, D).astype(F32)
    h_bf = h.astype(BF16)
    mem_bf = mem.reshape(B * MEM_LEN, D).astype(BF16)
    for i in range(DEPTH):
        j = i // N_MIXERS
        if i % N_MIXERS == 0:
            proj = matmul(h_bf, conv_w_in[j].astype(BF16), BF16).reshape(B, S, -1)
            prim = conv_mixer(proj, conv_dw[j], conv_dw_b[j], conv_ln_g[j], conv_ln_b[j])
            q_block = 2 * CONV_CH // MEM_WIDTH
        else:
            w = fox_w_in[j]
            nf = 3 * PRIMARY_WIDTH
            w_main = jnp.concatenate([w[:, :nf], w[:, nf + N_FOX_HEADS:]], axis=1).astype(BF16)
            proj = matmul(h_bf, w_main, BF16).reshape(B, S, -1)
            c = forget_cumlog(h_bf.reshape(B, S, D), w[:, nf:nf + N_FOX_HEADS], fox_b_f[j])
            prim = fox_attention(proj, c)
            q_block = nf // MEM_WIDTH
        kv = matmul(mem_bf, mem_w_kv[i].astype(BF16), BF16).reshape(B, MEM_LEN, 2 * MEM_WIDTH)
        memo = memory_attention(proj, q_block, kv)
        h, h_bf = outproj_ln(prim.reshape(N, PRIMARY_WIDTH), memo.reshape(N, MEM_WIDTH), w_out[i], h,
                             ln_g[i, 0], ln_b[i, 0])
        h, h_bf = moe_ffn_ln(h, h_bf, router_w[i], router_bias[i], exp_w_up[i], exp_w_down[i],
                             shared_w_up[i], shared_w_down[i], ln_g[i, 1], ln_b[i, 1])
    return h.reshape(B, S, D)
```

```python
import functools

import jax
import jax.numpy as jnp
from jax import lax
from jax.experimental import pallas as pl
from jax.experimental.pallas import tpu as pltpu

F32 = jnp.float32
BF16 = jnp.bfloat16

D_MODEL = 2048
DEPTH = 4
N_MIXERS = 2
MEM_LEN = 256
HEAD_DIM = 128
N_MEM_HEADS = 4
MEM_WIDTH = N_MEM_HEADS * HEAD_DIM
PRIMARY_WIDTH = D_MODEL - MEM_WIDTH
CONV_CH = PRIMARY_WIDTH
CONV_WIDTH = 31
N_FOX_HEADS = PRIMARY_WIDTH // HEAD_DIM
N_EXPERTS = 64
TOP_K = 8
N_GROUPS = 8
TOPK_GROUPS = 4
D_EXPERT = 512
ROUTED_SCALE = 2.5
LN_EPS = 1e-5
DEEPNORM_ALPHA = (2 * DEPTH) ** 0.25
LOG2E = 1.4426950408889634

LANES = 128
VMEM_LIMIT = 56 * 1024 * 1024
MM_TM, MM_TN = 1024, 512
ROW_TILE = 256
CONV_TS = 256
CONV_HALO = 32
CONV_SUB = 64
FOX_TQ = 256
FOX_TK = 512
FOX_HP = 2
MEM_TQ = 512
EXP_TM = 256
ROW_CHUNKS = D_MODEL // LANES
ROUTE_T = 256
DISPATCH_T = 256
COMBINE_T = 128


def _cparams(*sem):
    return pltpu.CompilerParams(dimension_semantics=sem, vmem_limit_bytes=VMEM_LIMIT)


def _ln_rows(y, g, b):
    mu = jnp.mean(y, axis=-1, keepdims=True)
    d = y - mu
    var = jnp.mean(d * d, axis=-1, keepdims=True)
    return d * lax.rsqrt(var + LN_EPS) * g + b


def _mm_kernel(a_ref, w_ref, o_ref):
    o_ref[...] = jnp.dot(a_ref[...], w_ref[...], preferred_element_type=F32).astype(o_ref.dtype)


def matmul(a, w, out_dtype, tm=MM_TM, tn=MM_TN):
    M, K = a.shape
    N = w.shape[1]
    tm, tn = min(tm, M), min(tn, N)
    assert M % tm == 0 and N % tn == 0
    return pl.pallas_call(
        _mm_kernel,
        grid=(M // tm, N // tn),
        in_specs=[pl.BlockSpec((tm, K), lambda i, j: (i, 0)),
                  pl.BlockSpec((K, tn), lambda i, j: (0, j))],
        out_specs=pl.BlockSpec((tm, tn), lambda i, j: (i, j)),
        out_shape=jax.ShapeDtypeStruct((M, N), out_dtype),
        compiler_params=_cparams("arbitrary", "arbitrary"),
        name="matmul",
    )(a, w)


def _conv_kernel(a_ref, g_ref, ah_ref, gh_ref, dw_ref, dwb_ref, lg_ref, lb_ref, o_ref, u_scr, y_scr):
    ts = a_ref.shape[0]
    nchunk = CONV_CH // LANES
    first = pl.program_id(1) == 0
    uh = ah_ref[...].astype(F32) * jax.nn.sigmoid(gh_ref[...].astype(F32))
    uh = jnp.where(first, 0.0, uh)
    u = a_ref[...].astype(F32) * jax.nn.sigmoid(g_ref[...].astype(F32))
    for c in range(nchunk):
        sl = slice(c * LANES, (c + 1) * LANES)
        u_scr[c, 0:CONV_HALO, :] = uh[:, sl]
        u_scr[c, CONV_HALO:CONV_HALO + ts, :] = u[:, sl]

    off = CONV_HALO - (CONV_WIDTH - 1)

    def chunk_body(c, carry):
        w = dw_ref[c]
        bias = dwb_ref[c]
        for r in range(ts // CONV_SUB):
            acc = jnp.broadcast_to(bias, (CONV_SUB, LANES))
            for j in range(CONV_WIDTH):
                acc = acc + w[j:j + 1, :] * u_scr[c, pl.ds(off + j + r * CONV_SUB, CONV_SUB), :]
            y_scr[c, r * CONV_SUB:(r + 1) * CONV_SUB, :] = acc
        return carry

    lax.fori_loop(0, nchunk, chunk_body, 0)

    s1 = jnp.zeros((ts, 1), F32)
    for c in range(nchunk):
        s1 = s1 + jnp.sum(y_scr[c], axis=-1, keepdims=True)
    mu = s1 * (1.0 / CONV_CH)
    s2 = jnp.zeros((ts, 1), F32)
    for c in range(nchunk):
        d = y_scr[c] - mu
        s2 = s2 + jnp.sum(d * d, axis=-1, keepdims=True)
    rstd = lax.rsqrt(s2 * (1.0 / CONV_CH) + LN_EPS)
    for c in range(nchunk):
        sl = slice(c * LANES, (c + 1) * LANES)
        z = (y_scr[c] - mu) * rstd * lg_ref[:, sl] + lb_ref[:, sl]
        o_ref[:, sl] = (z * jax.nn.sigmoid(z)).astype(o_ref.dtype)


def conv_mixer(proj, dw, dw_b, ln_g, ln_b):
    B, S, _ = proj.shape
    ts = min(CONV_TS, S)
    nchunk = CONV_CH // LANES
    hb = ts // CONV_HALO
    dw_p = jnp.zeros((CONV_HALO, CONV_CH), F32).at[:CONV_WIDTH].set(dw.astype(F32))
    dw_c = dw_p.reshape(CONV_HALO, nchunk, LANES).transpose(1, 0, 2)
    dwb_c = dw_b.astype(F32).reshape(nchunk, 1, LANES)
    halo_idx = lambda b, i: (b, jnp.maximum(i * hb - 1, 0), 0)
    halo_idx_g = lambda b, i: (b, jnp.maximum(i * hb - 1, 0), 1)
    return pl.pallas_call(
        _conv_kernel,
        grid=(B, S // ts),
        in_specs=[
            pl.BlockSpec((None, ts, CONV_CH), lambda b, i: (b, i, 0)),
            pl.BlockSpec((None, ts, CONV_CH), lambda b, i: (b, i, 1)),
            pl.BlockSpec((None, CONV_HALO, CONV_CH), halo_idx),
            pl.BlockSpec((None, CONV_HALO, CONV_CH), halo_idx_g),
            pl.BlockSpec((nchunk, CONV_HALO, LANES), lambda b, i: (0, 0, 0)),
            pl.BlockSpec((nchunk, 1, LANES), lambda b, i: (0, 0, 0)),
            pl.BlockSpec((1, CONV_CH), lambda b, i: (0, 0)),
            pl.BlockSpec((1, CONV_CH), lambda b, i: (0, 0)),
        ],
        out_specs=pl.BlockSpec((None, ts, CONV_CH), lambda b, i: (b, i, 0)),
        out_shape=jax.ShapeDtypeStruct((B, S, CONV_CH), BF16),
        scratch_shapes=[pltpu.VMEM((nchunk, CONV_HALO + ts, LANES), F32),
                        pltpu.VMEM((nchunk, ts, LANES), F32)],
        compiler_params=_cparams("arbitrary", "arbitrary"),
        name="conv_mixer",
    )(proj, proj, proj, proj, dw_c, dwb_c, ln_g.astype(F32).reshape(1, -1), ln_b.astype(F32).reshape(1, -1))


def _fgate_kernel(h_ref, w_ref, b_ref, c_ref, carry_ref):
    ts = h_ref.shape[0]

    @pl.when(pl.program_id(1) == 0)
    def _():
        carry_ref[...] = jnp.zeros_like(carry_ref)

    f = jnp.dot(h_ref[...], w_ref[...], preferred_element_type=F32) + b_ref[...]
    ls = jax.nn.log_sigmoid(f)
    row = lax.broadcasted_iota(jnp.int32, (ts, ts), 0)
    col = lax.broadcasted_iota(jnp.int32, (ts, ts), 1)
    tri = jnp.where(row >= col, 1.0, 0.0).astype(BF16)
    hi = ls.astype(BF16)
    r1 = ls - hi.astype(F32)
    mid = r1.astype(BF16)
    lo = (r1 - mid.astype(F32)).astype(BF16)
    cs = (jnp.dot(tri, hi, preferred_element_type=F32)
          + jnp.dot(tri, mid, preferred_element_type=F32)
          + jnp.dot(tri, lo, preferred_element_type=F32))
    c = cs + carry_ref[...]
    c_ref[...] = c
    carry_ref[...] = c[ts - 1:ts, :]


def forget_cumlog(h_bf, w_f, b_f):
    B, S, D = h_bf.shape
    H = w_f.shape[1]
    ts = min(256, S)
    w_p = jnp.zeros((D, LANES), BF16).at[:, :H].set(w_f.astype(BF16))
    b_p = jnp.zeros((1, LANES), F32).at[0, :H].set(b_f.astype(F32))
    return pl.pallas_call(
        _fgate_kernel,
        grid=(B, S // ts),
        in_specs=[pl.BlockSpec((None, ts, D), lambda b, i: (b, i, 0)),
                  pl.BlockSpec((D, LANES), lambda b, i: (0, 0)),
                  pl.BlockSpec((1, LANES), lambda b, i: (0, 0))],
        out_specs=pl.BlockSpec((None, ts, LANES), lambda b, i: (b, i, 0)),
        out_shape=jax.ShapeDtypeStruct((B, S, LANES), F32),
        scratch_shapes=[pltpu.VMEM((1, LANES), F32)],
        compiler_params=_cparams("arbitrary", "arbitrary"),
        name="forget_cumlog",
    )(h_bf, w_p, b_p)


def _fox_kernel(q_ref, k_ref, v_ref, cq_ref, ck_ref, o_ref):
    tq = q_ref.shape[0]
    tk = ck_ref.shape[-1]
    i = pl.program_id(2)
    qscale = HEAD_DIM ** -0.5 * LOG2E
    heads = range(FOX_HP)
    hs = [slice(h * HEAD_DIM, (h + 1) * HEAD_DIM) for h in heads]
    qs = [(q_ref[:, hs[h]].astype(F32) * qscale).astype(BF16) for h in heads]
    cqs = [cq_ref[h] for h in heads]

    def step(j, carry, masked):
        start = pl.multiple_of(j * tk, tk)
        ss = []
        for h in heads:
            k = k_ref[pl.ds(start, tk), hs[h]]
            ss.append(lax.dot_general(qs[h], k, (((1,), (1,)), ((), ())), preferred_element_type=F32))
        if masked:
            row = i * tq + lax.broadcasted_iota(jnp.int32, (tq, tk), 0)
            col = j * tk + lax.broadcasted_iota(jnp.int32, (tq, tk), 1)
            causal = row >= col
        ps, ms, alphas = [], [], []
        for h in heads:
            m = carry[h][0]
            s = ss[h] - ck_ref[h, j]
            if masked:
                s = jnp.where(causal, s, -jnp.inf)
            m_new = jnp.maximum(m, jnp.max(s, axis=-1, keepdims=True) + cqs[h])
            p = jnp.exp2(s - (m_new - cqs[h]))
            alpha = jnp.exp2(m - m_new)
            ps.append(p.astype(BF16))
            ms.append(m_new)
            alphas.append(alpha)
        out = []
        for h in heads:
            v1 = jnp.concatenate([v_ref[pl.ds(start, tk), hs[h]], ones], axis=-1)
            acc = alphas[h] * carry[h][1] + jnp.dot(ps[h], v1, preferred_element_type=F32)
            out.append((ms[h], acc))
        return tuple(out)

    ones = jnp.ones((tk, HEAD_DIM), BF16)
    init = tuple((jnp.full((tq, 1), -jnp.inf, F32), jnp.zeros((tq, 2 * HEAD_DIM), F32)) for _ in heads)
    n_full = (i * tq) // tk
    carry = lax.fori_loop(0, n_full, lambda j, c: step(j, c, False), init)
    for d in range(-(-tq // tk)):
        carry = step(n_full + d, carry, True)
    for h in heads:
        acc = carry[h][1]
        o_ref[:, hs[h]] = (acc[:, :HEAD_DIM] / acc[:, HEAD_DIM:]).astype(o_ref.dtype)


def fox_attention(proj, c):
    B, S, _ = proj.shape
    H = N_FOX_HEADS
    tq, tk = min(FOX_TQ, S), min(FOX_TK, S)
    nq, nk = S // tq, S // tk
    ng = H // FOX_HP
    w = FOX_HP * HEAD_DIM
    c_h = jnp.transpose(c[:, :, :H], (0, 2, 1)) * LOG2E
    c_col = c_h.reshape(B, H, S, 1)
    c_row = c_h.reshape(B, H, nk, 1, tk)
    return pl.pallas_call(
        _fox_kernel,
        grid=(B, ng, nq),
        in_specs=[
            pl.BlockSpec((None, tq, w), lambda b, g, i: (b, i, g)),
            pl.BlockSpec((None, S, w), lambda b, g, i: (b, 0, ng + g)),
            pl.BlockSpec((None, S, w), lambda b, g, i: (b, 0, 2 * ng + g)),
            pl.BlockSpec((None, FOX_HP, tq, 1), lambda b, g, i: (b, g, i, 0)),
            pl.BlockSpec((None, FOX_HP, nk, 1, tk), lambda b, g, i: (b, g, 0, 0, 0)),
        ],
        out_specs=pl.BlockSpec((None, tq, w), lambda b, g, i: (b, i, g)),
        out_shape=jax.ShapeDtypeStruct((B, S, PRIMARY_WIDTH), BF16),
        compiler_params=_cparams("arbitrary", "arbitrary", "arbitrary"),
        name="fox_attention",
    )(proj, proj, proj, c_col, c_row)


def _mem_attn_kernel(q_ref, kv_ref, o_ref):
    scale = HEAD_DIM ** -0.5
    for h in range(N_MEM_HEADS):
        sl = slice(h * HEAD_DIM, (h + 1) * HEAD_DIM)
        slv = slice(MEM_WIDTH + h * HEAD_DIM, MEM_WIDTH + (h + 1) * HEAD_DIM)
        s = lax.dot_general(q_ref[:, sl], kv_ref[:, sl], (((1,), (1,)), ((), ())),
                            preferred_element_type=F32) * scale
        m = jnp.max(s, axis=-1, keepdims=True)
        p = jnp.exp(s - m)
        l = jnp.sum(p, axis=-1, keepdims=True)
        o = jnp.dot(p.astype(BF16), kv_ref[:, slv], preferred_element_type=F32)
        o_ref[:, sl] = (o / l).astype(o_ref.dtype)


def memory_attention(proj, q_block, kv):
    B, S, _ = proj.shape
    M = kv.shape[1]
    tq = min(MEM_TQ, S)
    return pl.pallas_call(
        _mem_attn_kernel,
        grid=(B, S // tq),
        in_specs=[pl.BlockSpec((None, tq, MEM_WIDTH), lambda b, i: (b, i, q_block)),
                  pl.BlockSpec((None, M, 2 * MEM_WIDTH), lambda b, i: (b, 0, 0))],
        out_specs=pl.BlockSpec((None, tq, MEM_WIDTH), lambda b, i: (b, i, 0)),
        out_shape=jax.ShapeDtypeStruct((B, S, MEM_WIDTH), BF16),
        compiler_params=_cparams("arbitrary", "arbitrary"),
        name="memory_attention",
    )(proj, kv)


def _outproj_ln_kernel(p_ref, m_ref, wt_ref, wb_ref, h_ref, g_ref, b_ref, of_ref, ob_ref):
    mix = (jnp.dot(p_ref[...], wt_ref[...], preferred_element_type=F32)
           + jnp.dot(m_ref[...], wb_ref[...], preferred_element_type=F32))
    y = _ln_rows(DEEPNORM_ALPHA * h_ref[...] + mix, g_ref[...], b_ref[...])
    of_ref[...] = y
    ob_ref[...] = y.astype(BF16)


def outproj_ln(prim, memo, w_out, h, g, b):
    N, D = h.shape
    tm = min(ROW_TILE, N)
    wt = w_out[:PRIMARY_WIDTH].astype(BF16)
    wb = w_out[PRIMARY_WIDTH:].astype(BF16)
    row = lambda i: (i, 0)
    const = lambda i: (0, 0)
    return pl.pallas_call(
        _outproj_ln_kernel,
        grid=(N // tm,),
        in_specs=[pl.BlockSpec((tm, PRIMARY_WIDTH), row), pl.BlockSpec((tm, MEM_WIDTH), row),
                  pl.BlockSpec((PRIMARY_WIDTH, D), const), pl.BlockSpec((MEM_WIDTH, D), const),
                  pl.BlockSpec((tm, D), row), pl.BlockSpec((1, D), const), pl.BlockSpec((1, D), const)],
        out_specs=[pl.BlockSpec((tm, D), row), pl.BlockSpec((tm, D), row)],
        out_shape=[jax.ShapeDtypeStruct((N, D), F32), jax.ShapeDtypeStruct((N, D), BF16)],
        compiler_params=_cparams("arbitrary"),
        name="outproj_ln",
    )(prim, memo, wt, wb, h, g.astype(F32).reshape(1, D), b.astype(F32).reshape(1, D))


def _route_kernel(h_ref, w_ref, b_ref, idx_ref, rank_ref, gate_ref, cnt_ref, carry_ref):
    T = h_ref.shape[0]
    E = w_ref.shape[0]
    gsz = E // N_GROUPS
    neg = -jnp.inf

    @pl.when(pl.program_id(0) == 0)
    def _():
        carry_ref[...] = jnp.zeros_like(carry_ref)

    logits = lax.dot_general(w_ref[...], h_ref[...], (((1,), (1,)), ((), ())),
                             precision=lax.Precision.HIGHEST, preferred_element_type=F32)
    scores = jax.nn.sigmoid(logits)
    choice = scores + b_ref[...]

    io8 = lax.broadcasted_iota(jnp.int32, (gsz, T), 0)
    rows = []
    for g in range(N_GROUPS):
        blk = choice[g * gsz:(g + 1) * gsz, :]
        m1 = jnp.max(blk, axis=0, keepdims=True)
        f1 = jnp.min(jnp.where(blk == m1, io8, gsz), axis=0, keepdims=True)
        m2 = jnp.max(jnp.where(io8 == f1, neg, blk), axis=0, keepdims=True)
        rows.append(jnp.broadcast_to(m1 + m2, (gsz, T)))
    gs = jnp.concatenate(rows, axis=0)
    eid = lax.broadcasted_iota(jnp.int32, (E, T), 0)
    gid = eid // gsz
    emask = jnp.zeros((E, T), jnp.bool_)
    for _ in range(TOPK_GROUPS):
        m = jnp.max(gs, axis=0, keepdims=True)
        g = jnp.min(jnp.where(gs == m, gid, N_GROUPS), axis=0, keepdims=True)
        hit = gid == g
        emask = jnp.logical_or(emask, hit)
        gs = jnp.where(hit, neg, gs)

    masked = jnp.where(emask, choice, neg)
    sel = jnp.zeros((E, T), jnp.bool_)
    ids, ws, hits = [], [], []
    for _ in range(TOP_K):
        m = jnp.max(masked, axis=0, keepdims=True)
        i_k = jnp.min(jnp.where(masked == m, eid, E), axis=0, keepdims=True)
        hit = eid == i_k
        ids.append(i_k)
        ws.append(jnp.sum(jnp.where(hit, scores, 0.0), axis=0, keepdims=True))
        hits.append(hit)
        masked = jnp.where(hit, neg, masked)
        sel = jnp.logical_or(sel, hit)
    wsum = ws[0]
    for w in ws[1:]:
        wsum = wsum + w

    sel_f = jnp.where(sel, 1.0, 0.0)
    r = lax.broadcasted_iota(jnp.int32, (T, T), 0)
    c = lax.broadcasted_iota(jnp.int32, (T, T), 1)
    tri = jnp.where(r < c, 1.0, 0.0).astype(BF16)
    before = jnp.dot(sel_f.astype(BF16), tri, preferred_element_type=F32) + carry_ref[...]
    for k in range(TOP_K):
        idx_ref[k:k + 1, :] = ids[k]
        rank_ref[k:k + 1, :] = jnp.sum(jnp.where(hits[k], before, 0.0), axis=0, keepdims=True).astype(jnp.int32)
        gate_ref[k:k + 1, :] = ws[k] / wsum * ROUTED_SCALE
    total = carry_ref[...] + jnp.sum(sel_f, axis=1, keepdims=True)
    carry_ref[...] = total
    cnt_ref[...] = jnp.broadcast_to(total, cnt_ref.shape).astype(jnp.int32)


def route(h, w_r, r_bias):
    N, D = h.shape
    E = w_r.shape[1]
    T = min(ROUTE_T, N)
    kn = lambda i: (0, i)
    idx, rank, gate, cnt = pl.pallas_call(
        _route_kernel,
        grid=(N // T,),
        in_specs=[pl.BlockSpec((T, D), lambda i: (i, 0)), pl.BlockSpec((E, D), lambda i: (0, 0)),
                  pl.BlockSpec((E, 1), lambda i: (0, 0))],
        out_specs=[pl.BlockSpec((TOP_K, T), kn), pl.BlockSpec((TOP_K, T), kn), pl.BlockSpec((TOP_K, T), kn),
                   pl.BlockSpec((E, LANES), lambda i: (0, 0))],
        out_shape=[jax.ShapeDtypeStruct((TOP_K, N), jnp.int32), jax.ShapeDtypeStruct((TOP_K, N), jnp.int32),
                   jax.ShapeDtypeStruct((TOP_K, N), F32), jax.ShapeDtypeStruct((E, LANES), jnp.int32)],
        scratch_shapes=[pltpu.VMEM((E, 1), F32)],
        compiler_params=_cparams("arbitrary"),
        name="route",
    )(h, w_r.astype(F32).T, r_bias.astype(F32).reshape(E, 1))
    return idx, rank, gate, cnt[:, 0]


def _pos_kernel(starts_ref, idx_ref, rank_ref, pos_ref):
    idx = idx_ref[...]
    acc = rank_ref[...]
    for e in range(N_EXPERTS):
        acc = acc + jnp.where(idx == e, starts_ref[e], 0)
    pos_ref[...] = acc


def slot_positions(starts, idx, rank):
    K, N = idx.shape
    T = min(2048, N)
    grid_spec = pltpu.PrefetchScalarGridSpec(
        num_scalar_prefetch=1, grid=(N // T,),
        in_specs=[pl.BlockSpec((K, T), lambda i, s: (0, i)), pl.BlockSpec((K, T), lambda i, s: (0, i))],
        out_specs=pl.BlockSpec((K, T), lambda i, s: (0, i)))
    return pl.pallas_call(
        _pos_kernel, grid_spec=grid_spec, out_shape=jax.ShapeDtypeStruct((K, N), jnp.int32),
        compiler_params=_cparams("arbitrary"), name="slot_positions",
    )(starts.astype(jnp.int32), idx, rank)


def _tiles(a, T):
    K, N = a.shape
    return a.reshape(K, N // T, T).transpose(1, 0, 2)


def _swiglu_bf(x, wu, wd):
    f = wd.shape[0]
    h = jnp.dot(x, wu, preferred_element_type=F32)
    a = h[:, :f]
    act = (a * jax.nn.sigmoid(a)) * h[:, f:]
    return jnp.dot(act.astype(BF16), wd, preferred_element_type=F32)


def _dispatch_kernel(pos_ref, hf_ref, hb_ref, su_ref, sd_ref, sh_ref, xs_hbm, slab, su_bf, sd_bf, sem):
    T = hf_ref.shape[0]

    @pl.when(pl.program_id(0) == 0)
    def _():
        su_bf[...] = su_ref[...].astype(BF16)
        sd_bf[...] = sd_ref[...].astype(BF16)

    hf = hf_ref[...]
    for s in range(ROW_CHUNKS):
        slab[pl.ds(s, T, stride=ROW_CHUNKS), :] = hf[:, s * LANES:(s + 1) * LANES]

    def row_copy(t, p):
        return pltpu.make_async_copy(slab.at[pl.ds(pl.multiple_of(t * ROW_CHUNKS, ROW_CHUNKS), ROW_CHUNKS)],
                                     xs_hbm.at[pl.ds(pl.multiple_of(p * ROW_CHUNKS, ROW_CHUNKS), ROW_CHUNKS)], sem)

    def issue(t, carry):
        for k in range(TOP_K):
            row_copy(t, pos_ref[k, t]).start()
        return carry

    lax.fori_loop(0, T, issue, 0)
    sh_ref[...] = _swiglu_bf(hb_ref[...], su_bf[...], sd_bf[...]).astype(sh_ref.dtype)
    for k in range(TOP_K):
        pltpu.make_async_copy(slab, xs_hbm.at[pl.ds(0, T * ROW_CHUNKS)], sem).wait()


def dispatch_shared(pos_t, h, h_bf, sh_up, sh_down, layer):
    N, D = h.shape
    nt, K, T = pos_t.shape
    F2 = sh_up.shape[-1]
    Fd = sh_down.shape[1]
    return pl.pallas_call(
        _dispatch_kernel,
        grid=(nt,),
        in_specs=[pl.BlockSpec((None, K, T), lambda i: (i, 0, 0), memory_space=pltpu.SMEM),
                  pl.BlockSpec((T, D), lambda i: (i, 0)),
                  pl.BlockSpec((T, D), lambda i: (i, 0)),
                  pl.BlockSpec((None, D, F2), lambda i: (layer, 0, 0)),
                  pl.BlockSpec((None, Fd, D), lambda i: (layer, 0, 0))],
        out_specs=[pl.BlockSpec((T, D), lambda i: (i, 0)), pl.BlockSpec(memory_space=pl.ANY)],
        out_shape=[jax.ShapeDtypeStruct((N, D), BF16),
                   jax.ShapeDtypeStruct((N * K * ROW_CHUNKS, LANES), F32)],
        scratch_shapes=[pltpu.VMEM((T * ROW_CHUNKS, LANES), F32), pltpu.VMEM((D, F2), BF16),
                        pltpu.VMEM((Fd, D), BF16), pltpu.SemaphoreType.DMA(())],
        compiler_params=_cparams("arbitrary"),
        name="dispatch_shared",
    )(pos_t, h, h_bf, sh_up, sh_down)


def _expert_kernel(tile_ref, be_ref, lo_ref, hi_ref, first_ref, x_ref, wu_ref, wd_ref, o_ref, wu_bf, wd_bf):
    v = pl.program_id(0)
    tm = x_ref.shape[0] // ROW_CHUNKS
    lo = lo_ref[v]
    hi = hi_ref[v]

    @pl.when(hi > lo)
    def _():
        changed = jnp.logical_or(v == 0, be_ref[v] != be_ref[jnp.maximum(v - 1, 0)])

        @pl.when(changed)
        def _():
            wu_bf[...] = wu_ref[...].astype(BF16)
            wd_bf[...] = wd_ref[...].astype(BF16)

        x = jnp.concatenate([x_ref[pl.ds(s, tm, stride=ROW_CHUNKS), :] for s in range(ROW_CHUNKS)], axis=-1)
        y = _swiglu_bf(x.astype(BF16), wu_bf[...], wd_bf[...])
        row = lax.broadcasted_iota(jnp.int32, (tm, 1), 0)
        mine = jnp.logical_and(row >= lo, row < hi)

        @pl.when(first_ref[v] == 1)
        def _():
            for s in range(ROW_CHUNKS):
                o_ref[pl.ds(s, tm, stride=ROW_CHUNKS), :] = jnp.where(mine, y[:, s * LANES:(s + 1) * LANES], 0.0)

        @pl.when(first_ref[v] == 0)
        def _():
            for s in range(ROW_CHUNKS):
                keep = o_ref[pl.ds(s, tm, stride=ROW_CHUNKS), :]
                o_ref[pl.ds(s, tm, stride=ROW_CHUNKS), :] = jnp.where(mine, y[:, s * LANES:(s + 1) * LANES], keep)


def _lookup(table, idx):
    n = table.shape[0]
    return jnp.sum(jnp.where(idx[:, None] == jnp.arange(n)[None, :], table[None, :], 0), axis=1)


def _visit_tables(counts, tm, A):
    E = counts.shape[0]
    V = A // tm + E
    starts = jnp.cumsum(counts) - counts
    ends = starts + counts
    first_tile = starts // tm
    nvis = jnp.where(counts > 0, (ends - 1) // tm - first_tile + 1, 0)
    vend = jnp.cumsum(nvis)
    voff = vend - nvis
    total = vend[-1]
    v = jnp.arange(V, dtype=jnp.int32)
    e_v = jnp.minimum(jnp.sum(vend[None, :] <= v[:, None], axis=1), E - 1).astype(jnp.int32)
    valid = v < total
    e_last = jnp.sum(jnp.where(v == total - 1, e_v, 0))
    e_v = jnp.where(valid, e_v, e_last)
    tile_v = jnp.where(valid, _lookup(first_tile, e_v) + v - _lookup(voff, e_v), A // tm - 1)
    base = tile_v * tm
    lo_v = jnp.where(valid, jnp.maximum(_lookup(starts, e_v), base) - base, 0)
    hi_v = jnp.where(valid, jnp.minimum(_lookup(ends, e_v), base + tm) - base, 0)
    prev_tile = jnp.concatenate([jnp.full((1,), -1, tile_v.dtype), tile_v[:-1]])
    first_v = jnp.logical_and(valid, tile_v != prev_tile)
    i32 = lambda a: a.astype(jnp.int32)
    return starts, (i32(tile_v), i32(e_v), i32(lo_v), i32(hi_v), i32(first_v))


def expert_ffn(xs, tables, w_up, w_down, layer):
    tile_v, e_v, lo_v, hi_v, first_v = tables
    V = tile_v.shape[0]
    D, F2 = w_up.shape[-2:]
    Fd = w_down.shape[-2]
    tm = EXP_TM
    blk = tm * ROW_CHUNKS
    grid_spec = pltpu.PrefetchScalarGridSpec(
        num_scalar_prefetch=5,
        grid=(V,),
        in_specs=[pl.BlockSpec((blk, LANES), lambda v, t, e, lo, hi, f: (t[v], 0)),
                  pl.BlockSpec((None, None, D, F2), lambda v, t, e, lo, hi, f: (layer, e[v], 0, 0)),
                  pl.BlockSpec((None, None, Fd, D), lambda v, t, e, lo, hi, f: (layer, e[v], 0, 0))],
        out_specs=pl.BlockSpec((blk, LANES), lambda v, t, e, lo, hi, f: (t[v], 0)),
        scratch_shapes=[pltpu.VMEM((D, F2), BF16), pltpu.VMEM((Fd, D), BF16)],
    )
    return pl.pallas_call(
        _expert_kernel,
        grid_spec=grid_spec,
        out_shape=jax.ShapeDtypeStruct(xs.shape, F32),
        compiler_params=_cparams("arbitrary"),
        name="expert_ffn",
    )(tile_v, e_v, lo_v, hi_v, first_v, xs, w_up, w_down)


def _combine_ln_kernel(pos_ref, gate_ref, h_ref, sh_ref, g_ref, b_ref, y_hbm, of_ref, ob_ref, ybuf, acc_slab, sem):
    T = h_ref.shape[0]

    def issue(t, carry):
        dst0 = pl.multiple_of(t * ROW_CHUNKS, ROW_CHUNKS)
        for k in range(TOP_K):
            src0 = pl.multiple_of(pos_ref[k, t] * ROW_CHUNKS, ROW_CHUNKS)
            pltpu.make_async_copy(y_hbm.at[pl.ds(src0, ROW_CHUNKS)], ybuf.at[k, pl.ds(dst0, ROW_CHUNKS)], sem).start()
        return carry

    lax.fori_loop(0, T, issue, 0)
    for k in range(TOP_K):
        pltpu.make_async_copy(y_hbm.at[pl.ds(0, T * ROW_CHUNKS)], ybuf.at[k], sem).wait()

    def mix(t, carry):
        r0 = pl.multiple_of(t * ROW_CHUNKS, ROW_CHUNKS)
        acc = gate_ref[0, t] * ybuf[0, pl.ds(r0, ROW_CHUNKS), :]
        for k in range(1, TOP_K):
            acc = acc + gate_ref[k, t] * ybuf[k, pl.ds(r0, ROW_CHUNKS), :]
        acc_slab[pl.ds(r0, ROW_CHUNKS), :] = acc
        return carry

    lax.fori_loop(0, T, mix, 0)
    routed = jnp.concatenate([acc_slab[pl.ds(s, T, stride=ROW_CHUNKS), :] for s in range(ROW_CHUNKS)], axis=-1)
    y = DEEPNORM_ALPHA * h_ref[...] + routed + sh_ref[...].astype(F32)
    y = _ln_rows(y, g_ref[...], b_ref[...])
    of_ref[...] = y
    ob_ref[...] = y.astype(BF16)


def combine_ln(pos_t, gate_t, h, shared, y, g, b):
    N, D = h.shape
    nt, K, T = pos_t.shape
    row = lambda i: (i, 0)
    const = lambda i: (0, 0)
    return pl.pallas_call(
        _combine_ln_kernel,
        grid=(nt,),
        in_specs=[pl.BlockSpec((None, K, T), lambda i: (i, 0, 0), memory_space=pltpu.SMEM),
                  pl.BlockSpec((None, K, T), lambda i: (i, 0, 0), memory_space=pltpu.SMEM),
                  pl.BlockSpec((T, D), row), pl.BlockSpec((T, D), row),
                  pl.BlockSpec((1, D), const), pl.BlockSpec((1, D), const),
                  pl.BlockSpec(memory_space=pl.ANY)],
        out_specs=[pl.BlockSpec((T, D), row), pl.BlockSpec((T, D), row)],
        out_shape=[jax.ShapeDtypeStruct((N, D), F32), jax.ShapeDtypeStruct((N, D), BF16)],
        scratch_shapes=[pltpu.VMEM((K, T * ROW_CHUNKS, LANES), F32), pltpu.VMEM((T * ROW_CHUNKS, LANES), F32),
                        pltpu.SemaphoreType.DMA(())],
        compiler_params=_cparams("arbitrary"),
        name="combine_ln",
    )(pos_t, gate_t, h, shared, g.astype(F32).reshape(1, D), b.astype(F32).reshape(1, D), y)


def moe_ffn_ln(h, h_bf, layer, router_w, router_bias, exp_w_up, exp_w_down, shared_w_up, shared_w_down, g, b):
    N, D = h.shape
    idx, rank, gate, counts = route(h, router_w[layer], router_bias[layer])
    starts, tables = _visit_tables(counts, EXP_TM, N * TOP_K)
    pos = slot_positions(starts, idx, rank)
    shared, xs = dispatch_shared(_tiles(pos, min(DISPATCH_T, N)), h, h_bf, shared_w_up, shared_w_down, layer)
    y = expert_ffn(xs, tables, exp_w_up, exp_w_down, layer)
    tc = min(COMBINE_T, N)
    return combine_ln(_tiles(pos, tc), _tiles(gate, tc), h, shared, y, g, b)


def kernel(x, mem, conv_w_in, conv_dw, conv_dw_b, conv_ln_g, conv_ln_b, fox_w_in, fox_b_f, mem_w_kv, w_out,
           ln_g, ln_b, router_w, router_bias, exp_w_up, exp_w_down, shared_w_up, shared_w_down):
    B, S, D = x.shape
    N = B * S
    h = x.reshape(N, D).astype(F32)
    h_bf = h.astype(BF16)
    mem_bf = mem.reshape(B * MEM_LEN, D).astype(BF16)
    for i in range(DEPTH):
        j = i // N_MIXERS
        if i % N_MIXERS == 0:
            proj = matmul(h_bf, conv_w_in[j].astype(BF16), BF16).reshape(B, S, -1)
            prim = conv_mixer(proj, conv_dw[j], conv_dw_b[j], conv_ln_g[j], conv_ln_b[j])
            q_block = 2 * CONV_CH // MEM_WIDTH
        else:
            w = fox_w_in[j]
            nf = 3 * PRIMARY_WIDTH
            w_main = jnp.concatenate([w[:, :nf], w[:, nf + N_FOX_HEADS:]], axis=1).astype(BF16)
            proj = matmul(h_bf, w_main, BF16).reshape(B, S, -1)
            c = forget_cumlog(h_bf.reshape(B, S, D), w[:, nf:nf + N_FOX_HEADS], fox_b_f[j])
            prim = fox_attention(proj, c)
            q_block = nf // MEM_WIDTH
        kv = matmul(mem_bf, mem_w_kv[i].astype(BF16), BF16).reshape(B, MEM_LEN, 2 * MEM_WIDTH)
        memo = memory_attention(proj, q_block, kv)
        h, h_bf = outproj_ln(prim.reshape(N, PRIMARY_WIDTH), memo.reshape(N, MEM_WIDTH), w_out[i], h,
                             ln_g[i, 0], ln_b[i, 0])
        h, h_bf = moe_ffn_ln(h, h_bf, i, router_w, router_bias, exp_w_up, exp_w_down,
                             shared_w_up, shared_w_down, ln_g[i, 1], ln_b[i, 1])
    return h.reshape(B, S, D)
```

```python
import functools

import jax
import jax.numpy as jnp
from jax import lax
from jax.experimental import pallas as pl
from jax.experimental.pallas import tpu as pltpu

F32 = jnp.float32
BF16 = jnp.bfloat16

D_MODEL = 2048
DEPTH = 4
N_MIXERS = 2
MEM_LEN = 256
HEAD_DIM = 128
N_MEM_HEADS = 4
MEM_WIDTH = N_MEM_HEADS * HEAD_DIM
PRIMARY_WIDTH = D_MODEL - MEM_WIDTH
CONV_CH = PRIMARY_WIDTH
CONV_WIDTH = 31
N_FOX_HEADS = PRIMARY_WIDTH // HEAD_DIM
N_EXPERTS = 64
TOP_K = 8
N_GROUPS = 8
TOPK_GROUPS = 4
D_EXPERT = 512
ROUTED_SCALE = 2.5
LN_EPS = 1e-5
DEEPNORM_ALPHA = (2 * DEPTH) ** 0.25
LOG2E = 1.4426950408889634

LANES = 128
VMEM_LIMIT = 56 * 1024 * 1024
MM_TM, MM_TN = 1024, 512
ROW_TILE = 256
CONV_TS = 256
CONV_HALO = 32
CONV_SUB = 64
FOX_TQ = 256
FOX_TK = 512
FOX_HP = 2
MEM_TQ = 512
EXP_TM = 256
SLAB_ROWS = D_MODEL // LANES // 2
ROUTE_T = 256
DISPATCH_T = 256
COMBINE_T = 128


def _cparams(*sem):
    return pltpu.CompilerParams(dimension_semantics=sem, vmem_limit_bytes=VMEM_LIMIT)


def _ln_rows(y, g, b):
    mu = jnp.mean(y, axis=-1, keepdims=True)
    d = y - mu
    var = jnp.mean(d * d, axis=-1, keepdims=True)
    return d * lax.rsqrt(var + LN_EPS) * g + b


def _mm_kernel(a_ref, w_ref, o_ref):
    o_ref[...] = jnp.dot(a_ref[...], w_ref[...], preferred_element_type=F32).astype(o_ref.dtype)


def matmul(a, w, out_dtype, tm=MM_TM, tn=MM_TN):
    M, K = a.shape
    N = w.shape[1]
    tm, tn = min(tm, M), min(tn, N)
    assert M % tm == 0 and N % tn == 0
    return pl.pallas_call(
        _mm_kernel,
        grid=(M // tm, N // tn),
        in_specs=[pl.BlockSpec((tm, K), lambda i, j: (i, 0)),
                  pl.BlockSpec((K, tn), lambda i, j: (0, j))],
        out_specs=pl.BlockSpec((tm, tn), lambda i, j: (i, j)),
        out_shape=jax.ShapeDtypeStruct((M, N), out_dtype),
        compiler_params=_cparams("arbitrary", "arbitrary"),
        name="matmul",
    )(a, w)


def _conv_kernel(a_ref, g_ref, ah_ref, gh_ref, dw_ref, dwb_ref, lg_ref, lb_ref, o_ref, u_scr, y_scr):
    ts = a_ref.shape[0]
    nchunk = CONV_CH // LANES
    first = pl.program_id(1) == 0
    uh = ah_ref[...].astype(F32) * jax.nn.sigmoid(gh_ref[...].astype(F32))
    uh = jnp.where(first, 0.0, uh)
    u = a_ref[...].astype(F32) * jax.nn.sigmoid(g_ref[...].astype(F32))
    for c in range(nchunk):
        sl = slice(c * LANES, (c + 1) * LANES)
        u_scr[c, 0:CONV_HALO, :] = uh[:, sl]
        u_scr[c, CONV_HALO:CONV_HALO + ts, :] = u[:, sl]

    off = CONV_HALO - (CONV_WIDTH - 1)

    def chunk_body(c, carry):
        w = dw_ref[c]
        bias = dwb_ref[c]
        for r in range(ts // CONV_SUB):
            acc = jnp.broadcast_to(bias, (CONV_SUB, LANES))
            for j in range(CONV_WIDTH):
                acc = acc + w[j:j + 1, :] * u_scr[c, pl.ds(off + j + r * CONV_SUB, CONV_SUB), :]
            y_scr[c, r * CONV_SUB:(r + 1) * CONV_SUB, :] = acc
        return carry

    lax.fori_loop(0, nchunk, chunk_body, 0)

    s1 = jnp.zeros((ts, 1), F32)
    for c in range(nchunk):
        s1 = s1 + jnp.sum(y_scr[c], axis=-1, keepdims=True)
    mu = s1 * (1.0 / CONV_CH)
    s2 = jnp.zeros((ts, 1), F32)
    for c in range(nchunk):
        d = y_scr[c] - mu
        s2 = s2 + jnp.sum(d * d, axis=-1, keepdims=True)
    rstd = lax.rsqrt(s2 * (1.0 / CONV_CH) + LN_EPS)
    for c in range(nchunk):
        sl = slice(c * LANES, (c + 1) * LANES)
        z = (y_scr[c] - mu) * rstd * lg_ref[:, sl] + lb_ref[:, sl]
        o_ref[:, sl] = (z * jax.nn.sigmoid(z)).astype(o_ref.dtype)


def conv_mixer(proj, dw, dw_b, ln_g, ln_b):
    B, S, _ = proj.shape
    ts = min(CONV_TS, S)
    nchunk = CONV_CH // LANES
    hb = ts // CONV_HALO
    dw_p = jnp.zeros((CONV_HALO, CONV_CH), F32).at[:CONV_WIDTH].set(dw.astype(F32))
    dw_c = dw_p.reshape(CONV_HALO, nchunk, LANES).transpose(1, 0, 2)
    dwb_c = dw_b.astype(F32).reshape(nchunk, 1, LANES)
    halo_idx = lambda b, i: (b, jnp.maximum(i * hb - 1, 0), 0)
    halo_idx_g = lambda b, i: (b, jnp.maximum(i * hb - 1, 0), 1)
    return pl.pallas_call(
        _conv_kernel,
        grid=(B, S // ts),
        in_specs=[
            pl.BlockSpec((None, ts, CONV_CH), lambda b, i: (b, i, 0)),
            pl.BlockSpec((None, ts, CONV_CH), lambda b, i: (b, i, 1)),
            pl.BlockSpec((None, CONV_HALO, CONV_CH), halo_idx),
            pl.BlockSpec((None, CONV_HALO, CONV_CH), halo_idx_g),
            pl.BlockSpec((nchunk, CONV_HALO, LANES), lambda b, i: (0, 0, 0)),
            pl.BlockSpec((nchunk, 1, LANES), lambda b, i: (0, 0, 0)),
            pl.BlockSpec((1, CONV_CH), lambda b, i: (0, 0)),
            pl.BlockSpec((1, CONV_CH), lambda b, i: (0, 0)),
        ],
        out_specs=pl.BlockSpec((None, ts, CONV_CH), lambda b, i: (b, i, 0)),
        out_shape=jax.ShapeDtypeStruct((B, S, CONV_CH), BF16),
        scratch_shapes=[pltpu.VMEM((nchunk, CONV_HALO + ts, LANES), F32),
                        pltpu.VMEM((nchunk, ts, LANES), F32)],
        compiler_params=_cparams("arbitrary", "arbitrary"),
        name="conv_mixer",
    )(proj, proj, proj, proj, dw_c, dwb_c, ln_g.astype(F32).reshape(1, -1), ln_b.astype(F32).reshape(1, -1))


def _fgate_kernel(h_ref, w_ref, b_ref, c_ref, carry_ref):
    ts = h_ref.shape[0]

    @pl.when(pl.program_id(1) == 0)
    def _():
        carry_ref[...] = jnp.zeros_like(carry_ref)

    f = jnp.dot(h_ref[...], w_ref[...], preferred_element_type=F32) + b_ref[...]
    ls = jax.nn.log_sigmoid(f)
    row = lax.broadcasted_iota(jnp.int32, (ts, ts), 0)
    col = lax.broadcasted_iota(jnp.int32, (ts, ts), 1)
    tri = jnp.where(row >= col, 1.0, 0.0).astype(BF16)
    hi = ls.astype(BF16)
    r1 = ls - hi.astype(F32)
    mid = r1.astype(BF16)
    lo = (r1 - mid.astype(F32)).astype(BF16)
    cs = (jnp.dot(tri, hi, preferred_element_type=F32)
          + jnp.dot(tri, mid, preferred_element_type=F32)
          + jnp.dot(tri, lo, preferred_element_type=F32))
    c = cs + carry_ref[...]
    c_ref[...] = c
    carry_ref[...] = c[ts - 1:ts, :]


def forget_cumlog(h_bf, w_f, b_f):
    B, S, D = h_bf.shape
    H = w_f.shape[1]
    ts = min(256, S)
    w_p = jnp.zeros((D, LANES), BF16).at[:, :H].set(w_f.astype(BF16))
    b_p = jnp.zeros((1, LANES), F32).at[0, :H].set(b_f.astype(F32))
    return pl.pallas_call(
        _fgate_kernel,
        grid=(B, S // ts),
        in_specs=[pl.BlockSpec((None, ts, D), lambda b, i: (b, i, 0)),
                  pl.BlockSpec((D, LANES), lambda b, i: (0, 0)),
                  pl.BlockSpec((1, LANES), lambda b, i: (0, 0))],
        out_specs=pl.BlockSpec((None, ts, LANES), lambda b, i: (b, i, 0)),
        out_shape=jax.ShapeDtypeStruct((B, S, LANES), F32),
        scratch_shapes=[pltpu.VMEM((1, LANES), F32)],
        compiler_params=_cparams("arbitrary", "arbitrary"),
        name="forget_cumlog",
    )(h_bf, w_p, b_p)


def _fox_kernel(q_ref, k_ref, v_ref, cq_ref, ck_ref, o_ref):
    tq = q_ref.shape[0]
    tk = ck_ref.shape[-1]
    i = pl.program_id(2)
    qscale = HEAD_DIM ** -0.5 * LOG2E
    heads = range(FOX_HP)
    hs = [slice(h * HEAD_DIM, (h + 1) * HEAD_DIM) for h in heads]
    qs = [(q_ref[:, hs[h]].astype(F32) * qscale).astype(BF16) for h in heads]
    cqs = [cq_ref[h] for h in heads]

    def step(j, carry, masked):
        start = pl.multiple_of(j * tk, tk)
        ss = []
        for h in heads:
            k = k_ref[pl.ds(start, tk), hs[h]]
            ss.append(lax.dot_general(qs[h], k, (((1,), (1,)), ((), ())), preferred_element_type=F32))
        if masked:
            row = i * tq + lax.broadcasted_iota(jnp.int32, (tq, tk), 0)
            col = j * tk + lax.broadcasted_iota(jnp.int32, (tq, tk), 1)
            causal = row >= col
        ps, ms, alphas = [], [], []
        for h in heads:
            m = carry[h][0]
            s = ss[h] - ck_ref[h, j]
            if masked:
                s = jnp.where(causal, s, -jnp.inf)
            m_new = jnp.maximum(m, jnp.max(s, axis=-1, keepdims=True) + cqs[h])
            p = jnp.exp2(s - (m_new - cqs[h]))
            alpha = jnp.exp2(m - m_new)
            ps.append(p.astype(BF16))
            ms.append(m_new)
            alphas.append(alpha)
        out = []
        for h in heads:
            v1 = jnp.concatenate([v_ref[pl.ds(start, tk), hs[h]], ones], axis=-1)
            acc = alphas[h] * carry[h][1] + jnp.dot(ps[h], v1, preferred_element_type=F32)
            out.append((ms[h], acc))
        return tuple(out)

    ones = jnp.ones((tk, HEAD_DIM), BF16)
    init = tuple((jnp.full((tq, 1), -jnp.inf, F32), jnp.zeros((tq, 2 * HEAD_DIM), F32)) for _ in heads)
    n_full = (i * tq) // tk
    carry = lax.fori_loop(0, n_full, lambda j, c: step(j, c, False), init)
    for d in range(-(-tq // tk)):
        carry = step(n_full + d, carry, True)
    for h in heads:
        acc = carry[h][1]
        o_ref[:, hs[h]] = (acc[:, :HEAD_DIM] / acc[:, HEAD_DIM:]).astype(o_ref.dtype)


def fox_attention(proj, c):
    B, S, _ = proj.shape
    H = N_FOX_HEADS
    tq, tk = min(FOX_TQ, S), min(FOX_TK, S)
    nq, nk = S // tq, S // tk
    ng = H // FOX_HP
    w = FOX_HP * HEAD_DIM
    c_h = jnp.transpose(c[:, :, :H], (0, 2, 1)) * LOG2E
    c_col = c_h.reshape(B, H, S, 1)
    c_row = c_h.reshape(B, H, nk, 1, tk)
    return pl.pallas_call(
        _fox_kernel,
        grid=(B, ng, nq),
        in_specs=[
            pl.BlockSpec((None, tq, w), lambda b, g, i: (b, i, g)),
            pl.BlockSpec((None, S, w), lambda b, g, i: (b, 0, ng + g)),
            pl.BlockSpec((None, S, w), lambda b, g, i: (b, 0, 2 * ng + g)),
            pl.BlockSpec((None, FOX_HP, tq, 1), lambda b, g, i: (b, g, i, 0)),
            pl.BlockSpec((None, FOX_HP, nk, 1, tk), lambda b, g, i: (b, g, 0, 0, 0)),
        ],
        out_specs=pl.BlockSpec((None, tq, w), lambda b, g, i: (b, i, g)),
        out_shape=jax.ShapeDtypeStruct((B, S, PRIMARY_WIDTH), BF16),
        compiler_params=_cparams("arbitrary", "arbitrary", "arbitrary"),
        name="fox_attention",
    )(proj, proj, proj, c_col, c_row)


def _mem_attn_kernel(q_ref, kv_ref, o_ref):
    scale = HEAD_DIM ** -0.5
    for h in range(N_MEM_HEADS):
        sl = slice(h * HEAD_DIM, (h + 1) * HEAD_DIM)
        slv = slice(MEM_WIDTH + h * HEAD_DIM, MEM_WIDTH + (h + 1) * HEAD_DIM)
        s = lax.dot_general(q_ref[:, sl], kv_ref[:, sl], (((1,), (1,)), ((), ())),
                            preferred_element_type=F32) * scale
        m = jnp.max(s, axis=-1, keepdims=True)
        p = jnp.exp(s - m)
        l = jnp.sum(p, axis=-1, keepdims=True)
        o = jnp.dot(p.astype(BF16), kv_ref[:, slv], preferred_element_type=F32)
        o_ref[:, sl] = (o / l).astype(o_ref.dtype)


def memory_attention(proj, q_block, kv):
    B, S, _ = proj.shape
    M = kv.shape[1]
    tq = min(MEM_TQ, S)
    return pl.pallas_call(
        _mem_attn_kernel,
        grid=(B, S // tq),
        in_specs=[pl.BlockSpec((None, tq, MEM_WIDTH), lambda b, i: (b, i, q_block)),
                  pl.BlockSpec((None, M, 2 * MEM_WIDTH), lambda b, i: (b, 0, 0))],
        out_specs=pl.BlockSpec((None, tq, MEM_WIDTH), lambda b, i: (b, i, 0)),
        out_shape=jax.ShapeDtypeStruct((B, S, MEM_WIDTH), BF16),
        compiler_params=_cparams("arbitrary", "arbitrary"),
        name="memory_attention",
    )(proj, kv)


def _outproj_ln_kernel(p_ref, m_ref, wt_ref, wb_ref, h_ref, g_ref, b_ref, of_ref, ob_ref):
    mix = (jnp.dot(p_ref[...], wt_ref[...], preferred_element_type=F32)
           + jnp.dot(m_ref[...], wb_ref[...], preferred_element_type=F32))
    y = _ln_rows(DEEPNORM_ALPHA * h_ref[...] + mix, g_ref[...], b_ref[...])
    of_ref[...] = y
    ob_ref[...] = y.astype(BF16)


def outproj_ln(prim, memo, w_out, h, g, b):
    N, D = h.shape
    tm = min(ROW_TILE, N)
    wt = w_out[:PRIMARY_WIDTH].astype(BF16)
    wb = w_out[PRIMARY_WIDTH:].astype(BF16)
    row = lambda i: (i, 0)
    const = lambda i: (0, 0)
    return pl.pallas_call(
        _outproj_ln_kernel,
        grid=(N // tm,),
        in_specs=[pl.BlockSpec((tm, PRIMARY_WIDTH), row), pl.BlockSpec((tm, MEM_WIDTH), row),
                  pl.BlockSpec((PRIMARY_WIDTH, D), const), pl.BlockSpec((MEM_WIDTH, D), const),
                  pl.BlockSpec((tm, D), row), pl.BlockSpec((1, D), const), pl.BlockSpec((1, D), const)],
        out_specs=[pl.BlockSpec((tm, D), row), pl.BlockSpec((tm, D), row)],
        out_shape=[jax.ShapeDtypeStruct((N, D), F32), jax.ShapeDtypeStruct((N, D), BF16)],
        compiler_params=_cparams("arbitrary"),
        name="outproj_ln",
    )(prim, memo, wt, wb, h, g.astype(F32).reshape(1, D), b.astype(F32).reshape(1, D))


def _route_kernel(h_ref, w_ref, b_ref, idx_ref, rank_ref, gate_ref, cnt_ref, carry_ref):
    T = h_ref.shape[0]
    E = w_ref.shape[0]
    gsz = E // N_GROUPS
    neg = -jnp.inf

    @pl.when(pl.program_id(0) == 0)
    def _():
        carry_ref[...] = jnp.zeros_like(carry_ref)

    logits = lax.dot_general(w_ref[...], h_ref[...], (((1,), (1,)), ((), ())),
                             precision=lax.Precision.HIGHEST, preferred_element_type=F32)
    scores = jax.nn.sigmoid(logits)
    choice = scores + b_ref[...]

    io8 = lax.broadcasted_iota(jnp.int32, (gsz, T), 0)
    rows = []
    for g in range(N_GROUPS):
        blk = choice[g * gsz:(g + 1) * gsz, :]
        m1 = jnp.max(blk, axis=0, keepdims=True)
        f1 = jnp.min(jnp.where(blk == m1, io8, gsz), axis=0, keepdims=True)
        m2 = jnp.max(jnp.where(io8 == f1, neg, blk), axis=0, keepdims=True)
        rows.append(jnp.broadcast_to(m1 + m2, (gsz, T)))
    gs = jnp.concatenate(rows, axis=0)
    eid = lax.broadcasted_iota(jnp.int32, (E, T), 0)
    gid = eid // gsz
    emask = jnp.zeros((E, T), jnp.bool_)
    for _ in range(TOPK_GROUPS):
        m = jnp.max(gs, axis=0, keepdims=True)
        g = jnp.min(jnp.where(gs == m, gid, N_GROUPS), axis=0, keepdims=True)
        hit = gid == g
        emask = jnp.logical_or(emask, hit)
        gs = jnp.where(hit, neg, gs)

    masked = jnp.where(emask, choice, neg)
    sel = jnp.zeros((E, T), jnp.bool_)
    ids, ws, hits = [], [], []
    for _ in range(TOP_K):
        m = jnp.max(masked, axis=0, keepdims=True)
        i_k = jnp.min(jnp.where(masked == m, eid, E), axis=0, keepdims=True)
        hit = eid == i_k
        ids.append(i_k)
        ws.append(jnp.sum(jnp.where(hit, scores, 0.0), axis=0, keepdims=True))
        hits.append(hit)
        masked = jnp.where(hit, neg, masked)
        sel = jnp.logical_or(sel, hit)
    wsum = ws[0]
    for w in ws[1:]:
        wsum = wsum + w

    sel_f = jnp.where(sel, 1.0, 0.0)
    r = lax.broadcasted_iota(jnp.int32, (T, T), 0)
    c = lax.broadcasted_iota(jnp.int32, (T, T), 1)
    tri = jnp.where(r < c, 1.0, 0.0).astype(BF16)
    before = jnp.dot(sel_f.astype(BF16), tri, preferred_element_type=F32) + carry_ref[...]
    for k in range(TOP_K):
        idx_ref[k:k + 1, :] = ids[k]
        rank_ref[k:k + 1, :] = jnp.sum(jnp.where(hits[k], before, 0.0), axis=0, keepdims=True).astype(jnp.int32)
        gate_ref[k:k + 1, :] = ws[k] / wsum * ROUTED_SCALE
    total = carry_ref[...] + jnp.sum(sel_f, axis=1, keepdims=True)
    carry_ref[...] = total
    cnt_ref[...] = jnp.broadcast_to(total, cnt_ref.shape).astype(jnp.int32)


def route(h, w_r, r_bias):
    N, D = h.shape
    E = w_r.shape[1]
    T = min(ROUTE_T, N)
    kn = lambda i: (0, i)
    idx, rank, gate, cnt = pl.pallas_call(
        _route_kernel,
        grid=(N // T,),
        in_specs=[pl.BlockSpec((T, D), lambda i: (i, 0)), pl.BlockSpec((E, D), lambda i: (0, 0)),
                  pl.BlockSpec((E, 1), lambda i: (0, 0))],
        out_specs=[pl.BlockSpec((TOP_K, T), kn), pl.BlockSpec((TOP_K, T), kn), pl.BlockSpec((TOP_K, T), kn),
                   pl.BlockSpec((E, LANES), lambda i: (0, 0))],
        out_shape=[jax.ShapeDtypeStruct((TOP_K, N), jnp.int32), jax.ShapeDtypeStruct((TOP_K, N), jnp.int32),
                   jax.ShapeDtypeStruct((TOP_K, N), F32), jax.ShapeDtypeStruct((E, LANES), jnp.int32)],
        scratch_shapes=[pltpu.VMEM((E, 1), F32)],
        compiler_params=_cparams("arbitrary"),
        name="route",
    )(h, w_r.astype(F32).T, r_bias.astype(F32).reshape(E, 1))
    return idx, rank, gate, cnt[:, 0]


def _pos_kernel(starts_ref, idx_ref, rank_ref, pos_ref):
    idx = idx_ref[...]
    acc = rank_ref[...]
    for e in range(N_EXPERTS):
        acc = acc + jnp.where(idx == e, starts_ref[e], 0)
    pos_ref[...] = acc


def slot_positions(starts, idx, rank):
    K, N = idx.shape
    T = min(2048, N)
    grid_spec = pltpu.PrefetchScalarGridSpec(
        num_scalar_prefetch=1, grid=(N // T,),
        in_specs=[pl.BlockSpec((K, T), lambda i, s: (0, i)), pl.BlockSpec((K, T), lambda i, s: (0, i))],
        out_specs=pl.BlockSpec((K, T), lambda i, s: (0, i)))
    return pl.pallas_call(
        _pos_kernel, grid_spec=grid_spec, out_shape=jax.ShapeDtypeStruct((K, N), jnp.int32),
        compiler_params=_cparams("arbitrary"), name="slot_positions",
    )(starts.astype(jnp.int32), idx, rank)


def _pack_pair(a, b):
    ua = lax.bitcast_convert_type(a.astype(BF16).astype(F32), jnp.uint32)
    ub = lax.bitcast_convert_type(b.astype(BF16).astype(F32), jnp.uint32)
    return ub | (ua >> 16)


def _unpack_pair(w):
    lo = lax.bitcast_convert_type(w << 16, F32)
    hi = lax.bitcast_convert_type(w & jnp.uint32(0xFFFF0000), F32)
    return lo, hi


def _rows_to_slab(x):
    return [_pack_pair(x[:, j * LANES:(j + 1) * LANES], x[:, (j + SLAB_ROWS) * LANES:(j + SLAB_ROWS + 1) * LANES])
            for j in range(SLAB_ROWS)]


def _slab_to_rows(pieces):
    halves = [_unpack_pair(w) for w in pieces]
    return jnp.concatenate([h[0] for h in halves] + [h[1] for h in halves], axis=-1)


def _tiles(a, T):
    K, N = a.shape
    return a.reshape(K, N // T, T).transpose(1, 0, 2)


def _swiglu_bf(x, wu, wd):
    f = wd.shape[0]
    h = jnp.dot(x, wu, preferred_element_type=F32)
    a = h[:, :f]
    act = (a * jax.nn.sigmoid(a)) * h[:, f:]
    return jnp.dot(act.astype(BF16), wd, preferred_element_type=F32)


def _dispatch_kernel(pos_ref, hb_ref, su_ref, sd_ref, sh_ref, xs_hbm, slab, su_bf, sd_bf, sem):
    T = hb_ref.shape[0]

    @pl.when(pl.program_id(0) == 0)
    def _():
        su_bf[...] = su_ref[...].astype(BF16)
        sd_bf[...] = sd_ref[...].astype(BF16)

    hb = hb_ref[...]
    for j, piece in enumerate(_rows_to_slab(hb.astype(F32))):
        slab[pl.ds(j, T, stride=SLAB_ROWS), :] = piece

    def issue(t, carry):
        src = slab.at[pl.ds(pl.multiple_of(t * SLAB_ROWS, SLAB_ROWS), SLAB_ROWS)]
        for k in range(TOP_K):
            dst = xs_hbm.at[pl.ds(pl.multiple_of(pos_ref[k, t] * SLAB_ROWS, SLAB_ROWS), SLAB_ROWS)]
            pltpu.make_async_copy(src, dst, sem).start(priority=k % 2)
        return carry

    lax.fori_loop(0, T, issue, 0)
    sh_ref[...] = _swiglu_bf(hb, su_bf[...], sd_bf[...]).astype(sh_ref.dtype)
    for k in range(TOP_K):
        pltpu.make_async_copy(slab, xs_hbm.at[pl.ds(0, T * SLAB_ROWS)], sem).wait()


def dispatch_shared(pos_t, h_bf, sh_up, sh_down, layer):
    N, D = h_bf.shape
    nt, K, T = pos_t.shape
    F2 = sh_up.shape[-1]
    Fd = sh_down.shape[1]
    return pl.pallas_call(
        _dispatch_kernel,
        grid=(nt,),
        in_specs=[pl.BlockSpec((None, K, T), lambda i: (i, 0, 0), memory_space=pltpu.SMEM),
                  pl.BlockSpec((T, D), lambda i: (i, 0)),
                  pl.BlockSpec((None, D, F2), lambda i: (layer, 0, 0)),
                  pl.BlockSpec((None, Fd, D), lambda i: (layer, 0, 0))],
        out_specs=[pl.BlockSpec((T, D), lambda i: (i, 0)), pl.BlockSpec(memory_space=pl.ANY)],
        out_shape=[jax.ShapeDtypeStruct((N, D), BF16),
                   jax.ShapeDtypeStruct((N * K * SLAB_ROWS, LANES), jnp.uint32)],
        scratch_shapes=[pltpu.VMEM((T * SLAB_ROWS, LANES), jnp.uint32), pltpu.VMEM((D, F2), BF16),
                        pltpu.VMEM((Fd, D), BF16), pltpu.SemaphoreType.DMA(())],
        compiler_params=_cparams("arbitrary"),
        name="dispatch_shared",
    )(pos_t, h_bf, sh_up, sh_down)


def _expert_kernel(layer, tile_ref, be_ref, lo_ref, hi_ref, first_ref, slot_ref, next_ref, x_ref, wu_hbm, wd_hbm,
                   o_ref, wu_f32, wd_f32, wu_bf, wd_bf, sem):
    v = pl.program_id(0)
    tm = x_ref.shape[0] // SLAB_ROWS
    lo = lo_ref[v]
    hi = hi_ref[v]

    def weight_copies(e, slot):
        return (pltpu.make_async_copy(wu_hbm.at[layer, e], wu_f32.at[slot], sem.at[0, slot]),
                pltpu.make_async_copy(wd_hbm.at[layer, e], wd_f32.at[slot], sem.at[1, slot]))

    @pl.when(hi > lo)
    def _():
        changed = jnp.logical_or(v == 0, be_ref[v] != be_ref[jnp.maximum(v - 1, 0)])

        @pl.when(changed)
        def _():
            slot = slot_ref[v]

            @pl.when(v == 0)
            def _():
                for c in weight_copies(be_ref[v], slot):
                    c.start()

            for c in weight_copies(be_ref[v], slot):
                c.wait()

            @pl.when(next_ref[v] >= 0)
            def _():
                for c in weight_copies(next_ref[v], 1 - slot):
                    c.start()

            wu_bf[...] = wu_f32[slot].astype(BF16)
            wd_bf[...] = wd_f32[slot].astype(BF16)

        x = _slab_to_rows([x_ref[pl.ds(j, tm, stride=SLAB_ROWS), :] for j in range(SLAB_ROWS)])
        y = _swiglu_bf(x.astype(BF16), wu_bf[...], wd_bf[...])
        row = lax.broadcasted_iota(jnp.int32, (tm, 1), 0)
        mine = jnp.logical_and(row >= lo, row < hi)
        pieces = _rows_to_slab(y)

        @pl.when(first_ref[v] == 1)
        def _():
            for j in range(SLAB_ROWS):
                o_ref[pl.ds(j, tm, stride=SLAB_ROWS), :] = jnp.where(mine, pieces[j], jnp.uint32(0))

        @pl.when(first_ref[v] == 0)
        def _():
            for j in range(SLAB_ROWS):
                keep = o_ref[pl.ds(j, tm, stride=SLAB_ROWS), :]
                o_ref[pl.ds(j, tm, stride=SLAB_ROWS), :] = jnp.where(mine, pieces[j], keep)


def _lookup(table, idx):
    n = table.shape[0]
    return jnp.sum(jnp.where(idx[:, None] == jnp.arange(n)[None, :], table[None, :], 0), axis=1)


def _visit_tables(counts, tm, A):
    E = counts.shape[0]
    V = A // tm + E
    starts = jnp.cumsum(counts) - counts
    ends = starts + counts
    first_tile = starts // tm
    nvis = jnp.where(counts > 0, (ends - 1) // tm - first_tile + 1, 0)
    vend = jnp.cumsum(nvis)
    voff = vend - nvis
    total = vend[-1]
    v = jnp.arange(V, dtype=jnp.int32)
    e_v = jnp.minimum(jnp.sum(vend[None, :] <= v[:, None], axis=1), E - 1).astype(jnp.int32)
    valid = v < total
    e_last = jnp.sum(jnp.where(v == total - 1, e_v, 0))
    e_v = jnp.where(valid, e_v, e_last)
    tile_v = jnp.where(valid, _lookup(first_tile, e_v) + v - _lookup(voff, e_v), A // tm - 1)
    base = tile_v * tm
    lo_v = jnp.where(valid, jnp.maximum(_lookup(starts, e_v), base) - base, 0)
    hi_v = jnp.where(valid, jnp.minimum(_lookup(ends, e_v), base + tm) - base, 0)
    prev_tile = jnp.concatenate([jnp.full((1,), -1, tile_v.dtype), tile_v[:-1]])
    first_v = jnp.logical_and(valid, tile_v != prev_tile)
    has = nvis > 0
    ids = jnp.arange(E)
    slot_e = (jnp.cumsum(has) - 1) & 1
    later = jnp.logical_and(ids[None, :] > ids[:, None], has[None, :])
    next_e = jnp.min(jnp.where(later, ids[None, :], E), axis=1)
    next_e = jnp.where(next_e == E, -1, next_e)
    i32 = lambda a: a.astype(jnp.int32)
    return starts, (i32(tile_v), i32(e_v), i32(lo_v), i32(hi_v), i32(first_v),
                    i32(_lookup(slot_e, e_v)), i32(_lookup(next_e, e_v)))


def expert_ffn(xs, tables, w_up, w_down, layer):
    V = tables[0].shape[0]
    D, F2 = w_up.shape[-2:]
    Fd = w_down.shape[-2]
    blk = EXP_TM * SLAB_ROWS
    nt = len(tables)
    tile_map = lambda v, *t: (t[0][v], 0)
    grid_spec = pltpu.PrefetchScalarGridSpec(
        num_scalar_prefetch=nt,
        grid=(V,),
        in_specs=[pl.BlockSpec((blk, LANES), tile_map),
                  pl.BlockSpec(memory_space=pl.ANY), pl.BlockSpec(memory_space=pl.ANY)],
        out_specs=pl.BlockSpec((blk, LANES), tile_map),
        scratch_shapes=[pltpu.VMEM((2, D, F2), F32), pltpu.VMEM((2, Fd, D), F32),
                        pltpu.VMEM((D, F2), BF16), pltpu.VMEM((Fd, D), BF16),
                        pltpu.SemaphoreType.DMA((2, 2))],
    )
    return pl.pallas_call(
        functools.partial(_expert_kernel, layer),
        grid_spec=grid_spec,
        out_shape=jax.ShapeDtypeStruct(xs.shape, jnp.uint32),
        compiler_params=_cparams("arbitrary"),
        name="expert_ffn",
    )(*tables, xs, w_up, w_down)


def _combine_ln_kernel(pos_ref, gate_ref, h_ref, sh_ref, g_ref, b_ref, y_hbm, of_ref, ob_ref, ybuf, acc_lo, acc_hi, sem):
    T = h_ref.shape[0]

    def issue(t, carry):
        dst0 = pl.multiple_of(t * SLAB_ROWS, SLAB_ROWS)
        for k in range(TOP_K):
            src0 = pl.multiple_of(pos_ref[k, t] * SLAB_ROWS, SLAB_ROWS)
            pltpu.make_async_copy(y_hbm.at[pl.ds(src0, SLAB_ROWS)], ybuf.at[k, pl.ds(dst0, SLAB_ROWS)],
                                  sem).start(priority=k % 2)
        return carry

    lax.fori_loop(0, T, issue, 0)
    for k in range(TOP_K):
        pltpu.make_async_copy(y_hbm.at[pl.ds(0, T * SLAB_ROWS)], ybuf.at[k], sem).wait()

    def mix(t, carry):
        r0 = pl.multiple_of(t * SLAB_ROWS, SLAB_ROWS)
        lo, hi = _unpack_pair(ybuf[0, pl.ds(r0, SLAB_ROWS), :])
        g = gate_ref[0, t]
        lo, hi = g * lo, g * hi
        for k in range(1, TOP_K):
            l2, h2 = _unpack_pair(ybuf[k, pl.ds(r0, SLAB_ROWS), :])
            g = gate_ref[k, t]
            lo, hi = lo + g * l2, hi + g * h2
        acc_lo[pl.ds(r0, SLAB_ROWS), :] = lo
        acc_hi[pl.ds(r0, SLAB_ROWS), :] = hi
        return carry

    lax.fori_loop(0, T, mix, 0)
    routed = jnp.concatenate([acc_lo[pl.ds(j, T, stride=SLAB_ROWS), :] for j in range(SLAB_ROWS)]
                             + [acc_hi[pl.ds(j, T, stride=SLAB_ROWS), :] for j in range(SLAB_ROWS)], axis=-1)
    y = DEEPNORM_ALPHA * h_ref[...] + routed + sh_ref[...].astype(F32)
    y = _ln_rows(y, g_ref[...], b_ref[...])
    of_ref[...] = y
    ob_ref[...] = y.astype(BF16)


def combine_ln(pos_t, gate_t, h, shared, y, g, b):
    N, D = h.shape
    nt, K, T = pos_t.shape
    row = lambda i: (i, 0)
    const = lambda i: (0, 0)
    return pl.pallas_call(
        _combine_ln_kernel,
        grid=(nt,),
        in_specs=[pl.BlockSpec((None, K, T), lambda i: (i, 0, 0), memory_space=pltpu.SMEM),
                  pl.BlockSpec((None, K, T), lambda i: (i, 0, 0), memory_space=pltpu.SMEM),
                  pl.BlockSpec((T, D), row), pl.BlockSpec((T, D), row),
                  pl.BlockSpec((1, D), const), pl.BlockSpec((1, D), const),
                  pl.BlockSpec(memory_space=pl.ANY)],
        out_specs=[pl.BlockSpec((T, D), row), pl.BlockSpec((T, D), row)],
        out_shape=[jax.ShapeDtypeStruct((N, D), F32), jax.ShapeDtypeStruct((N, D), BF16)],
        scratch_shapes=[pltpu.VMEM((K, T * SLAB_ROWS, LANES), jnp.uint32), pltpu.VMEM((T * SLAB_ROWS, LANES), F32),
                        pltpu.VMEM((T * SLAB_ROWS, LANES), F32), pltpu.SemaphoreType.DMA(())],
        compiler_params=_cparams("arbitrary"),
        name="combine_ln",
    )(pos_t, gate_t, h, shared, g.astype(F32).reshape(1, D), b.astype(F32).reshape(1, D), y)


def moe_ffn_ln(h, h_bf, layer, router_w, router_bias, exp_w_up, exp_w_down, shared_w_up, shared_w_down, g, b):
    N, D = h.shape
    idx, rank, gate, counts = route(h, router_w[layer], router_bias[layer])
    starts, tables = _visit_tables(counts, EXP_TM, N * TOP_K)
    pos = slot_positions(starts, idx, rank)
    shared, xs = dispatch_shared(_tiles(pos, min(DISPATCH_T, N)), h_bf, shared_w_up, shared_w_down, layer)
    y = expert_ffn(xs, tables, exp_w_up, exp_w_down, layer)
    tc = min(COMBINE_T, N)
    return combine_ln(_tiles(pos, tc), _tiles(gate, tc), h, shared, y, g, b)


def kernel(x, mem, conv_w_in, conv_dw, conv_dw_b, conv_ln_g, conv_ln_b, fox_w_in, fox_b_f, mem_w_kv, w_out,
           ln_g, ln_b, router_w, router_bias, exp_w_up, exp_w_down, shared_w_up, shared_w_down):
    B, S, D = x.shape
    N = B * S
    h = x.reshape(N, D).astype(F32)
    h_bf = h.astype(BF16)
    mem_bf = mem.reshape(B * MEM_LEN, D).astype(BF16)
    for i in range(DEPTH):
        j = i // N_MIXERS
        if i % N_MIXERS == 0:
            proj = matmul(h_bf, conv_w_in[j].astype(BF16), BF16).reshape(B, S, -1)
            prim = conv_mixer(proj, conv_dw[j], conv_dw_b[j], conv_ln_g[j], conv_ln_b[j])
            q_block = 2 * CONV_CH // MEM_WIDTH
        else:
            w = fox_w_in[j]
            nf = 3 * PRIMARY_WIDTH
            w_main = jnp.concatenate([w[:, :nf], w[:, nf + N_FOX_HEADS:]], axis=1).astype(BF16)
            proj = matmul(h_bf, w_main, BF16).reshape(B, S, -1)
            c = forget_cumlog(h_bf.reshape(B, S, D), w[:, nf:nf + N_FOX_HEADS], fox_b_f[j])
            prim = fox_attention(proj, c)
            q_block = nf // MEM_WIDTH
        kv = matmul(mem_bf, mem_w_kv[i].astype(BF16), BF16).reshape(B, MEM_LEN, 2 * MEM_WIDTH)
        memo = memory_attention(proj, q_block, kv)
        h, h_bf = outproj_ln(prim.reshape(N, PRIMARY_WIDTH), memo.reshape(N, MEM_WIDTH), w_out[i], h,
                             ln_g[i, 0], ln_b[i, 0])
        h, h_bf = moe_ffn_ln(h, h_bf, i, router_w, router_bias, exp_w_up, exp_w_down,
                             shared_w_up, shared_w_down, ln_g[i, 1], ln_b[i, 1])
    return h.reshape(B, S, D)
```

```python
import functools

import jax
import jax.numpy as jnp
from jax import lax
from jax.experimental import pallas as pl
from jax.experimental.pallas import tpu as pltpu

F32 = jnp.float32
BF16 = jnp.bfloat16

D_MODEL = 2048
DEPTH = 4
N_MIXERS = 2
MEM_LEN = 256
HEAD_DIM = 128
N_MEM_HEADS = 4
MEM_WIDTH = N_MEM_HEADS * HEAD_DIM
PRIMARY_WIDTH = D_MODEL - MEM_WIDTH
CONV_CH = PRIMARY_WIDTH
CONV_WIDTH = 31
N_FOX_HEADS = PRIMARY_WIDTH // HEAD_DIM
N_EXPERTS = 64
TOP_K = 8
N_GROUPS = 8
TOPK_GROUPS = 4
D_EXPERT = 512
ROUTED_SCALE = 2.5
LN_EPS = 1e-5
DEEPNORM_ALPHA = (2 * DEPTH) ** 0.25
LOG2E = 1.4426950408889634

LANES = 128
VMEM_LIMIT = 56 * 1024 * 1024
MM_TM, MM_TN = 1024, 512
ROW_TILE = 256
CONV_TS = 256
CONV_HALO = 32
CONV_SUB = 64
FOX_TQ = 256
FOX_TK = 512
FOX_HP = 2
MEM_TQ = 512
EXP_TM = 256
SLAB_ROWS = D_MODEL // LANES // 2
ROUTE_T = 256
DISPATCH_T = 256
COMBINE_T = 128


def _cparams(*sem):
    return pltpu.CompilerParams(dimension_semantics=sem, vmem_limit_bytes=VMEM_LIMIT)


def _ln_rows(y, g, b):
    mu = jnp.mean(y, axis=-1, keepdims=True)
    d = y - mu
    var = jnp.mean(d * d, axis=-1, keepdims=True)
    return d * lax.rsqrt(var + LN_EPS) * g + b


def _mm_kernel(a_ref, w_ref, o_ref, w_bf):
    @pl.when(pl.program_id(1) == 0)
    def _():
        w_bf[...] = w_ref[...].astype(BF16)

    o_ref[...] = jnp.dot(a_ref[...], w_bf[...], preferred_element_type=F32).astype(o_ref.dtype)


def matmul(a, w, layer, n_cols, out_dtype, tm=MM_TM, tn=MM_TN):
    M, K = a.shape
    tm, tn = min(tm, M), min(tn, n_cols)
    assert M % tm == 0 and n_cols % tn == 0 and w.shape[1] == K
    return pl.pallas_call(
        _mm_kernel,
        grid=(n_cols // tn, M // tm),
        in_specs=[pl.BlockSpec((tm, K), lambda j, i: (i, 0)),
                  pl.BlockSpec((None, K, tn), lambda j, i: (layer, 0, j))],
        out_specs=pl.BlockSpec((tm, tn), lambda j, i: (i, j)),
        out_shape=jax.ShapeDtypeStruct((M, n_cols), out_dtype),
        scratch_shapes=[pltpu.VMEM((K, tn), BF16)],
        compiler_params=_cparams("arbitrary", "arbitrary"),
        name="matmul",
    )(a, w)


def _conv_kernel(a_ref, g_ref, ah_ref, gh_ref, dw_ref, dwb_ref, lg_ref, lb_ref, o_ref, u_scr, y_scr):
    ts = a_ref.shape[0]
    nchunk = CONV_CH // LANES
    first = pl.program_id(1) == 0
    uh = ah_ref[...].astype(F32) * jax.nn.sigmoid(gh_ref[...].astype(F32))
    uh = jnp.where(first, 0.0, uh)
    u = a_ref[...].astype(F32) * jax.nn.sigmoid(g_ref[...].astype(F32))
    for c in range(nchunk):
        sl = slice(c * LANES, (c + 1) * LANES)
        u_scr[c, 0:CONV_HALO, :] = uh[:, sl]
        u_scr[c, CONV_HALO:CONV_HALO + ts, :] = u[:, sl]

    off = CONV_HALO - (CONV_WIDTH - 1)

    def chunk_body(c, carry):
        w = dw_ref[c]
        bias = dwb_ref[c]
        for r in range(ts // CONV_SUB):
            acc = jnp.broadcast_to(bias, (CONV_SUB, LANES))
            for j in range(CONV_WIDTH):
                acc = acc + w[j:j + 1, :] * u_scr[c, pl.ds(off + j + r * CONV_SUB, CONV_SUB), :]
            y_scr[c, r * CONV_SUB:(r + 1) * CONV_SUB, :] = acc
        return carry

    lax.fori_loop(0, nchunk, chunk_body, 0)

    s1 = jnp.zeros((ts, 1), F32)
    for c in range(nchunk):
        s1 = s1 + jnp.sum(y_scr[c], axis=-1, keepdims=True)
    mu = s1 * (1.0 / CONV_CH)
    s2 = jnp.zeros((ts, 1), F32)
    for c in range(nchunk):
        d = y_scr[c] - mu
        s2 = s2 + jnp.sum(d * d, axis=-1, keepdims=True)
    rstd = lax.rsqrt(s2 * (1.0 / CONV_CH) + LN_EPS)
    for c in range(nchunk):
        sl = slice(c * LANES, (c + 1) * LANES)
        z = (y_scr[c] - mu) * rstd * lg_ref[:, sl] + lb_ref[:, sl]
        o_ref[:, sl] = (z * jax.nn.sigmoid(z)).astype(o_ref.dtype)


def conv_mixer(proj, dw, dw_b, ln_g, ln_b):
    B, S, _ = proj.shape
    ts = min(CONV_TS, S)
    nchunk = CONV_CH // LANES
    hb = ts // CONV_HALO
    dw_p = jnp.zeros((CONV_HALO, CONV_CH), F32).at[:CONV_WIDTH].set(dw.astype(F32))
    dw_c = dw_p.reshape(CONV_HALO, nchunk, LANES).transpose(1, 0, 2)
    dwb_c = dw_b.astype(F32).reshape(nchunk, 1, LANES)
    halo_idx = lambda b, i: (b, jnp.maximum(i * hb - 1, 0), 0)
    halo_idx_g = lambda b, i: (b, jnp.maximum(i * hb - 1, 0), 1)
    return pl.pallas_call(
        _conv_kernel,
        grid=(B, S // ts),
        in_specs=[
            pl.BlockSpec((None, ts, CONV_CH), lambda b, i: (b, i, 0)),
            pl.BlockSpec((None, ts, CONV_CH), lambda b, i: (b, i, 1)),
            pl.BlockSpec((None, CONV_HALO, CONV_CH), halo_idx),
            pl.BlockSpec((None, CONV_HALO, CONV_CH), halo_idx_g),
            pl.BlockSpec((nchunk, CONV_HALO, LANES), lambda b, i: (0, 0, 0)),
            pl.BlockSpec((nchunk, 1, LANES), lambda b, i: (0, 0, 0)),
            pl.BlockSpec((1, CONV_CH), lambda b, i: (0, 0)),
            pl.BlockSpec((1, CONV_CH), lambda b, i: (0, 0)),
        ],
        out_specs=pl.BlockSpec((None, ts, CONV_CH), lambda b, i: (b, i, 0)),
        out_shape=jax.ShapeDtypeStruct((B, S, CONV_CH), BF16),
        scratch_shapes=[pltpu.VMEM((nchunk, CONV_HALO + ts, LANES), F32),
                        pltpu.VMEM((nchunk, ts, LANES), F32)],
        compiler_params=_cparams("arbitrary", "arbitrary"),
        name="conv_mixer",
    )(proj, proj, proj, proj, dw_c, dwb_c, ln_g.astype(F32).reshape(1, -1), ln_b.astype(F32).reshape(1, -1))


def _fgate_kernel(h_ref, w_ref, b_ref, c_ref, carry_ref):
    ts = h_ref.shape[0]

    @pl.when(pl.program_id(1) == 0)
    def _():
        carry_ref[...] = jnp.zeros_like(carry_ref)

    f = jnp.dot(h_ref[...], w_ref[...], preferred_element_type=F32) + b_ref[...]
    ls = jax.nn.log_sigmoid(f)
    row = lax.broadcasted_iota(jnp.int32, (ts, ts), 0)
    col = lax.broadcasted_iota(jnp.int32, (ts, ts), 1)
    tri = jnp.where(row >= col, 1.0, 0.0).astype(BF16)
    hi = ls.astype(BF16)
    r1 = ls - hi.astype(F32)
    mid = r1.astype(BF16)
    lo = (r1 - mid.astype(F32)).astype(BF16)
    cs = (jnp.dot(tri, hi, preferred_element_type=F32)
          + jnp.dot(tri, mid, preferred_element_type=F32)
          + jnp.dot(tri, lo, preferred_element_type=F32))
    c = cs + carry_ref[...]
    c_ref[...] = c
    carry_ref[...] = c[ts - 1:ts, :]


def forget_cumlog(h_bf, w_f, b_f):
    B, S, D = h_bf.shape
    H = w_f.shape[1]
    ts = min(256, S)
    w_p = jnp.zeros((D, LANES), BF16).at[:, :H].set(w_f.astype(BF16))
    b_p = jnp.zeros((1, LANES), F32).at[0, :H].set(b_f.astype(F32))
    return pl.pallas_call(
        _fgate_kernel,
        grid=(B, S // ts),
        in_specs=[pl.BlockSpec((None, ts, D), lambda b, i: (b, i, 0)),
                  pl.BlockSpec((D, LANES), lambda b, i: (0, 0)),
                  pl.BlockSpec((1, LANES), lambda b, i: (0, 0))],
        out_specs=pl.BlockSpec((None, ts, LANES), lambda b, i: (b, i, 0)),
        out_shape=jax.ShapeDtypeStruct((B, S, LANES), F32),
        scratch_shapes=[pltpu.VMEM((1, LANES), F32)],
        compiler_params=_cparams("arbitrary", "arbitrary"),
        name="forget_cumlog",
    )(h_bf, w_p, b_p)


def _fox_kernel(q_ref, k_ref, v_ref, cq_ref, ck_ref, o_ref):
    tq = q_ref.shape[0]
    tk = ck_ref.shape[-1]
    i = pl.program_id(2)
    qscale = HEAD_DIM ** -0.5 * LOG2E
    heads = range(FOX_HP)
    hs = [slice(h * HEAD_DIM, (h + 1) * HEAD_DIM) for h in heads]
    qs = [(q_ref[:, hs[h]].astype(F32) * qscale).astype(BF16) for h in heads]
    cqs = [jnp.concatenate([jnp.broadcast_to(cq_ref[h, r], (LANES, LANES)).T[:, 0:1] for r in range(tq // LANES)],
                           axis=0) for h in heads]

    def step(j, carry, masked):
        start = pl.multiple_of(j * tk, tk)
        ss = []
        for h in heads:
            k = k_ref[pl.ds(start, tk), hs[h]]
            ss.append(lax.dot_general(qs[h], k, (((1,), (1,)), ((), ())), preferred_element_type=F32))
        if masked:
            row = i * tq + lax.broadcasted_iota(jnp.int32, (tq, tk), 0)
            col = j * tk + lax.broadcasted_iota(jnp.int32, (tq, tk), 1)
            causal = row >= col
        ps, ms, alphas = [], [], []
        for h in heads:
            m = carry[h][0]
            s = ss[h] - ck_ref[h, j]
            if masked:
                s = jnp.where(causal, s, -jnp.inf)
            m_new = jnp.maximum(m, jnp.max(s, axis=-1, keepdims=True) + cqs[h])
            p = jnp.exp2(s - (m_new - cqs[h]))
            alpha = jnp.exp2(m - m_new)
            ps.append(p.astype(BF16))
            ms.append(m_new)
            alphas.append(alpha)
        out = []
        for h in heads:
            v1 = jnp.concatenate([v_ref[pl.ds(start, tk), hs[h]], ones], axis=-1)
            acc = alphas[h] * carry[h][1] + jnp.dot(ps[h], v1, preferred_element_type=F32)
            out.append((ms[h], acc))
        return tuple(out)

    ones = jnp.ones((tk, HEAD_DIM), BF16)
    init = tuple((jnp.full((tq, 1), -jnp.inf, F32), jnp.zeros((tq, 2 * HEAD_DIM), F32)) for _ in heads)
    n_full = (i * tq) // tk
    carry = lax.fori_loop(0, n_full, lambda j, c: step(j, c, False), init)
    for d in range(-(-tq // tk)):
        carry = step(n_full + d, carry, True)
    for h in heads:
        acc = carry[h][1]
        o_ref[:, hs[h]] = (acc[:, :HEAD_DIM] / acc[:, HEAD_DIM:]).astype(o_ref.dtype)


def fox_attention(proj, c):
    B, S, _ = proj.shape
    H = N_FOX_HEADS
    tq, tk = min(FOX_TQ, S), min(FOX_TK, S)
    nq, nk = S // tq, S // tk
    ng = H // FOX_HP
    w = FOX_HP * HEAD_DIM
    c_h = jnp.transpose(c[:, :, :H], (0, 2, 1)) * LOG2E
    c_q = c_h.reshape(B, H, S // LANES, 1, LANES)
    c_row = c_h.reshape(B, H, nk, 1, tk)
    return pl.pallas_call(
        _fox_kernel,
        grid=(B, ng, nq),
        in_specs=[
            pl.BlockSpec((None, tq, w), lambda b, g, i: (b, i, g)),
            pl.BlockSpec((None, S, w), lambda b, g, i: (b, 0, ng + g)),
            pl.BlockSpec((None, S, w), lambda b, g, i: (b, 0, 2 * ng + g)),
            pl.BlockSpec((None, FOX_HP, tq // LANES, 1, LANES), lambda b, g, i: (b, g, i, 0, 0)),
            pl.BlockSpec((None, FOX_HP, nk, 1, tk), lambda b, g, i: (b, g, 0, 0, 0)),
        ],
        out_specs=pl.BlockSpec((None, tq, w), lambda b, g, i: (b, i, g)),
        out_shape=jax.ShapeDtypeStruct((B, S, PRIMARY_WIDTH), BF16),
        compiler_params=_cparams("arbitrary", "arbitrary", "arbitrary"),
        name="fox_attention",
    )(proj, proj, proj, c_q, c_row)


def _mem_attn_kernel(q_ref, kv_ref, o_ref):
    scale = HEAD_DIM ** -0.5
    for h in range(N_MEM_HEADS):
        sl = slice(h * HEAD_DIM, (h + 1) * HEAD_DIM)
        slv = slice(MEM_WIDTH + h * HEAD_DIM, MEM_WIDTH + (h + 1) * HEAD_DIM)
        s = lax.dot_general(q_ref[:, sl], kv_ref[:, sl], (((1,), (1,)), ((), ())),
                            preferred_element_type=F32) * scale
        m = jnp.max(s, axis=-1, keepdims=True)
        p = jnp.exp(s - m)
        l = jnp.sum(p, axis=-1, keepdims=True)
        o = jnp.dot(p.astype(BF16), kv_ref[:, slv], preferred_element_type=F32)
        o_ref[:, sl] = (o / l).astype(o_ref.dtype)


def memory_attention(proj, q_block, kv):
    B, S, _ = proj.shape
    M = kv.shape[1]
    tq = min(MEM_TQ, S)
    return pl.pallas_call(
        _mem_attn_kernel,
        grid=(B, S // tq),
        in_specs=[pl.BlockSpec((None, tq, MEM_WIDTH), lambda b, i: (b, i, q_block)),
                  pl.BlockSpec((None, M, 2 * MEM_WIDTH), lambda b, i: (b, 0, 0))],
        out_specs=pl.BlockSpec((None, tq, MEM_WIDTH), lambda b, i: (b, i, 0)),
        out_shape=jax.ShapeDtypeStruct((B, S, MEM_WIDTH), BF16),
        compiler_params=_cparams("arbitrary", "arbitrary"),
        name="memory_attention",
    )(proj, kv)


def _outproj_ln_kernel(p_ref, m_ref, wt_ref, wb_ref, h_ref, g_ref, b_ref, of_ref, ob_ref, wt_bf, wb_bf):
    @pl.when(pl.program_id(0) == 0)
    def _():
        wt_bf[...] = wt_ref[...].astype(BF16)
        wb_bf[...] = wb_ref[...].astype(BF16)

    mix = (jnp.dot(p_ref[...], wt_bf[...], preferred_element_type=F32)
           + jnp.dot(m_ref[...], wb_bf[...], preferred_element_type=F32))
    y = _ln_rows(DEEPNORM_ALPHA * h_ref[...] + mix, g_ref[...], b_ref[...])
    of_ref[...] = y
    ob_ref[...] = y.astype(BF16)


def outproj_ln(prim, memo, w_out, layer, h, g, b):
    N, D = h.shape
    tm = min(ROW_TILE, N)
    row = lambda i: (i, 0)
    const = lambda i: (0, 0)
    once = pl.Buffered(1)
    return pl.pallas_call(
        _outproj_ln_kernel,
        grid=(N // tm,),
        in_specs=[pl.BlockSpec((tm, PRIMARY_WIDTH), row), pl.BlockSpec((tm, MEM_WIDTH), row),
                  pl.BlockSpec((None, PRIMARY_WIDTH, D), lambda i: (layer, 0, 0), pipeline_mode=once),
                  pl.BlockSpec((None, MEM_WIDTH, D), lambda i: (layer, PRIMARY_WIDTH // MEM_WIDTH, 0),
                               pipeline_mode=once),
                  pl.BlockSpec((tm, D), row), pl.BlockSpec((1, D), const), pl.BlockSpec((1, D), const)],
        out_specs=[pl.BlockSpec((tm, D), row), pl.BlockSpec((tm, D), row)],
        out_shape=[jax.ShapeDtypeStruct((N, D), F32), jax.ShapeDtypeStruct((N, D), BF16)],
        scratch_shapes=[pltpu.VMEM((PRIMARY_WIDTH, D), BF16), pltpu.VMEM((MEM_WIDTH, D), BF16)],
        compiler_params=_cparams("arbitrary"),
        name="outproj_ln",
    )(prim, memo, w_out, w_out, h, g.astype(F32).reshape(1, D), b.astype(F32).reshape(1, D))


def _route_kernel(h_ref, w_ref, b_ref, idx_ref, rank_ref, gate_ref, cnt_ref, carry_ref):
    T = h_ref.shape[0]
    E = w_ref.shape[0]
    gsz = E // N_GROUPS
    neg = -jnp.inf

    @pl.when(pl.program_id(0) == 0)
    def _():
        carry_ref[...] = jnp.zeros_like(carry_ref)

    logits = lax.dot_general(w_ref[...], h_ref[...], (((1,), (1,)), ((), ())),
                             precision=lax.Precision.HIGHEST, preferred_element_type=F32)
    scores = jax.nn.sigmoid(logits)
    choice = scores + b_ref[...]

    io8 = lax.broadcasted_iota(jnp.int32, (gsz, T), 0)
    rows = []
    for g in range(N_GROUPS):
        blk = choice[g * gsz:(g + 1) * gsz, :]
        m1 = jnp.max(blk, axis=0, keepdims=True)
        f1 = jnp.min(jnp.where(blk == m1, io8, gsz), axis=0, keepdims=True)
        m2 = jnp.max(jnp.where(io8 == f1, neg, blk), axis=0, keepdims=True)
        rows.append(jnp.broadcast_to(m1 + m2, (gsz, T)))
    gs = jnp.concatenate(rows, axis=0)
    eid = lax.broadcasted_iota(jnp.int32, (E, T), 0)
    gid = eid // gsz
    emask = jnp.zeros((E, T), jnp.bool_)
    for _ in range(TOPK_GROUPS):
        m = jnp.max(gs, axis=0, keepdims=True)
        g = jnp.min(jnp.where(gs == m, gid, N_GROUPS), axis=0, keepdims=True)
        hit = gid == g
        emask = jnp.logical_or(emask, hit)
        gs = jnp.where(hit, neg, gs)

    masked = jnp.where(emask, choice, neg)
    sel = jnp.zeros((E, T), jnp.bool_)
    ids, ws, hits = [], [], []
    for _ in range(TOP_K):
        m = jnp.max(masked, axis=0, keepdims=True)
        i_k = jnp.min(jnp.where(masked == m, eid, E), axis=0, keepdims=True)
        hit = eid == i_k
        ids.append(i_k)
        ws.append(jnp.sum(jnp.where(hit, scores, 0.0), axis=0, keepdims=True))
        hits.append(hit)
        masked = jnp.where(hit, neg, masked)
        sel = jnp.logical_or(sel, hit)
    wsum = ws[0]
    for w in ws[1:]:
        wsum = wsum + w

    sel_f = jnp.where(sel, 1.0, 0.0)
    r = lax.broadcasted_iota(jnp.int32, (T, T), 0)
    c = lax.broadcasted_iota(jnp.int32, (T, T), 1)
    tri = jnp.where(r < c, 1.0, 0.0).astype(BF16)
    before = jnp.dot(sel_f.astype(BF16), tri, preferred_element_type=F32) + carry_ref[...]
    for k in range(TOP_K):
        idx_ref[k:k + 1, :] = ids[k]
        rank_ref[k:k + 1, :] = jnp.sum(jnp.where(hits[k], before, 0.0), axis=0, keepdims=True).astype(jnp.int32)
        gate_ref[k:k + 1, :] = ws[k] / wsum * ROUTED_SCALE
    total = carry_ref[...] + jnp.sum(sel_f, axis=1, keepdims=True)
    carry_ref[...] = total
    cnt_ref[...] = jnp.broadcast_to(total, cnt_ref.shape).astype(jnp.int32)


def route(h, w_r, r_bias):
    N, D = h.shape
    E = w_r.shape[1]
    T = min(ROUTE_T, N)
    kn = lambda i: (0, i)
    idx, rank, gate, cnt = pl.pallas_call(
        _route_kernel,
        grid=(N // T,),
        in_specs=[pl.BlockSpec((T, D), lambda i: (i, 0)), pl.BlockSpec((E, D), lambda i: (0, 0)),
                  pl.BlockSpec((E, 1), lambda i: (0, 0))],
        out_specs=[pl.BlockSpec((TOP_K, T), kn), pl.BlockSpec((TOP_K, T), kn), pl.BlockSpec((TOP_K, T), kn),
                   pl.BlockSpec((E, LANES), lambda i: (0, 0))],
        out_shape=[jax.ShapeDtypeStruct((TOP_K, N), jnp.int32), jax.ShapeDtypeStruct((TOP_K, N), jnp.int32),
                   jax.ShapeDtypeStruct((TOP_K, N), F32), jax.ShapeDtypeStruct((E, LANES), jnp.int32)],
        scratch_shapes=[pltpu.VMEM((E, 1), F32)],
        compiler_params=_cparams("arbitrary"),
        name="route",
    )(h, w_r.astype(F32).T, r_bias.astype(F32).reshape(E, 1))
    return idx, rank, gate, cnt[:, 0]


def _pos_kernel(starts_ref, idx_ref, rank_ref, pos_ref):
    idx = idx_ref[...]
    acc = rank_ref[...]
    for e in range(N_EXPERTS):
        acc = acc + jnp.where(idx == e, starts_ref[e], 0)
    pos_ref[...] = acc


def slot_positions(starts, idx, rank):
    K, N = idx.shape
    T = min(2048, N)
    grid_spec = pltpu.PrefetchScalarGridSpec(
        num_scalar_prefetch=1, grid=(N // T,),
        in_specs=[pl.BlockSpec((K, T), lambda i, s: (0, i)), pl.BlockSpec((K, T), lambda i, s: (0, i))],
        out_specs=pl.BlockSpec((K, T), lambda i, s: (0, i)))
    return pl.pallas_call(
        _pos_kernel, grid_spec=grid_spec, out_shape=jax.ShapeDtypeStruct((K, N), jnp.int32),
        compiler_params=_cparams("arbitrary"), name="slot_positions",
    )(starts.astype(jnp.int32), idx, rank)


def _pack_pair(a, b):
    ua = lax.bitcast_convert_type(a.astype(BF16).astype(F32), jnp.uint32)
    ub = lax.bitcast_convert_type(b.astype(BF16).astype(F32), jnp.uint32)
    return ub | (ua >> 16)


def _unpack_pair(w):
    lo = lax.bitcast_convert_type(w << 16, F32)
    hi = lax.bitcast_convert_type(w & jnp.uint32(0xFFFF0000), F32)
    return lo, hi


def _rows_to_slab(x):
    return [_pack_pair(x[:, j * LANES:(j + 1) * LANES], x[:, (j + SLAB_ROWS) * LANES:(j + SLAB_ROWS + 1) * LANES])
            for j in range(SLAB_ROWS)]


def _slab_to_rows(pieces):
    halves = [_unpack_pair(w) for w in pieces]
    return jnp.concatenate([h[0] for h in halves] + [h[1] for h in halves], axis=-1)


def _tiles(a, T):
    K, N = a.shape
    return a.reshape(K, N // T, T).transpose(1, 0, 2)


def _swiglu_bf(x, wu, wd):
    f = wd.shape[0]
    h = jnp.dot(x, wu, preferred_element_type=F32)
    a = h[:, :f]
    act = (a * jax.nn.sigmoid(a)) * h[:, f:]
    return jnp.dot(act.astype(BF16), wd, preferred_element_type=F32)


def _dispatch_kernel(pos_ref, hb_ref, su_ref, sd_ref, sh_ref, xs_hbm, slab, su_bf, sd_bf, sem):
    T = hb_ref.shape[0]

    @pl.when(pl.program_id(0) == 0)
    def _():
        su_bf[...] = su_ref[...].astype(BF16)
        sd_bf[...] = sd_ref[...].astype(BF16)

    hb = hb_ref[...]
    for j, piece in enumerate(_rows_to_slab(hb.astype(F32))):
        slab[pl.ds(j, T, stride=SLAB_ROWS), :] = piece

    def issue(t, carry):
        src = slab.at[pl.ds(pl.multiple_of(t * SLAB_ROWS, SLAB_ROWS), SLAB_ROWS)]
        for k in range(TOP_K):
            dst = xs_hbm.at[pl.ds(pl.multiple_of(pos_ref[k, t] * SLAB_ROWS, SLAB_ROWS), SLAB_ROWS)]
            pltpu.make_async_copy(src, dst, sem).start(priority=k % 2)
        return carry

    lax.fori_loop(0, T, issue, 0)
    sh_ref[...] = _swiglu_bf(hb, su_bf[...], sd_bf[...]).astype(sh_ref.dtype)
    for k in range(TOP_K):
        pltpu.make_async_copy(slab, xs_hbm.at[pl.ds(0, T * SLAB_ROWS)], sem).wait()


def dispatch_shared(pos_t, h_bf, sh_up, sh_down, layer):
    N, D = h_bf.shape
    nt, K, T = pos_t.shape
    F2 = sh_up.shape[-1]
    Fd = sh_down.shape[1]
    return pl.pallas_call(
        _dispatch_kernel,
        grid=(nt,),
        in_specs=[pl.BlockSpec((None, K, T), lambda i: (i, 0, 0), memory_space=pltpu.SMEM),
                  pl.BlockSpec((T, D), lambda i: (i, 0)),
                  pl.BlockSpec((None, D, F2), lambda i: (layer, 0, 0)),
                  pl.BlockSpec((None, Fd, D), lambda i: (layer, 0, 0))],
        out_specs=[pl.BlockSpec((T, D), lambda i: (i, 0)), pl.BlockSpec(memory_space=pl.ANY)],
        out_shape=[jax.ShapeDtypeStruct((N, D), BF16),
                   jax.ShapeDtypeStruct((N * K * SLAB_ROWS, LANES), jnp.uint32)],
        scratch_shapes=[pltpu.VMEM((T * SLAB_ROWS, LANES), jnp.uint32), pltpu.VMEM((D, F2), BF16),
                        pltpu.VMEM((Fd, D), BF16), pltpu.SemaphoreType.DMA(())],
        compiler_params=_cparams("arbitrary"),
        name="dispatch_shared",
    )(pos_t, h_bf, sh_up, sh_down)


def _expert_kernel(layer, tile_ref, be_ref, lo_ref, hi_ref, first_ref, slot_ref, next_ref, x_ref, wu_hbm, wd_hbm,
                   o_ref, wu_f32, wd_f32, wu_bf, wd_bf, sem):
    v = pl.program_id(0)
    tm = x_ref.shape[0] // SLAB_ROWS
    lo = lo_ref[v]
    hi = hi_ref[v]

    def weight_copies(e, slot):
        return (pltpu.make_async_copy(wu_hbm.at[layer, e], wu_f32.at[slot], sem.at[0, slot]),
                pltpu.make_async_copy(wd_hbm.at[layer, e], wd_f32.at[slot], sem.at[1, slot]))

    @pl.when(hi > lo)
    def _():
        changed = jnp.logical_or(v == 0, be_ref[v] != be_ref[jnp.maximum(v - 1, 0)])

        @pl.when(changed)
        def _():
            slot = slot_ref[v]

            @pl.when(v == 0)
            def _():
                for c in weight_copies(be_ref[v], slot):
                    c.start()

            for c in weight_copies(be_ref[v], slot):
                c.wait()

            @pl.when(next_ref[v] >= 0)
            def _():
                for c in weight_copies(next_ref[v], 1 - slot):
                    c.start()

            wu_bf[...] = wu_f32[slot].astype(BF16)
            wd_bf[...] = wd_f32[slot].astype(BF16)

        x = _slab_to_rows([x_ref[pl.ds(j, tm, stride=SLAB_ROWS), :] for j in range(SLAB_ROWS)])
        y = _swiglu_bf(x.astype(BF16), wu_bf[...], wd_bf[...])
        row = lax.broadcasted_iota(jnp.int32, (tm, 1), 0)
        mine = jnp.logical_and(row >= lo, row < hi)
        pieces = _rows_to_slab(y)

        @pl.when(first_ref[v] == 1)
        def _():
            for j in range(SLAB_ROWS):
                o_ref[pl.ds(j, tm, stride=SLAB_ROWS), :] = jnp.where(mine, pieces[j], jnp.uint32(0))

        @pl.when(first_ref[v] == 0)
        def _():
            for j in range(SLAB_ROWS):
                keep = o_ref[pl.ds(j, tm, stride=SLAB_ROWS), :]
                o_ref[pl.ds(j, tm, stride=SLAB_ROWS), :] = jnp.where(mine, pieces[j], keep)


def _lookup(table, idx):
    n = table.shape[0]
    return jnp.sum(jnp.where(idx[:, None] == jnp.arange(n)[None, :], table[None, :], 0), axis=1)


def _visit_tables(counts, tm, A):
    E = counts.shape[0]
    V = A // tm + E
    starts = jnp.cumsum(counts) - counts
    ends = starts + counts
    first_tile = starts // tm
    nvis = jnp.where(counts > 0, (ends - 1) // tm - first_tile + 1, 0)
    vend = jnp.cumsum(nvis)
    voff = vend - nvis
    total = vend[-1]
    v = jnp.arange(V, dtype=jnp.int32)
    e_v = jnp.minimum(jnp.sum(vend[None, :] <= v[:, None], axis=1), E - 1).astype(jnp.int32)
    valid = v < total
    e_last = jnp.sum(jnp.where(v == total - 1, e_v, 0))
    e_v = jnp.where(valid, e_v, e_last)
    tile_v = jnp.where(valid, _lookup(first_tile, e_v) + v - _lookup(voff, e_v), A // tm - 1)
    base = tile_v * tm
    lo_v = jnp.where(valid, jnp.maximum(_lookup(starts, e_v), base) - base, 0)
    hi_v = jnp.where(valid, jnp.minimum(_lookup(ends, e_v), base + tm) - base, 0)
    prev_tile = jnp.concatenate([jnp.full((1,), -1, tile_v.dtype), tile_v[:-1]])
    first_v = jnp.logical_and(valid, tile_v != prev_tile)
    has = nvis > 0
    ids = jnp.arange(E)
    slot_e = (jnp.cumsum(has) - 1) & 1
    later = jnp.logical_and(ids[None, :] > ids[:, None], has[None, :])
    next_e = jnp.min(jnp.where(later, ids[None, :], E), axis=1)
    next_e = jnp.where(next_e == E, -1, next_e)
    i32 = lambda a: a.astype(jnp.int32)
    return starts, (i32(tile_v), i32(e_v), i32(lo_v), i32(hi_v), i32(first_v),
                    i32(_lookup(slot_e, e_v)), i32(_lookup(next_e, e_v)))


def expert_ffn(xs, tables, w_up, w_down, layer):
    V = tables[0].shape[0]
    D, F2 = w_up.shape[-2:]
    Fd = w_down.shape[-2]
    blk = EXP_TM * SLAB_ROWS
    nt = len(tables)
    tile_map = lambda v, *t: (t[0][v], 0)
    grid_spec = pltpu.PrefetchScalarGridSpec(
        num_scalar_prefetch=nt,
        grid=(V,),
        in_specs=[pl.BlockSpec((blk, LANES), tile_map),
                  pl.BlockSpec(memory_space=pl.ANY), pl.BlockSpec(memory_space=pl.ANY)],
        out_specs=pl.BlockSpec((blk, LANES), tile_map),
        scratch_shapes=[pltpu.VMEM((2, D, F2), F32), pltpu.VMEM((2, Fd, D), F32),
                        pltpu.VMEM((D, F2), BF16), pltpu.VMEM((Fd, D), BF16),
                        pltpu.SemaphoreType.DMA((2, 2))],
    )
    return pl.pallas_call(
        functools.partial(_expert_kernel, layer),
        grid_spec=grid_spec,
        out_shape=jax.ShapeDtypeStruct(xs.shape, jnp.uint32),
        compiler_params=_cparams("arbitrary"),
        name="expert_ffn",
    )(*tables, xs, w_up, w_down)


def _combine_ln_kernel(pos_ref, nxt_ref, gate_ref, h_ref, sh_ref, g_ref, b_ref, y_hbm, of_ref, ob_ref,
                       ybuf, acc_lo, acc_hi, sem):
    T = h_ref.shape[0]
    i = pl.program_id(0)
    slot = i % 2

    def issue_tile(p_ref, s):
        def issue(t, carry):
            dst0 = pl.multiple_of(t * SLAB_ROWS, SLAB_ROWS)
            for k in range(TOP_K):
                src0 = pl.multiple_of(p_ref[k, t] * SLAB_ROWS, SLAB_ROWS)
                pltpu.make_async_copy(y_hbm.at[pl.ds(src0, SLAB_ROWS)], ybuf.at[s, k, pl.ds(dst0, SLAB_ROWS)],
                                      sem.at[s]).start(priority=k % 2)
            return carry

        lax.fori_loop(0, T, issue, 0)

    @pl.when(i == 0)
    def _():
        issue_tile(pos_ref, slot)

    @pl.when(i + 1 < pl.num_programs(0))
    def _():
        issue_tile(nxt_ref, 1 - slot)

    for k in range(TOP_K):
        pltpu.make_async_copy(y_hbm.at[pl.ds(0, T * SLAB_ROWS)], ybuf.at[slot, k], sem.at[slot]).wait()

    def mix(t, carry):
        r0 = pl.multiple_of(t * SLAB_ROWS, SLAB_ROWS)
        lo, hi = _unpack_pair(ybuf[slot, 0, pl.ds(r0, SLAB_ROWS), :])
        g = gate_ref[0, t]
        lo, hi = g * lo, g * hi
        for k in range(1, TOP_K):
            l2, h2 = _unpack_pair(ybuf[slot, k, pl.ds(r0, SLAB_ROWS), :])
            g = gate_ref[k, t]
            lo, hi = lo + g * l2, hi + g * h2
        acc_lo[pl.ds(r0, SLAB_ROWS), :] = lo
        acc_hi[pl.ds(r0, SLAB_ROWS), :] = hi
        return carry

    lax.fori_loop(0, T, mix, 0)
    routed = jnp.concatenate([acc_lo[pl.ds(j, T, stride=SLAB_ROWS), :] for j in range(SLAB_ROWS)]
                             + [acc_hi[pl.ds(j, T, stride=SLAB_ROWS), :] for j in range(SLAB_ROWS)], axis=-1)
    y = DEEPNORM_ALPHA * h_ref[...] + routed + sh_ref[...].astype(F32)
    y = _ln_rows(y, g_ref[...], b_ref[...])
    of_ref[...] = y
    ob_ref[...] = y.astype(BF16)


def combine_ln(pos_t, gate_t, h, shared, y, g, b):
    N, D = h.shape
    nt, K, T = pos_t.shape
    row = lambda i: (i, 0)
    const = lambda i: (0, 0)
    return pl.pallas_call(
        _combine_ln_kernel,
        grid=(nt,),
        in_specs=[pl.BlockSpec((None, K, T), lambda i: (i, 0, 0), memory_space=pltpu.SMEM),
                  pl.BlockSpec((None, K, T), lambda i: (jnp.minimum(i + 1, nt - 1), 0, 0), memory_space=pltpu.SMEM),
                  pl.BlockSpec((None, K, T), lambda i: (i, 0, 0), memory_space=pltpu.SMEM),
                  pl.BlockSpec((T, D), row), pl.BlockSpec((T, D), row),
                  pl.BlockSpec((1, D), const), pl.BlockSpec((1, D), const),
                  pl.BlockSpec(memory_space=pl.ANY)],
        out_specs=[pl.BlockSpec((T, D), row), pl.BlockSpec((T, D), row)],
        out_shape=[jax.ShapeDtypeStruct((N, D), F32), jax.ShapeDtypeStruct((N, D), BF16)],
        scratch_shapes=[pltpu.VMEM((2, K, T * SLAB_ROWS, LANES), jnp.uint32), pltpu.VMEM((T * SLAB_ROWS, LANES), F32),
                        pltpu.VMEM((T * SLAB_ROWS, LANES), F32), pltpu.SemaphoreType.DMA((2,))],
        compiler_params=_cparams("arbitrary"),
        name="combine_ln",
    )(pos_t, pos_t, gate_t, h, shared, g.astype(F32).reshape(1, D), b.astype(F32).reshape(1, D), y)


def moe_ffn_ln(h, h_bf, layer, router_w, router_bias, exp_w_up, exp_w_down, shared_w_up, shared_w_down, g, b):
    N, D = h.shape
    idx, rank, gate, counts = route(h, router_w[layer], router_bias[layer])
    starts, tables = _visit_tables(counts, EXP_TM, N * TOP_K)
    pos = slot_positions(starts, idx, rank)
    shared, xs = dispatch_shared(_tiles(pos, min(DISPATCH_T, N)), h_bf, shared_w_up, shared_w_down, layer)
    y = expert_ffn(xs, tables, exp_w_up, exp_w_down, layer)
    tc = min(COMBINE_T, N)
    return combine_ln(_tiles(pos, tc), _tiles(gate, tc), h, shared, y, g, b)


def kernel(x, mem, conv_w_in, conv_dw, conv_dw_b, conv_ln_g, conv_ln_b, fox_w_in, fox_b_f, mem_w_kv, w_out,
           ln_g, ln_b, router_w, router_bias, exp_w_up, exp_w_down, shared_w_up, shared_w_down):
    B, S, D = x.shape
    N = B * S
    h = x.reshape(N, D).astype(F32)
    h_bf = h.astype(BF16)
    mem_bf = mem.reshape(B * MEM_LEN, D).astype(BF16)
    nf = 3 * PRIMARY_WIDTH
    for i in range(DEPTH):
        j = i // N_MIXERS
        if i % N_MIXERS == 0:
            proj = matmul(h_bf, conv_w_in, j, 2 * CONV_CH + MEM_WIDTH, BF16).reshape(B, S, -1)
            prim = conv_mixer(proj, conv_dw[j], conv_dw_b[j], conv_ln_g[j], conv_ln_b[j])
            q_src, q_block = proj, 2 * CONV_CH // MEM_WIDTH
        else:
            proj = matmul(h_bf, fox_w_in, j, nf, BF16).reshape(B, S, nf)
            w_qm = fox_w_in[j:j + 1, :, nf + N_FOX_HEADS:]
            q_src, q_block = matmul(h_bf, w_qm, 0, MEM_WIDTH, BF16).reshape(B, S, MEM_WIDTH), 0
            c = forget_cumlog(h_bf.reshape(B, S, D), fox_w_in[j, :, nf:nf + N_FOX_HEADS], fox_b_f[j])
            prim = fox_attention(proj, c)
        kv = matmul(mem_bf, mem_w_kv, i, 2 * MEM_WIDTH, BF16).reshape(B, MEM_LEN, 2 * MEM_WIDTH)
        memo = memory_attention(q_src, q_block, kv)
        h, h_bf = outproj_ln(prim.reshape(N, PRIMARY_WIDTH), memo.reshape(N, MEM_WIDTH), w_out, i, h,
                             ln_g[i, 0], ln_b[i, 0])
        h, h_bf = moe_ffn_ln(h, h_bf, i, router_w, router_bias, exp_w_up, exp_w_down,
                             shared_w_up, shared_w_down, ln_g[i, 1], ln_b[i, 1])
    return h.reshape(B, S, D)
```

```python
import functools

import jax
import jax.numpy as jnp
from jax import lax
from jax.experimental import pallas as pl
from jax.experimental.pallas import tpu as pltpu

F32 = jnp.float32
BF16 = jnp.bfloat16

D_MODEL = 2048
DEPTH = 4
N_MIXERS = 2
MEM_LEN = 256
HEAD_DIM = 128
N_MEM_HEADS = 4
MEM_WIDTH = N_MEM_HEADS * HEAD_DIM
PRIMARY_WIDTH = D_MODEL - MEM_WIDTH
CONV_CH = PRIMARY_WIDTH
CONV_WIDTH = 31
N_FOX_HEADS = PRIMARY_WIDTH // HEAD_DIM
N_EXPERTS = 64
TOP_K = 8
N_GROUPS = 8
TOPK_GROUPS = 4
D_EXPERT = 512
ROUTED_SCALE = 2.5
LN_EPS = 1e-5
DEEPNORM_ALPHA = (2 * DEPTH) ** 0.25
LOG2E = 1.4426950408889634

LANES = 128
VMEM_LIMIT = 56 * 1024 * 1024
MM_TM, MM_TN = 1024, 512
ROW_TILE = 256
CONV_TS = 256
CONV_HALO = 32
CONV_SUB = 64
FOX_TQ = 256
FOX_TK = 512
FOX_HP = 2
MEM_TQ = 512
EXP_TM = 256
SLAB_ROWS = D_MODEL // LANES // 2
ROUTE_T = 256
DISPATCH_T = 256
COMBINE_T = 128


def _cparams(*sem):
    return pltpu.CompilerParams(dimension_semantics=sem, vmem_limit_bytes=VMEM_LIMIT)


def _ln_rows(y, g, b):
    mu = jnp.mean(y, axis=-1, keepdims=True)
    d = y - mu
    var = jnp.mean(d * d, axis=-1, keepdims=True)
    return d * lax.rsqrt(var + LN_EPS) * g + b


def _mm_kernel(a_ref, w_ref, o_ref, w_bf):
    @pl.when(pl.program_id(1) == 0)
    def _():
        w_bf[...] = w_ref[...].astype(BF16)

    o_ref[...] = jnp.dot(a_ref[...], w_bf[...], preferred_element_type=F32).astype(o_ref.dtype)


def matmul(a, w, layer, n_cols, out_dtype, tm=MM_TM, tn=MM_TN):
    M, K = a.shape
    tm, tn = min(tm, M), min(tn, n_cols)
    assert M % tm == 0 and n_cols % tn == 0 and w.shape[1] == K
    return pl.pallas_call(
        _mm_kernel,
        grid=(n_cols // tn, M // tm),
        in_specs=[pl.BlockSpec((tm, K), lambda j, i: (i, 0)),
                  pl.BlockSpec((None, K, tn), lambda j, i: (layer, 0, j))],
        out_specs=pl.BlockSpec((tm, tn), lambda j, i: (i, j)),
        out_shape=jax.ShapeDtypeStruct((M, n_cols), out_dtype),
        scratch_shapes=[pltpu.VMEM((K, tn), BF16)],
        compiler_params=_cparams("arbitrary", "arbitrary"),
        name="matmul",
    )(a, w)


def _cols_kernel(c0, w_ref, o_ref):
    n = w_ref.shape[1] - c0
    o_ref[:, :n] = w_ref[:, c0:]
    o_ref[:, n:] = jnp.zeros((o_ref.shape[0], o_ref.shape[1] - n), o_ref.dtype)


def trailing_columns(w, layer, c0):
    _, K, C = w.shape
    n_pad = -(-(C - c0) // LANES) * LANES
    tk = min(256, K)
    return pl.pallas_call(
        functools.partial(_cols_kernel, c0),
        grid=(K // tk,),
        in_specs=[pl.BlockSpec((None, tk, C), lambda i: (layer, i, 0))],
        out_specs=pl.BlockSpec((tk, n_pad), lambda i: (i, 0)),
        out_shape=jax.ShapeDtypeStruct((K, n_pad), w.dtype),
        compiler_params=_cparams("arbitrary"),
        name="trailing_columns",
    )(w)


def _conv_kernel(a_ref, g_ref, ah_ref, gh_ref, dw_ref, dwb_ref, lg_ref, lb_ref, o_ref, u_scr, y_scr):
    ts = a_ref.shape[0]
    nchunk = CONV_CH // LANES
    first = pl.program_id(1) == 0
    uh = ah_ref[...].astype(F32) * jax.nn.sigmoid(gh_ref[...].astype(F32))
    uh = jnp.where(first, 0.0, uh)
    u = a_ref[...].astype(F32) * jax.nn.sigmoid(g_ref[...].astype(F32))
    for c in range(nchunk):
        sl = slice(c * LANES, (c + 1) * LANES)
        u_scr[c, 0:CONV_HALO, :] = uh[:, sl]
        u_scr[c, CONV_HALO:CONV_HALO + ts, :] = u[:, sl]

    off = CONV_HALO - (CONV_WIDTH - 1)

    def chunk_body(c, carry):
        w = dw_ref[c]
        bias = dwb_ref[c]
        for r in range(ts // CONV_SUB):
            acc = jnp.broadcast_to(bias, (CONV_SUB, LANES))
            for j in range(CONV_WIDTH):
                acc = acc + w[j:j + 1, :] * u_scr[c, pl.ds(off + j + r * CONV_SUB, CONV_SUB), :]
            y_scr[c, r * CONV_SUB:(r + 1) * CONV_SUB, :] = acc
        return carry

    lax.fori_loop(0, nchunk, chunk_body, 0)

    s1 = jnp.zeros((ts, 1), F32)
    for c in range(nchunk):
        s1 = s1 + jnp.sum(y_scr[c], axis=-1, keepdims=True)
    mu = s1 * (1.0 / CONV_CH)
    s2 = jnp.zeros((ts, 1), F32)
    for c in range(nchunk):
        d = y_scr[c] - mu
        s2 = s2 + jnp.sum(d * d, axis=-1, keepdims=True)
    rstd = lax.rsqrt(s2 * (1.0 / CONV_CH) + LN_EPS)
    for c in range(nchunk):
        sl = slice(c * LANES, (c + 1) * LANES)
        z = (y_scr[c] - mu) * rstd * lg_ref[:, sl] + lb_ref[:, sl]
        o_ref[:, sl] = (z * jax.nn.sigmoid(z)).astype(o_ref.dtype)


def conv_mixer(proj, dw, dw_b, ln_g, ln_b):
    B, S, _ = proj.shape
    ts = min(CONV_TS, S)
    nchunk = CONV_CH // LANES
    hb = ts // CONV_HALO
    dw_p = jnp.zeros((CONV_HALO, CONV_CH), F32).at[:CONV_WIDTH].set(dw.astype(F32))
    dw_c = dw_p.reshape(CONV_HALO, nchunk, LANES).transpose(1, 0, 2)
    dwb_c = dw_b.astype(F32).reshape(nchunk, 1, LANES)
    halo_idx = lambda b, i: (b, jnp.maximum(i * hb - 1, 0), 0)
    halo_idx_g = lambda b, i: (b, jnp.maximum(i * hb - 1, 0), 1)
    return pl.pallas_call(
        _conv_kernel,
        grid=(B, S // ts),
        in_specs=[
            pl.BlockSpec((None, ts, CONV_CH), lambda b, i: (b, i, 0)),
            pl.BlockSpec((None, ts, CONV_CH), lambda b, i: (b, i, 1)),
            pl.BlockSpec((None, CONV_HALO, CONV_CH), halo_idx),
            pl.BlockSpec((None, CONV_HALO, CONV_CH), halo_idx_g),
            pl.BlockSpec((nchunk, CONV_HALO, LANES), lambda b, i: (0, 0, 0)),
            pl.BlockSpec((nchunk, 1, LANES), lambda b, i: (0, 0, 0)),
            pl.BlockSpec((1, CONV_CH), lambda b, i: (0, 0)),
            pl.BlockSpec((1, CONV_CH), lambda b, i: (0, 0)),
        ],
        out_specs=pl.BlockSpec((None, ts, CONV_CH), lambda b, i: (b, i, 0)),
        out_shape=jax.ShapeDtypeStruct((B, S, CONV_CH), BF16),
        scratch_shapes=[pltpu.VMEM((nchunk, CONV_HALO + ts, LANES), F32),
                        pltpu.VMEM((nchunk, ts, LANES), F32)],
        compiler_params=_cparams("arbitrary", "arbitrary"),
        name="conv_mixer",
    )(proj, proj, proj, proj, dw_c, dwb_c, ln_g.astype(F32).reshape(1, -1), ln_b.astype(F32).reshape(1, -1))


def _fgate_kernel(h_ref, w_ref, b_ref, c_ref, carry_ref):
    ts = h_ref.shape[0]

    @pl.when(pl.program_id(1) == 0)
    def _():
        carry_ref[...] = jnp.zeros_like(carry_ref)

    f = jnp.dot(h_ref[...], w_ref[...], preferred_element_type=F32) + b_ref[...]
    ls = jax.nn.log_sigmoid(f)
    row = lax.broadcasted_iota(jnp.int32, (ts, ts), 0)
    col = lax.broadcasted_iota(jnp.int32, (ts, ts), 1)
    tri = jnp.where(row >= col, 1.0, 0.0).astype(BF16)
    hi = ls.astype(BF16)
    r1 = ls - hi.astype(F32)
    mid = r1.astype(BF16)
    lo = (r1 - mid.astype(F32)).astype(BF16)
    cs = (jnp.dot(tri, hi, preferred_element_type=F32)
          + jnp.dot(tri, mid, preferred_element_type=F32)
          + jnp.dot(tri, lo, preferred_element_type=F32))
    c = cs + carry_ref[...]
    c_ref[...] = c
    carry_ref[...] = c[ts - 1:ts, :]


def forget_cumlog(h_bf, w_f, b_f):
    B, S, D = h_bf.shape
    H = w_f.shape[1]
    ts = min(256, S)
    w_p = jnp.zeros((D, LANES), BF16).at[:, :H].set(w_f.astype(BF16))
    b_p = jnp.zeros((1, LANES), F32).at[0, :H].set(b_f.astype(F32))
    return pl.pallas_call(
        _fgate_kernel,
        grid=(B, S // ts),
        in_specs=[pl.BlockSpec((None, ts, D), lambda b, i: (b, i, 0)),
                  pl.BlockSpec((D, LANES), lambda b, i: (0, 0)),
                  pl.BlockSpec((1, LANES), lambda b, i: (0, 0))],
        out_specs=pl.BlockSpec((None, ts, LANES), lambda b, i: (b, i, 0)),
        out_shape=jax.ShapeDtypeStruct((B, S, LANES), F32),
        scratch_shapes=[pltpu.VMEM((1, LANES), F32)],
        compiler_params=_cparams("arbitrary", "arbitrary"),
        name="forget_cumlog",
    )(h_bf, w_p, b_p)


def _fox_kernel(q_ref, k_ref, v_ref, cq_ref, ck_ref, o_ref):
    tq = q_ref.shape[0]
    tk = ck_ref.shape[-1]
    i = pl.program_id(2)
    qscale = HEAD_DIM ** -0.5 * LOG2E
    heads = range(FOX_HP)
    hs = [slice(h * HEAD_DIM, (h + 1) * HEAD_DIM) for h in heads]
    qs = [(q_ref[:, hs[h]].astype(F32) * qscale).astype(BF16) for h in heads]
    cqs = [jnp.concatenate([jnp.broadcast_to(cq_ref[h, r], (LANES, LANES)).T[:, 0:1] for r in range(tq // LANES)],
                           axis=0) for h in heads]

    def step(j, carry, masked):
        start = pl.multiple_of(j * tk, tk)
        ss = []
        for h in heads:
            k = k_ref[pl.ds(start, tk), hs[h]]
            ss.append(lax.dot_general(qs[h], k, (((1,), (1,)), ((), ())), preferred_element_type=F32))
        if masked:
            row = i * tq + lax.broadcasted_iota(jnp.int32, (tq, tk), 0)
            col = j * tk + lax.broadcasted_iota(jnp.int32, (tq, tk), 1)
            causal = row >= col
        ps, ms, alphas = [], [], []
        for h in heads:
            m = carry[h][0]
            s = ss[h] - ck_ref[h, j]
            if masked:
                s = jnp.where(causal, s, -jnp.inf)
            m_new = jnp.maximum(m, jnp.max(s, axis=-1, keepdims=True) + cqs[h])
            p = jnp.exp2(s - (m_new - cqs[h]))
            alpha = jnp.exp2(m - m_new)
            ps.append(p.astype(BF16))
            ms.append(m_new)
            alphas.append(alpha)
        out = []
        for h in heads:
            v1 = jnp.concatenate([v_ref[pl.ds(start, tk), hs[h]], ones], axis=-1)
            acc = alphas[h] * carry[h][1] + jnp.dot(ps[h], v1, preferred_element_type=F32)
            out.append((ms[h], acc))
        return tuple(out)

    ones = jnp.ones((tk, HEAD_DIM), BF16)
    init = tuple((jnp.full((tq, 1), -jnp.inf, F32), jnp.zeros((tq, 2 * HEAD_DIM), F32)) for _ in heads)
    n_full = (i * tq) // tk
    carry = lax.fori_loop(0, n_full, lambda j, c: step(j, c, False), init)
    for d in range(-(-tq // tk)):
        carry = step(n_full + d, carry, True)
    for h in heads:
        acc = carry[h][1]
        o_ref[:, hs[h]] = (acc[:, :HEAD_DIM] / acc[:, HEAD_DIM:]).astype(o_ref.dtype)


def fox_attention(proj, c):
    B, S, _ = proj.shape
    H = N_FOX_HEADS
    tq, tk = min(FOX_TQ, S), min(FOX_TK, S)
    nq, nk = S // tq, S // tk
    ng = H // FOX_HP
    w = FOX_HP * HEAD_DIM
    c_h = jnp.transpose(c[:, :, :H], (0, 2, 1)) * LOG2E
    c_q = c_h.reshape(B, H, S // LANES, 1, LANES)
    c_row = c_h.reshape(B, H, nk, 1, tk)
    return pl.pallas_call(
        _fox_kernel,
        grid=(B, ng, nq),
        in_specs=[
            pl.BlockSpec((None, tq, w), lambda b, g, i: (b, i, g)),
            pl.BlockSpec((None, S, w), lambda b, g, i: (b, 0, ng + g)),
            pl.BlockSpec((None, S, w), lambda b, g, i: (b, 0, 2 * ng + g)),
            pl.BlockSpec((None, FOX_HP, tq // LANES, 1, LANES), lambda b, g, i: (b, g, i, 0, 0)),
            pl.BlockSpec((None, FOX_HP, nk, 1, tk), lambda b, g, i: (b, g, 0, 0, 0)),
        ],
        out_specs=pl.BlockSpec((None, tq, w), lambda b, g, i: (b, i, g)),
        out_shape=jax.ShapeDtypeStruct((B, S, PRIMARY_WIDTH), BF16),
        compiler_params=_cparams("arbitrary", "arbitrary", "arbitrary"),
        name="fox_attention",
    )(proj, proj, proj, c_q, c_row)


def _mem_attn_kernel(q_ref, kv_ref, o_ref):
    scale = HEAD_DIM ** -0.5
    for h in range(N_MEM_HEADS):
        sl = slice(h * HEAD_DIM, (h + 1) * HEAD_DIM)
        slv = slice(MEM_WIDTH + h * HEAD_DIM, MEM_WIDTH + (h + 1) * HEAD_DIM)
        s = lax.dot_general(q_ref[:, sl], kv_ref[:, sl], (((1,), (1,)), ((), ())),
                            preferred_element_type=F32) * scale
        m = jnp.max(s, axis=-1, keepdims=True)
        p = jnp.exp(s - m)
        l = jnp.sum(p, axis=-1, keepdims=True)
        o = jnp.dot(p.astype(BF16), kv_ref[:, slv], preferred_element_type=F32)
        o_ref[:, sl] = (o / l).astype(o_ref.dtype)


def memory_attention(proj, q_block, kv):
    B, S, _ = proj.shape
    M = kv.shape[1]
    tq = min(MEM_TQ, S)
    return pl.pallas_call(
        _mem_attn_kernel,
        grid=(B, S // tq),
        in_specs=[pl.BlockSpec((None, tq, MEM_WIDTH), lambda b, i: (b, i, q_block)),
                  pl.BlockSpec((None, M, 2 * MEM_WIDTH), lambda b, i: (b, 0, 0))],
        out_specs=pl.BlockSpec((None, tq, MEM_WIDTH), lambda b, i: (b, i, 0)),
        out_shape=jax.ShapeDtypeStruct((B, S, MEM_WIDTH), BF16),
        compiler_params=_cparams("arbitrary", "arbitrary"),
        name="memory_attention",
    )(proj, kv)


def _outproj_ln_kernel(p_ref, m_ref, wt_ref, wb_ref, h_ref, g_ref, b_ref, of_ref, ob_ref, wt_bf, wb_bf):
    @pl.when(pl.program_id(0) == 0)
    def _():
        wt_bf[...] = wt_ref[...].astype(BF16)
        wb_bf[...] = wb_ref[...].astype(BF16)

    mix = (jnp.dot(p_ref[...], wt_bf[...], preferred_element_type=F32)
           + jnp.dot(m_ref[...], wb_bf[...], preferred_element_type=F32))
    y = _ln_rows(DEEPNORM_ALPHA * h_ref[...] + mix, g_ref[...], b_ref[...])
    of_ref[...] = y
    ob_ref[...] = y.astype(BF16)


def outproj_ln(prim, memo, w_out, layer, h, g, b):
    N, D = h.shape
    tm = min(ROW_TILE, N)
    row = lambda i: (i, 0)
    const = lambda i: (0, 0)
    once = pl.Buffered(1)
    return pl.pallas_call(
        _outproj_ln_kernel,
        grid=(N // tm,),
        in_specs=[pl.BlockSpec((tm, PRIMARY_WIDTH), row), pl.BlockSpec((tm, MEM_WIDTH), row),
                  pl.BlockSpec((None, PRIMARY_WIDTH, D), lambda i: (layer, 0, 0), pipeline_mode=once),
                  pl.BlockSpec((None, MEM_WIDTH, D), lambda i: (layer, PRIMARY_WIDTH // MEM_WIDTH, 0),
                               pipeline_mode=once),
                  pl.BlockSpec((tm, D), row), pl.BlockSpec((1, D), const), pl.BlockSpec((1, D), const)],
        out_specs=[pl.BlockSpec((tm, D), row), pl.BlockSpec((tm, D), row)],
        out_shape=[jax.ShapeDtypeStruct((N, D), F32), jax.ShapeDtypeStruct((N, D), BF16)],
        scratch_shapes=[pltpu.VMEM((PRIMARY_WIDTH, D), BF16), pltpu.VMEM((MEM_WIDTH, D), BF16)],
        compiler_params=_cparams("arbitrary"),
        name="outproj_ln",
    )(prim, memo, w_out, w_out, h, g.astype(F32).reshape(1, D), b.astype(F32).reshape(1, D))


def _route_kernel(h_ref, w_ref, b_ref, idx_ref, rank_ref, gate_ref, cnt_ref, carry_ref):
    T = h_ref.shape[0]
    E = w_ref.shape[0]
    gsz = E // N_GROUPS
    neg = -jnp.inf

    @pl.when(pl.program_id(0) == 0)
    def _():
        carry_ref[...] = jnp.zeros_like(carry_ref)

    logits = lax.dot_general(w_ref[...], h_ref[...], (((1,), (1,)), ((), ())),
                             precision=lax.Precision.HIGHEST, preferred_element_type=F32)
    scores = jax.nn.sigmoid(logits)
    choice = scores + b_ref[...]

    io8 = lax.broadcasted_iota(jnp.int32, (gsz, T), 0)
    rows = []
    for g in range(N_GROUPS):
        blk = choice[g * gsz:(g + 1) * gsz, :]
        m1 = jnp.max(blk, axis=0, keepdims=True)
        f1 = jnp.min(jnp.where(blk == m1, io8, gsz), axis=0, keepdims=True)
        m2 = jnp.max(jnp.where(io8 == f1, neg, blk), axis=0, keepdims=True)
        rows.append(jnp.broadcast_to(m1 + m2, (gsz, T)))
    gs = jnp.concatenate(rows, axis=0)
    eid = lax.broadcasted_iota(jnp.int32, (E, T), 0)
    gid = eid // gsz
    emask = jnp.zeros((E, T), jnp.bool_)
    for _ in range(TOPK_GROUPS):
        m = jnp.max(gs, axis=0, keepdims=True)
        g = jnp.min(jnp.where(gs == m, gid, N_GROUPS), axis=0, keepdims=True)
        hit = gid == g
        emask = jnp.logical_or(emask, hit)
        gs = jnp.where(hit, neg, gs)

    masked = jnp.where(emask, choice, neg)
    sel = jnp.zeros((E, T), jnp.bool_)
    ids, ws, hits = [], [], []
    for _ in range(TOP_K):
        m = jnp.max(masked, axis=0, keepdims=True)
        i_k = jnp.min(jnp.where(masked == m, eid, E), axis=0, keepdims=True)
        hit = eid == i_k
        ids.append(i_k)
        ws.append(jnp.sum(jnp.where(hit, scores, 0.0), axis=0, keepdims=True))
        hits.append(hit)
        masked = jnp.where(hit, neg, masked)
        sel = jnp.logical_or(sel, hit)
    wsum = ws[0]
    for w in ws[1:]:
        wsum = wsum + w

    sel_f = jnp.where(sel, 1.0, 0.0)
    r = lax.broadcasted_iota(jnp.int32, (T, T), 0)
    c = lax.broadcasted_iota(jnp.int32, (T, T), 1)
    tri = jnp.where(r < c, 1.0, 0.0).astype(BF16)
    before = jnp.dot(sel_f.astype(BF16), tri, preferred_element_type=F32) + carry_ref[...]
    for k in range(TOP_K):
        idx_ref[k:k + 1, :] = ids[k]
        rank_ref[k:k + 1, :] = jnp.sum(jnp.where(hits[k], before, 0.0), axis=0, keepdims=True).astype(jnp.int32)
        gate_ref[k:k + 1, :] = ws[k] / wsum * ROUTED_SCALE
    total = carry_ref[...] + jnp.sum(sel_f, axis=1, keepdims=True)
    carry_ref[...] = total
    cnt_ref[...] = jnp.broadcast_to(total, cnt_ref.shape).astype(jnp.int32)


def route(h, w_r, r_bias):
    N, D = h.shape
    E = w_r.shape[1]
    T = min(ROUTE_T, N)
    kn = lambda i: (0, i)
    idx, rank, gate, cnt = pl.pallas_call(
        _route_kernel,
        grid=(N // T,),
        in_specs=[pl.BlockSpec((T, D), lambda i: (i, 0)), pl.BlockSpec((E, D), lambda i: (0, 0)),
                  pl.BlockSpec((E, 1), lambda i: (0, 0))],
        out_specs=[pl.BlockSpec((TOP_K, T), kn), pl.BlockSpec((TOP_K, T), kn), pl.BlockSpec((TOP_K, T), kn),
                   pl.BlockSpec((E, LANES), lambda i: (0, 0))],
        out_shape=[jax.ShapeDtypeStruct((TOP_K, N), jnp.int32), jax.ShapeDtypeStruct((TOP_K, N), jnp.int32),
                   jax.ShapeDtypeStruct((TOP_K, N), F32), jax.ShapeDtypeStruct((E, LANES), jnp.int32)],
        scratch_shapes=[pltpu.VMEM((E, 1), F32)],
        compiler_params=_cparams("arbitrary"),
        name="route",
    )(h, w_r.astype(F32).T, r_bias.astype(F32).reshape(E, 1))
    return idx, rank, gate, cnt[:, 0]


def _pos_kernel(starts_ref, idx_ref, rank_ref, pos_ref):
    idx = idx_ref[...]
    acc = rank_ref[...]
    for e in range(N_EXPERTS):
        acc = acc + jnp.where(idx == e, starts_ref[e], 0)
    pos_ref[...] = acc


def slot_positions(starts, idx, rank):
    K, N = idx.shape
    T = min(2048, N)
    grid_spec = pltpu.PrefetchScalarGridSpec(
        num_scalar_prefetch=1, grid=(N // T,),
        in_specs=[pl.BlockSpec((K, T), lambda i, s: (0, i)), pl.BlockSpec((K, T), lambda i, s: (0, i))],
        out_specs=pl.BlockSpec((K, T), lambda i, s: (0, i)))
    return pl.pallas_call(
        _pos_kernel, grid_spec=grid_spec, out_shape=jax.ShapeDtypeStruct((K, N), jnp.int32),
        compiler_params=_cparams("arbitrary"), name="slot_positions",
    )(starts.astype(jnp.int32), idx, rank)


def _pack_pair(a, b):
    ua = lax.bitcast_convert_type(a.astype(BF16).astype(F32), jnp.uint32)
    ub = lax.bitcast_convert_type(b.astype(BF16).astype(F32), jnp.uint32)
    return ub | (ua >> 16)


def _unpack_pair(w):
    lo = lax.bitcast_convert_type(w << 16, F32)
    hi = lax.bitcast_convert_type(w & jnp.uint32(0xFFFF0000), F32)
    return lo, hi


def _rows_to_slab(x):
    return [_pack_pair(x[:, j * LANES:(j + 1) * LANES], x[:, (j + SLAB_ROWS) * LANES:(j + SLAB_ROWS + 1) * LANES])
            for j in range(SLAB_ROWS)]


def _slab_to_rows(pieces):
    halves = [_unpack_pair(w) for w in pieces]
    return jnp.concatenate([h[0] for h in halves] + [h[1] for h in halves], axis=-1)


def _tiles(a, T):
    K, N = a.shape
    return a.reshape(K, N // T, T).transpose(1, 0, 2)


def _swiglu_bf(x, wu, wd):
    f = wd.shape[0]
    h = jnp.dot(x, wu, preferred_element_type=F32)
    a = h[:, :f]
    act = (a * jax.nn.sigmoid(a)) * h[:, f:]
    return jnp.dot(act.astype(BF16), wd, preferred_element_type=F32)


def _dispatch_kernel(pos_ref, hb_ref, su_ref, sd_ref, sh_ref, xs_hbm, slab, su_bf, sd_bf, sem):
    T = hb_ref.shape[0]

    @pl.when(pl.program_id(0) == 0)
    def _():
        su_bf[...] = su_ref[...].astype(BF16)
        sd_bf[...] = sd_ref[...].astype(BF16)

    hb = hb_ref[...]
    for j, piece in enumerate(_rows_to_slab(hb.astype(F32))):
        slab[pl.ds(j, T, stride=SLAB_ROWS), :] = piece

    for t in range(T):
        src = slab.at[pl.ds(t * SLAB_ROWS, SLAB_ROWS)]
        for k in range(TOP_K):
            dst = xs_hbm.at[pl.ds(pl.multiple_of(pos_ref[k, t] * SLAB_ROWS, SLAB_ROWS), SLAB_ROWS)]
            pltpu.make_async_copy(src, dst, sem).start(priority=k % 2)
    sh_ref[...] = _swiglu_bf(hb, su_bf[...], sd_bf[...]).astype(sh_ref.dtype)
    for k in range(TOP_K):
        pltpu.make_async_copy(slab, xs_hbm.at[pl.ds(0, T * SLAB_ROWS)], sem).wait()


def dispatch_shared(pos_t, h_bf, sh_up, sh_down, layer):
    N, D = h_bf.shape
    nt, K, T = pos_t.shape
    F2 = sh_up.shape[-1]
    Fd = sh_down.shape[1]
    return pl.pallas_call(
        _dispatch_kernel,
        grid=(nt,),
        in_specs=[pl.BlockSpec((None, K, T), lambda i: (i, 0, 0), memory_space=pltpu.SMEM),
                  pl.BlockSpec((T, D), lambda i: (i, 0)),
                  pl.BlockSpec((None, D, F2), lambda i: (layer, 0, 0)),
                  pl.BlockSpec((None, Fd, D), lambda i: (layer, 0, 0))],
        out_specs=[pl.BlockSpec((T, D), lambda i: (i, 0)), pl.BlockSpec(memory_space=pl.ANY)],
        out_shape=[jax.ShapeDtypeStruct((N, D), BF16),
                   jax.ShapeDtypeStruct((N * K * SLAB_ROWS, LANES), jnp.uint32)],
        scratch_shapes=[pltpu.VMEM((T * SLAB_ROWS, LANES), jnp.uint32), pltpu.VMEM((D, F2), BF16),
                        pltpu.VMEM((Fd, D), BF16), pltpu.SemaphoreType.DMA(())],
        compiler_params=_cparams("arbitrary"),
        name="dispatch_shared",
    )(pos_t, h_bf, sh_up, sh_down)


def _expert_kernel(layer, tile_ref, be_ref, lo_ref, hi_ref, first_ref, slot_ref, next_ref, x_ref, wu_hbm, wd_hbm,
                   o_ref, wu_f32, wd_f32, wu_bf, wd_bf, sem):
    v = pl.program_id(0)
    tm = x_ref.shape[0] // SLAB_ROWS
    lo = lo_ref[v]
    hi = hi_ref[v]

    def weight_copies(e, slot):
        return (pltpu.make_async_copy(wu_hbm.at[layer, e], wu_f32.at[slot], sem.at[0, slot]),
                pltpu.make_async_copy(wd_hbm.at[layer, e], wd_f32.at[slot], sem.at[1, slot]))

    @pl.when(hi > lo)
    def _():
        changed = jnp.logical_or(v == 0, be_ref[v] != be_ref[jnp.maximum(v - 1, 0)])

        @pl.when(changed)
        def _():
            slot = slot_ref[v]

            @pl.when(v == 0)
            def _():
                for c in weight_copies(be_ref[v], slot):
                    c.start()

            for c in weight_copies(be_ref[v], slot):
                c.wait()

            @pl.when(next_ref[v] >= 0)
            def _():
                for c in weight_copies(next_ref[v], 1 - slot):
                    c.start()

            wu_bf[...] = wu_f32[slot].astype(BF16)
            wd_bf[...] = wd_f32[slot].astype(BF16)

        x = _slab_to_rows([x_ref[pl.ds(j, tm, stride=SLAB_ROWS), :] for j in range(SLAB_ROWS)])
        y = _swiglu_bf(x.astype(BF16), wu_bf[...], wd_bf[...])
        row = lax.broadcasted_iota(jnp.int32, (tm, 1), 0)
        mine = jnp.logical_and(row >= lo, row < hi)
        pieces = _rows_to_slab(y)

        @pl.when(first_ref[v] == 1)
        def _():
            for j in range(SLAB_ROWS):
                o_ref[pl.ds(j, tm, stride=SLAB_ROWS), :] = jnp.where(mine, pieces[j], jnp.uint32(0))

        @pl.when(first_ref[v] == 0)
        def _():
            for j in range(SLAB_ROWS):
                keep = o_ref[pl.ds(j, tm, stride=SLAB_ROWS), :]
                o_ref[pl.ds(j, tm, stride=SLAB_ROWS), :] = jnp.where(mine, pieces[j], keep)


def _lookup(table, idx):
    n = table.shape[0]
    return jnp.sum(jnp.where(idx[:, None] == jnp.arange(n)[None, :], table[None, :], 0), axis=1)


def _visit_tables(counts, tm, A):
    E = counts.shape[0]
    V = A // tm + E
    starts = jnp.cumsum(counts) - counts
    ends = starts + counts
    first_tile = starts // tm
    nvis = jnp.where(counts > 0, (ends - 1) // tm - first_tile + 1, 0)
    vend = jnp.cumsum(nvis)
    voff = vend - nvis
    total = vend[-1]
    v = jnp.arange(V, dtype=jnp.int32)
    e_v = jnp.minimum(jnp.sum(vend[None, :] <= v[:, None], axis=1), E - 1).astype(jnp.int32)
    valid = v < total
    e_last = jnp.sum(jnp.where(v == total - 1, e_v, 0))
    e_v = jnp.where(valid, e_v, e_last)
    tile_v = jnp.where(valid, _lookup(first_tile, e_v) + v - _lookup(voff, e_v), A // tm - 1)
    base = tile_v * tm
    lo_v = jnp.where(valid, jnp.maximum(_lookup(starts, e_v), base) - base, 0)
    hi_v = jnp.where(valid, jnp.minimum(_lookup(ends, e_v), base + tm) - base, 0)
    prev_tile = jnp.concatenate([jnp.full((1,), -1, tile_v.dtype), tile_v[:-1]])
    first_v = jnp.logical_and(valid, tile_v != prev_tile)
    has = nvis > 0
    ids = jnp.arange(E)
    slot_e = (jnp.cumsum(has) - 1) & 1
    later = jnp.logical_and(ids[None, :] > ids[:, None], has[None, :])
    next_e = jnp.min(jnp.where(later, ids[None, :], E), axis=1)
    next_e = jnp.where(next_e == E, -1, next_e)
    i32 = lambda a: a.astype(jnp.int32)
    return starts, (i32(tile_v), i32(e_v), i32(lo_v), i32(hi_v), i32(first_v),
                    i32(_lookup(slot_e, e_v)), i32(_lookup(next_e, e_v)))


def expert_ffn(xs, tables, w_up, w_down, layer):
    V = tables[0].shape[0]
    D, F2 = w_up.shape[-2:]
    Fd = w_down.shape[-2]
    blk = EXP_TM * SLAB_ROWS
    nt = len(tables)
    tile_map = lambda v, *t: (t[0][v], 0)
    grid_spec = pltpu.PrefetchScalarGridSpec(
        num_scalar_prefetch=nt,
        grid=(V,),
        in_specs=[pl.BlockSpec((blk, LANES), tile_map),
                  pl.BlockSpec(memory_space=pl.ANY), pl.BlockSpec(memory_space=pl.ANY)],
        out_specs=pl.BlockSpec((blk, LANES), tile_map),
        scratch_shapes=[pltpu.VMEM((2, D, F2), F32), pltpu.VMEM((2, Fd, D), F32),
                        pltpu.VMEM((D, F2), BF16), pltpu.VMEM((Fd, D), BF16),
                        pltpu.SemaphoreType.DMA((2, 2))],
    )
    return pl.pallas_call(
        functools.partial(_expert_kernel, layer),
        grid_spec=grid_spec,
        out_shape=jax.ShapeDtypeStruct(xs.shape, jnp.uint32),
        compiler_params=_cparams("arbitrary"),
        name="expert_ffn",
    )(*tables, xs, w_up, w_down)


def _combine_ln_kernel(pos_ref, nxt_ref, gate_ref, h_ref, sh_ref, g_ref, b_ref, y_hbm, of_ref, ob_ref,
                       ybuf, acc_lo, acc_hi, gate_rows, sem):
    T = h_ref.shape[0]
    i = pl.program_id(0)
    slot = i % 2

    last = i == pl.num_programs(0) - 1

    def issue_tile(p_ref, s):
        for t in range(T):
            for k in range(TOP_K):
                src0 = pl.multiple_of(p_ref[k, t] * SLAB_ROWS, SLAB_ROWS)
                pltpu.make_async_copy(y_hbm.at[pl.ds(src0, SLAB_ROWS)],
                                      ybuf.at[s, k, pl.ds(t * SLAB_ROWS, SLAB_ROWS)],
                                      sem.at[s]).start(priority=k % 2)

    def wait_tile(s):
        for k in range(TOP_K):
            pltpu.make_async_copy(y_hbm.at[pl.ds(0, T * SLAB_ROWS)], ybuf.at[s, k], sem.at[s]).wait()

    for k in range(TOP_K):
        for c in range(T // LANES):
            gk = gate_ref[k:k + 1, c * LANES:(c + 1) * LANES]
            gate_rows[k, c * LANES:(c + 1) * LANES, :] = jnp.broadcast_to(gk, (LANES, LANES)).T

    @pl.when(i == 0)
    def _():
        issue_tile(pos_ref, slot)

    wait_tile(slot)
    issue_tile(nxt_ref, 1 - slot)
    for t in range(T):
        r0 = t * SLAB_ROWS
        lo, hi = _unpack_pair(ybuf[slot, 0, pl.ds(r0, SLAB_ROWS), :])
        g = gate_rows[0, t:t + 1, :]
        lo, hi = g * lo, g * hi
        for k in range(1, TOP_K):
            l2, h2 = _unpack_pair(ybuf[slot, k, pl.ds(r0, SLAB_ROWS), :])
            g = gate_rows[k, t:t + 1, :]
            lo, hi = lo + g * l2, hi + g * h2
        acc_lo[pl.ds(r0, SLAB_ROWS), :] = lo
        acc_hi[pl.ds(r0, SLAB_ROWS), :] = hi

    @pl.when(last)
    def _():
        wait_tile(1 - slot)

    routed = jnp.concatenate([acc_lo[pl.ds(j, T, stride=SLAB_ROWS), :] for j in range(SLAB_ROWS)]
                             + [acc_hi[pl.ds(j, T, stride=SLAB_ROWS), :] for j in range(SLAB_ROWS)], axis=-1)
    y = DEEPNORM_ALPHA * h_ref[...] + routed + sh_ref[...].astype(F32)
    y = _ln_rows(y, g_ref[...], b_ref[...])
    of_ref[...] = y
    ob_ref[...] = y.astype(BF16)


def combine_ln(pos_t, gate_t, h, shared, y, g, b):
    N, D = h.shape
    nt, K, T = pos_t.shape
    row = lambda i: (i, 0)
    const = lambda i: (0, 0)
    return pl.pallas_call(
        _combine_ln_kernel,
        grid=(nt,),
        in_specs=[pl.BlockSpec((None, K, T), lambda i: (i, 0, 0), memory_space=pltpu.SMEM),
                  pl.BlockSpec((None, K, T), lambda i: (jnp.minimum(i + 1, nt - 1), 0, 0), memory_space=pltpu.SMEM),
                  pl.BlockSpec((None, K, T), lambda i: (i, 0, 0)),
                  pl.BlockSpec((T, D), row), pl.BlockSpec((T, D), row),
                  pl.BlockSpec((1, D), const), pl.BlockSpec((1, D), const),
                  pl.BlockSpec(memory_space=pl.ANY)],
        out_specs=[pl.BlockSpec((T, D), row), pl.BlockSpec((T, D), row)],
        out_shape=[jax.ShapeDtypeStruct((N, D), F32), jax.ShapeDtypeStruct((N, D), BF16)],
        scratch_shapes=[pltpu.VMEM((2, K, T * SLAB_ROWS, LANES), jnp.uint32), pltpu.VMEM((T * SLAB_ROWS, LANES), F32),
                        pltpu.VMEM((T * SLAB_ROWS, LANES), F32), pltpu.VMEM((K, T, LANES), F32),
                        pltpu.SemaphoreType.DMA((2,))],
        compiler_params=_cparams("arbitrary"),
        name="combine_ln",
    )(pos_t, pos_t, gate_t, h, shared, g.astype(F32).reshape(1, D), b.astype(F32).reshape(1, D), y)


def moe_ffn_ln(h, h_bf, layer, router_w, router_bias, exp_w_up, exp_w_down, shared_w_up, shared_w_down, g, b):
    N, D = h.shape
    idx, rank, gate, counts = route(h, router_w[layer], router_bias[layer])
    starts, tables = _visit_tables(counts, EXP_TM, N * TOP_K)
    pos = slot_positions(starts, idx, rank)
    shared, xs = dispatch_shared(_tiles(pos, min(DISPATCH_T, N)), h_bf, shared_w_up, shared_w_down, layer)
    y = expert_ffn(xs, tables, exp_w_up, exp_w_down, layer)
    tc = min(COMBINE_T, N)
    return combine_ln(_tiles(pos, tc), _tiles(gate, tc), h, shared, y, g, b)


def kernel(x, mem, conv_w_in, conv_dw, conv_dw_b, conv_ln_g, conv_ln_b, fox_w_in, fox_b_f, mem_w_kv, w_out,
           ln_g, ln_b, router_w, router_bias, exp_w_up, exp_w_down, shared_w_up, shared_w_down):
    B, S, D = x.shape
    N = B * S
    h = x.reshape(N, D).astype(F32)
    h_bf = h.astype(BF16)
    mem_bf = mem.reshape(B * MEM_LEN, D).astype(BF16)
    nf = 3 * PRIMARY_WIDTH
    for i in range(DEPTH):
        j = i // N_MIXERS
        if i % N_MIXERS == 0:
            proj = matmul(h_bf, conv_w_in, j, 2 * CONV_CH + MEM_WIDTH, BF16).reshape(B, S, -1)
            prim = conv_mixer(proj, conv_dw[j], conv_dw_b[j], conv_ln_g[j], conv_ln_b[j])
            q_src, q_block = proj, 2 * CONV_CH // MEM_WIDTH
        else:
            proj = matmul(h_bf, fox_w_in, j, nf, BF16).reshape(B, S, nf)
            tail = trailing_columns(fox_w_in, j, nf)
            w_qm = tail[None, :, N_FOX_HEADS:N_FOX_HEADS + MEM_WIDTH]
            q_src, q_block = matmul(h_bf, w_qm, 0, MEM_WIDTH, BF16).reshape(B, S, MEM_WIDTH), 0
            c = forget_cumlog(h_bf.reshape(B, S, D), tail[:, :N_FOX_HEADS], fox_b_f[j])
            prim = fox_attention(proj, c)
        kv = matmul(mem_bf, mem_w_kv, i, 2 * MEM_WIDTH, BF16).reshape(B, MEM_LEN, 2 * MEM_WIDTH)
        memo = memory_attention(q_src, q_block, kv)
        h, h_bf = outproj_ln(prim.reshape(N, PRIMARY_WIDTH), memo.reshape(N, MEM_WIDTH), w_out, i, h,
                             ln_g[i, 0], ln_b[i, 0])
        h, h_bf = moe_ffn_ln(h, h_bf, i, router_w, router_bias, exp_w_up, exp_w_down,
                             shared_w_up, shared_w_down, ln_g[i, 1], ln_b[i, 1])
    return h.reshape(B, S, D)
```

```python
import functools

import jax
import jax.numpy as jnp
from jax import lax
from jax.experimental import pallas as pl
from jax.experimental.pallas import tpu as pltpu

F32 = jnp.float32
BF16 = jnp.bfloat16

D_MODEL = 2048
DEPTH = 4
N_MIXERS = 2
MEM_LEN = 256
HEAD_DIM = 128
N_MEM_HEADS = 4
MEM_WIDTH = N_MEM_HEADS * HEAD_DIM
PRIMARY_WIDTH = D_MODEL - MEM_WIDTH
CONV_CH = PRIMARY_WIDTH
CONV_WIDTH = 31
N_FOX_HEADS = PRIMARY_WIDTH // HEAD_DIM
N_EXPERTS = 64
TOP_K = 8
N_GROUPS = 8
TOPK_GROUPS = 4
D_EXPERT = 512
ROUTED_SCALE = 2.5
LN_EPS = 1e-5
DEEPNORM_ALPHA = (2 * DEPTH) ** 0.25
LOG2E = 1.4426950408889634

LANES = 128
VMEM_LIMIT = 56 * 1024 * 1024
MM_TM, MM_TN = 1024, 512
ROW_TILE = 256
CONV_TS = 256
CONV_HALO = 32
CONV_SUB = 64
FOX_TQ = 256
FOX_TK = 512
FOX_HP = 2
MEM_TQ = 512
EXP_TM = 256
SLAB_ROWS = D_MODEL // LANES // 2
ROUTE_T = 256
DISPATCH_T = 256
COMBINE_T = 128


def _cparams(*sem):
    return pltpu.CompilerParams(dimension_semantics=sem, vmem_limit_bytes=VMEM_LIMIT)


def _ln_rows(y, g, b):
    mu = jnp.mean(y, axis=-1, keepdims=True)
    d = y - mu
    var = jnp.mean(d * d, axis=-1, keepdims=True)
    return d * lax.rsqrt(var + LN_EPS) * g + b


def _mm_kernel(a_ref, w_ref, o_ref, w_bf):
    @pl.when(pl.program_id(1) == 0)
    def _():
        w_bf[...] = w_ref[...].astype(BF16)

    o_ref[...] = jnp.dot(a_ref[...], w_bf[...], preferred_element_type=F32).astype(o_ref.dtype)


def matmul(a, w, layer, n_cols, out_dtype, tm=MM_TM, tn=MM_TN):
    M, K = a.shape
    tm, tn = min(tm, M), min(tn, n_cols)
    assert M % tm == 0 and n_cols % tn == 0 and w.shape[1] == K
    return pl.pallas_call(
        _mm_kernel,
        grid=(n_cols // tn, M // tm),
        in_specs=[pl.BlockSpec((tm, K), lambda j, i: (i, 0)),
                  pl.BlockSpec((None, K, tn), lambda j, i: (layer, 0, j))],
        out_specs=pl.BlockSpec((tm, tn), lambda j, i: (i, j)),
        out_shape=jax.ShapeDtypeStruct((M, n_cols), out_dtype),
        scratch_shapes=[pltpu.VMEM((K, tn), BF16)],
        compiler_params=_cparams("arbitrary", "arbitrary"),
        name="matmul",
    )(a, w)


def _cols_kernel(c0, w_ref, o_ref):
    n = w_ref.shape[1] - c0
    o_ref[:, :n] = w_ref[:, c0:]
    o_ref[:, n:] = jnp.zeros((o_ref.shape[0], o_ref.shape[1] - n), o_ref.dtype)


def trailing_columns(w, layer, c0):
    _, K, C = w.shape
    n_pad = -(-(C - c0) // LANES) * LANES
    tk = min(256, K)
    return pl.pallas_call(
        functools.partial(_cols_kernel, c0),
        grid=(K // tk,),
        in_specs=[pl.BlockSpec((None, tk, C), lambda i: (layer, i, 0))],
        out_specs=pl.BlockSpec((tk, n_pad), lambda i: (i, 0)),
        out_shape=jax.ShapeDtypeStruct((K, n_pad), w.dtype),
        compiler_params=_cparams("arbitrary"),
        name="trailing_columns",
    )(w)


def _conv_kernel(a_ref, g_ref, ah_ref, gh_ref, dw_ref, dwb_ref, lg_ref, lb_ref, o_ref, u_scr, y_scr):
    ts = a_ref.shape[0]
    nchunk = CONV_CH // LANES
    first = pl.program_id(1) == 0
    uh = ah_ref[...].astype(F32) * jax.nn.sigmoid(gh_ref[...].astype(F32))
    uh = jnp.where(first, 0.0, uh)
    u = a_ref[...].astype(F32) * jax.nn.sigmoid(g_ref[...].astype(F32))
    for c in range(nchunk):
        sl = slice(c * LANES, (c + 1) * LANES)
        u_scr[c, 0:CONV_HALO, :] = uh[:, sl]
        u_scr[c, CONV_HALO:CONV_HALO + ts, :] = u[:, sl]

    off = CONV_HALO - (CONV_WIDTH - 1)

    def chunk_body(c, carry):
        w = dw_ref[c]
        bias = dwb_ref[c]
        for r in range(ts // CONV_SUB):
            acc = jnp.broadcast_to(bias, (CONV_SUB, LANES))
            for j in range(CONV_WIDTH):
                acc = acc + w[j:j + 1, :] * u_scr[c, pl.ds(off + j + r * CONV_SUB, CONV_SUB), :]
            y_scr[c, r * CONV_SUB:(r + 1) * CONV_SUB, :] = acc
        return carry

    lax.fori_loop(0, nchunk, chunk_body, 0)

    s1 = jnp.zeros((ts, 1), F32)
    for c in range(nchunk):
        s1 = s1 + jnp.sum(y_scr[c], axis=-1, keepdims=True)
    mu = s1 * (1.0 / CONV_CH)
    s2 = jnp.zeros((ts, 1), F32)
    for c in range(nchunk):
        d = y_scr[c] - mu
        s2 = s2 + jnp.sum(d * d, axis=-1, keepdims=True)
    rstd = lax.rsqrt(s2 * (1.0 / CONV_CH) + LN_EPS)
    for c in range(nchunk):
        sl = slice(c * LANES, (c + 1) * LANES)
        z = (y_scr[c] - mu) * rstd * lg_ref[:, sl] + lb_ref[:, sl]
        o_ref[:, sl] = (z * jax.nn.sigmoid(z)).astype(o_ref.dtype)


def conv_mixer(proj, dw, dw_b, ln_g, ln_b):
    B, S, _ = proj.shape
    ts = min(CONV_TS, S)
    nchunk = CONV_CH // LANES
    hb = ts // CONV_HALO
    dw_p = jnp.zeros((CONV_HALO, CONV_CH), F32).at[:CONV_WIDTH].set(dw.astype(F32))
    dw_c = dw_p.reshape(CONV_HALO, nchunk, LANES).transpose(1, 0, 2)
    dwb_c = dw_b.astype(F32).reshape(nchunk, 1, LANES)
    halo_idx = lambda b, i: (b, jnp.maximum(i * hb - 1, 0), 0)
    halo_idx_g = lambda b, i: (b, jnp.maximum(i * hb - 1, 0), 1)
    return pl.pallas_call(
        _conv_kernel,
        grid=(B, S // ts),
        in_specs=[
            pl.BlockSpec((None, ts, CONV_CH), lambda b, i: (b, i, 0)),
            pl.BlockSpec((None, ts, CONV_CH), lambda b, i: (b, i, 1)),
            pl.BlockSpec((None, CONV_HALO, CONV_CH), halo_idx),
            pl.BlockSpec((None, CONV_HALO, CONV_CH), halo_idx_g),
            pl.BlockSpec((nchunk, CONV_HALO, LANES), lambda b, i: (0, 0, 0)),
            pl.BlockSpec((nchunk, 1, LANES), lambda b, i: (0, 0, 0)),
            pl.BlockSpec((1, CONV_CH), lambda b, i: (0, 0)),
            pl.BlockSpec((1, CONV_CH), lambda b, i: (0, 0)),
        ],
        out_specs=pl.BlockSpec((None, ts, CONV_CH), lambda b, i: (b, i, 0)),
        out_shape=jax.ShapeDtypeStruct((B, S, CONV_CH), BF16),
        scratch_shapes=[pltpu.VMEM((nchunk, CONV_HALO + ts, LANES), F32),
                        pltpu.VMEM((nchunk, ts, LANES), F32)],
        compiler_params=_cparams("arbitrary", "arbitrary"),
        name="conv_mixer",
    )(proj, proj, proj, proj, dw_c, dwb_c, ln_g.astype(F32).reshape(1, -1), ln_b.astype(F32).reshape(1, -1))


def _fgate_kernel(h_ref, w_ref, b_ref, c_ref, carry_ref):
    ts = h_ref.shape[0]

    @pl.when(pl.program_id(1) == 0)
    def _():
        carry_ref[...] = jnp.zeros_like(carry_ref)

    f = jnp.dot(h_ref[...], w_ref[...], preferred_element_type=F32) + b_ref[...]
    ls = jax.nn.log_sigmoid(f)
    row = lax.broadcasted_iota(jnp.int32, (ts, ts), 0)
    col = lax.broadcasted_iota(jnp.int32, (ts, ts), 1)
    tri = jnp.where(row >= col, 1.0, 0.0).astype(BF16)
    hi = ls.astype(BF16)
    r1 = ls - hi.astype(F32)
    mid = r1.astype(BF16)
    lo = (r1 - mid.astype(F32)).astype(BF16)
    cs = (jnp.dot(tri, hi, preferred_element_type=F32)
          + jnp.dot(tri, mid, preferred_element_type=F32)
          + jnp.dot(tri, lo, preferred_element_type=F32))
    c = cs + carry_ref[...]
    c_ref[...] = c
    carry_ref[...] = c[ts - 1:ts, :]


def forget_cumlog(h_bf, w_f, b_f):
    B, S, D = h_bf.shape
    H = w_f.shape[1]
    ts = min(256, S)
    w_p = jnp.zeros((D, LANES), BF16).at[:, :H].set(w_f.astype(BF16))
    b_p = jnp.zeros((1, LANES), F32).at[0, :H].set(b_f.astype(F32))
    return pl.pallas_call(
        _fgate_kernel,
        grid=(B, S // ts),
        in_specs=[pl.BlockSpec((None, ts, D), lambda b, i: (b, i, 0)),
                  pl.BlockSpec((D, LANES), lambda b, i: (0, 0)),
                  pl.BlockSpec((1, LANES), lambda b, i: (0, 0))],
        out_specs=pl.BlockSpec((None, ts, LANES), lambda b, i: (b, i, 0)),
        out_shape=jax.ShapeDtypeStruct((B, S, LANES), F32),
        scratch_shapes=[pltpu.VMEM((1, LANES), F32)],
        compiler_params=_cparams("arbitrary", "arbitrary"),
        name="forget_cumlog",
    )(h_bf, w_p, b_p)


def _fox_kernel(q_ref, k_ref, v_ref, cq_ref, ck_ref, o_ref):
    tq = q_ref.shape[0]
    tk = ck_ref.shape[-1]
    i = pl.program_id(2)
    qscale = HEAD_DIM ** -0.5 * LOG2E
    heads = range(FOX_HP)
    hs = [slice(h * HEAD_DIM, (h + 1) * HEAD_DIM) for h in heads]
    qs = [(q_ref[:, hs[h]].astype(F32) * qscale).astype(BF16) for h in heads]
    cqs = [jnp.concatenate([jnp.broadcast_to(cq_ref[h, r], (LANES, LANES)).T[:, 0:1] for r in range(tq // LANES)],
                           axis=0) for h in heads]

    def step(j, carry, masked):
        start = pl.multiple_of(j * tk, tk)
        ss = []
        for h in heads:
            k = k_ref[pl.ds(start, tk), hs[h]]
            ss.append(lax.dot_general(qs[h], k, (((1,), (1,)), ((), ())), preferred_element_type=F32))
        if masked:
            row = i * tq + lax.broadcasted_iota(jnp.int32, (tq, tk), 0)
            col = j * tk + lax.broadcasted_iota(jnp.int32, (tq, tk), 1)
            causal = row >= col
        ps, ms, alphas = [], [], []
        for h in heads:
            m = carry[h][0]
            s = ss[h] - ck_ref[h, j]
            if masked:
                s = jnp.where(causal, s, -jnp.inf)
            m_new = jnp.maximum(m, jnp.max(s, axis=-1, keepdims=True) + cqs[h])
            p = jnp.exp2(s - (m_new - cqs[h]))
            alpha = jnp.exp2(m - m_new)
            ps.append(p.astype(BF16))
            ms.append(m_new)
            alphas.append(alpha)
        out = []
        for h in heads:
            v1 = jnp.concatenate([v_ref[pl.ds(start, tk), hs[h]], ones], axis=-1)
            acc = alphas[h] * carry[h][1] + jnp.dot(ps[h], v1, preferred_element_type=F32)
            out.append((ms[h], acc))
        return tuple(out)

    ones = jnp.ones((tk, HEAD_DIM), BF16)
    init = tuple((jnp.full((tq, 1), -jnp.inf, F32), jnp.zeros((tq, 2 * HEAD_DIM), F32)) for _ in heads)
    n_full = (i * tq) // tk
    carry = lax.fori_loop(0, n_full, lambda j, c: step(j, c, False), init)
    for d in range(-(-tq // tk)):
        carry = step(n_full + d, carry, True)
    for h in heads:
        acc = carry[h][1]
        o_ref[:, hs[h]] = (acc[:, :HEAD_DIM] / acc[:, HEAD_DIM:]).astype(o_ref.dtype)


def fox_attention(proj, c):
    B, S, _ = proj.shape
    H = N_FOX_HEADS
    tq, tk = min(FOX_TQ, S), min(FOX_TK, S)
    nq, nk = S // tq, S // tk
    ng = H // FOX_HP
    w = FOX_HP * HEAD_DIM
    c_h = jnp.transpose(c[:, :, :H], (0, 2, 1)) * LOG2E
    c_q = c_h.reshape(B, H, S // LANES, 1, LANES)
    c_row = c_h.reshape(B, H, nk, 1, tk)
    return pl.pallas_call(
        _fox_kernel,
        grid=(B, ng, nq),
        in_specs=[
            pl.BlockSpec((None, tq, w), lambda b, g, i: (b, i, g)),
            pl.BlockSpec((None, S, w), lambda b, g, i: (b, 0, ng + g)),
            pl.BlockSpec((None, S, w), lambda b, g, i: (b, 0, 2 * ng + g)),
            pl.BlockSpec((None, FOX_HP, tq // LANES, 1, LANES), lambda b, g, i: (b, g, i, 0, 0)),
            pl.BlockSpec((None, FOX_HP, nk, 1, tk), lambda b, g, i: (b, g, 0, 0, 0)),
        ],
        out_specs=pl.BlockSpec((None, tq, w), lambda b, g, i: (b, i, g)),
        out_shape=jax.ShapeDtypeStruct((B, S, PRIMARY_WIDTH), BF16),
        compiler_params=_cparams("arbitrary", "arbitrary", "arbitrary"),
        name="fox_attention",
    )(proj, proj, proj, c_q, c_row)


def _mem_attn_kernel(q_ref, kv_ref, o_ref):
    scale = HEAD_DIM ** -0.5
    for h in range(N_MEM_HEADS):
        sl = slice(h * HEAD_DIM, (h + 1) * HEAD_DIM)
        slv = slice(MEM_WIDTH + h * HEAD_DIM, MEM_WIDTH + (h + 1) * HEAD_DIM)
        s = lax.dot_general(q_ref[:, sl], kv_ref[:, sl], (((1,), (1,)), ((), ())),
                            preferred_element_type=F32) * scale
        m = jnp.max(s, axis=-1, keepdims=True)
        p = jnp.exp(s - m)
        l = jnp.sum(p, axis=-1, keepdims=True)
        o = jnp.dot(p.astype(BF16), kv_ref[:, slv], preferred_element_type=F32)
        o_ref[:, sl] = (o / l).astype(o_ref.dtype)


def memory_attention(proj, q_block, kv):
    B, S, _ = proj.shape
    M = kv.shape[1]
    tq = min(MEM_TQ, S)
    return pl.pallas_call(
        _mem_attn_kernel,
        grid=(B, S // tq),
        in_specs=[pl.BlockSpec((None, tq, MEM_WIDTH), lambda b, i: (b, i, q_block)),
                  pl.BlockSpec((None, M, 2 * MEM_WIDTH), lambda b, i: (b, 0, 0))],
        out_specs=pl.BlockSpec((None, tq, MEM_WIDTH), lambda b, i: (b, i, 0)),
        out_shape=jax.ShapeDtypeStruct((B, S, MEM_WIDTH), BF16),
        compiler_params=_cparams("arbitrary", "arbitrary"),
        name="memory_attention",
    )(proj, kv)


def _outproj_ln_kernel(p_ref, m_ref, wt_ref, wb_ref, h_ref, g_ref, b_ref, of_ref, ob_ref, wt_bf, wb_bf):
    @pl.when(pl.program_id(0) == 0)
    def _():
        wt_bf[...] = wt_ref[...].astype(BF16)
        wb_bf[...] = wb_ref[...].astype(BF16)

    mix = (jnp.dot(p_ref[...], wt_bf[...], preferred_element_type=F32)
           + jnp.dot(m_ref[...], wb_bf[...], preferred_element_type=F32))
    y = _ln_rows(DEEPNORM_ALPHA * h_ref[...] + mix, g_ref[...], b_ref[...])
    of_ref[...] = y
    ob_ref[...] = y.astype(BF16)


def outproj_ln(prim, memo, w_out, layer, h, g, b):
    N, D = h.shape
    tm = min(ROW_TILE, N)
    row = lambda i: (i, 0)
    const = lambda i: (0, 0)
    once = pl.Buffered(1)
    return pl.pallas_call(
        _outproj_ln_kernel,
        grid=(N // tm,),
        in_specs=[pl.BlockSpec((tm, PRIMARY_WIDTH), row), pl.BlockSpec((tm, MEM_WIDTH), row),
                  pl.BlockSpec((None, PRIMARY_WIDTH, D), lambda i: (layer, 0, 0), pipeline_mode=once),
                  pl.BlockSpec((None, MEM_WIDTH, D), lambda i: (layer, PRIMARY_WIDTH // MEM_WIDTH, 0),
                               pipeline_mode=once),
                  pl.BlockSpec((tm, D), row), pl.BlockSpec((1, D), const), pl.BlockSpec((1, D), const)],
        out_specs=[pl.BlockSpec((tm, D), row), pl.BlockSpec((tm, D), row)],
        out_shape=[jax.ShapeDtypeStruct((N, D), F32), jax.ShapeDtypeStruct((N, D), BF16)],
        scratch_shapes=[pltpu.VMEM((PRIMARY_WIDTH, D), BF16), pltpu.VMEM((MEM_WIDTH, D), BF16)],
        compiler_params=_cparams("arbitrary"),
        name="outproj_ln",
    )(prim, memo, w_out, w_out, h, g.astype(F32).reshape(1, D), b.astype(F32).reshape(1, D))


def _route_kernel(h_ref, w_ref, b_ref, idx_ref, rank_ref, gate_ref, cnt_ref, carry_ref):
    T = h_ref.shape[0]
    E = w_ref.shape[0]
    gsz = E // N_GROUPS
    neg = -jnp.inf

    @pl.when(pl.program_id(0) == 0)
    def _():
        carry_ref[...] = jnp.zeros_like(carry_ref)

    def split(a):
        hi = a.astype(BF16)
        return hi, (a - hi.astype(F32)).astype(BF16)

    nt = lambda a, b: lax.dot_general(a, b, (((1,), (1,)), ((), ())), preferred_element_type=F32)
    w_hi, w_lo = split(w_ref[...])
    h_hi, h_lo = split(h_ref[...])
    logits = nt(w_hi, h_hi) + (nt(w_hi, h_lo) + nt(w_lo, h_hi))
    scores = jax.nn.sigmoid(logits)
    choice = scores + b_ref[...]

    io8 = lax.broadcasted_iota(jnp.int32, (gsz, T), 0)
    rows = []
    for g in range(N_GROUPS):
        blk = choice[g * gsz:(g + 1) * gsz, :]
        m1 = jnp.max(blk, axis=0, keepdims=True)
        f1 = jnp.min(jnp.where(blk == m1, io8, gsz), axis=0, keepdims=True)
        m2 = jnp.max(jnp.where(io8 == f1, neg, blk), axis=0, keepdims=True)
        rows.append(jnp.broadcast_to(m1 + m2, (gsz, T)))
    gs = jnp.concatenate(rows, axis=0)
    eid = lax.broadcasted_iota(jnp.int32, (E, T), 0)
    gid = eid // gsz
    emask = jnp.zeros((E, T), jnp.bool_)
    for _ in range(TOPK_GROUPS):
        m = jnp.max(gs, axis=0, keepdims=True)
        g = jnp.min(jnp.where(gs == m, gid, N_GROUPS), axis=0, keepdims=True)
        hit = gid == g
        emask = jnp.logical_or(emask, hit)
        gs = jnp.where(hit, neg, gs)

    masked = jnp.where(emask, choice, neg)
    sel = jnp.zeros((E, T), jnp.bool_)
    ids, ws, hits = [], [], []
    for _ in range(TOP_K):
        m = jnp.max(masked, axis=0, keepdims=True)
        i_k = jnp.min(jnp.where(masked == m, eid, E), axis=0, keepdims=True)
        hit = eid == i_k
        ids.append(i_k)
        ws.append(jnp.sum(jnp.where(hit, scores, 0.0), axis=0, keepdims=True))
        hits.append(hit)
        masked = jnp.where(hit, neg, masked)
        sel = jnp.logical_or(sel, hit)
    wsum = ws[0]
    for w in ws[1:]:
        wsum = wsum + w

    sel_f = jnp.where(sel, 1.0, 0.0)
    r = lax.broadcasted_iota(jnp.int32, (T, T), 0)
    c = lax.broadcasted_iota(jnp.int32, (T, T), 1)
    tri = jnp.where(r < c, 1.0, 0.0).astype(BF16)
    before = jnp.dot(sel_f.astype(BF16), tri, preferred_element_type=F32) + carry_ref[...]
    for k in range(TOP_K):
        idx_ref[k:k + 1, :] = ids[k]
        rank_ref[k:k + 1, :] = jnp.sum(jnp.where(hits[k], before, 0.0), axis=0, keepdims=True).astype(jnp.int32)
        gate_ref[k:k + 1, :] = ws[k] / wsum * ROUTED_SCALE
    total = carry_ref[...] + jnp.sum(sel_f, axis=1, keepdims=True)
    carry_ref[...] = total
    cnt_ref[...] = jnp.broadcast_to(total, cnt_ref.shape).astype(jnp.int32)


def route(h, w_r, r_bias):
    N, D = h.shape
    E = w_r.shape[1]
    T = min(ROUTE_T, N)
    kn = lambda i: (0, i)
    idx, rank, gate, cnt = pl.pallas_call(
        _route_kernel,
        grid=(N // T,),
        in_specs=[pl.BlockSpec((T, D), lambda i: (i, 0)), pl.BlockSpec((E, D), lambda i: (0, 0)),
                  pl.BlockSpec((E, 1), lambda i: (0, 0))],
        out_specs=[pl.BlockSpec((TOP_K, T), kn), pl.BlockSpec((TOP_K, T), kn), pl.BlockSpec((TOP_K, T), kn),
                   pl.BlockSpec((E, LANES), lambda i: (0, 0))],
        out_shape=[jax.ShapeDtypeStruct((TOP_K, N), jnp.int32), jax.ShapeDtypeStruct((TOP_K, N), jnp.int32),
                   jax.ShapeDtypeStruct((TOP_K, N), F32), jax.ShapeDtypeStruct((E, LANES), jnp.int32)],
        scratch_shapes=[pltpu.VMEM((E, 1), F32)],
        compiler_params=_cparams("arbitrary"),
        name="route",
    )(h, w_r.astype(F32).T, r_bias.astype(F32).reshape(E, 1))
    return idx, rank, gate, cnt[:, 0]


def _pos_kernel(starts_ref, idx_ref, rank_ref, pos_ref):
    idx = idx_ref[...]
    acc = rank_ref[...]
    for e in range(N_EXPERTS):
        acc = acc + jnp.where(idx == e, starts_ref[e], 0)
    pos_ref[...] = acc


def slot_positions(starts, idx, rank):
    K, N = idx.shape
    T = min(2048, N)
    grid_spec = pltpu.PrefetchScalarGridSpec(
        num_scalar_prefetch=1, grid=(N // T,),
        in_specs=[pl.BlockSpec((K, T), lambda i, s: (0, i)), pl.BlockSpec((K, T), lambda i, s: (0, i))],
        out_specs=pl.BlockSpec((K, T), lambda i, s: (0, i)))
    return pl.pallas_call(
        _pos_kernel, grid_spec=grid_spec, out_shape=jax.ShapeDtypeStruct((K, N), jnp.int32),
        compiler_params=_cparams("arbitrary"), name="slot_positions",
    )(starts.astype(jnp.int32), idx, rank)


def _pack_pair(a, b):
    ua = lax.bitcast_convert_type(a.astype(BF16).astype(F32), jnp.uint32)
    ub = lax.bitcast_convert_type(b.astype(BF16).astype(F32), jnp.uint32)
    return ub | (ua >> 16)


def _unpack_pair(w):
    lo = lax.bitcast_convert_type(w << 16, F32)
    hi = lax.bitcast_convert_type(w & jnp.uint32(0xFFFF0000), F32)
    return lo, hi


def _rows_to_slab(x):
    return [_pack_pair(x[:, 2 * j * LANES:(2 * j + 1) * LANES], x[:, (2 * j + 1) * LANES:(2 * j + 2) * LANES])
            for j in range(SLAB_ROWS)]


def _slab_to_rows(pieces):
    return jnp.concatenate([half for w in pieces for half in _unpack_pair(w)], axis=-1)


def _tiles(a, T):
    K, N = a.shape
    return a.reshape(K, N // T, T).transpose(1, 0, 2)


def _swiglu_bf(x, wu, wd):
    f = wd.shape[0]
    h = jnp.dot(x, wu, preferred_element_type=F32)
    a = h[:, :f]
    act = (a * jax.nn.sigmoid(a)) * h[:, f:]
    return jnp.dot(act.astype(BF16), wd, preferred_element_type=F32)


def _dispatch_kernel(pos_ref, hb_ref, su_ref, sd_ref, sh_ref, xs_hbm, slab, su_bf, sd_bf, sem):
    T = hb_ref.shape[0]

    @pl.when(pl.program_id(0) == 0)
    def _():
        su_bf[...] = su_ref[...].astype(BF16)
        sd_bf[...] = sd_ref[...].astype(BF16)

    hb = hb_ref[...]
    for j, piece in enumerate(_rows_to_slab(hb.astype(F32))):
        slab[pl.ds(j, T, stride=SLAB_ROWS), :] = piece

    for t in range(T):
        src = slab.at[pl.ds(t * SLAB_ROWS, SLAB_ROWS)]
        for k in range(TOP_K):
            dst = xs_hbm.at[pl.ds(pl.multiple_of(pos_ref[k, t] * SLAB_ROWS, SLAB_ROWS), SLAB_ROWS)]
            pltpu.make_async_copy(src, dst, sem).start(priority=k % 2)
    sh_ref[...] = _swiglu_bf(hb, su_bf[...], sd_bf[...]).astype(sh_ref.dtype)
    for k in range(TOP_K):
        pltpu.make_async_copy(slab, xs_hbm.at[pl.ds(0, T * SLAB_ROWS)], sem).wait()


def dispatch_shared(pos_t, h_bf, sh_up, sh_down, layer):
    N, D = h_bf.shape
    nt, K, T = pos_t.shape
    F2 = sh_up.shape[-1]
    Fd = sh_down.shape[1]
    return pl.pallas_call(
        _dispatch_kernel,
        grid=(nt,),
        in_specs=[pl.BlockSpec((None, K, T), lambda i: (i, 0, 0), memory_space=pltpu.SMEM),
                  pl.BlockSpec((T, D), lambda i: (i, 0)),
                  pl.BlockSpec((None, D, F2), lambda i: (layer, 0, 0)),
                  pl.BlockSpec((None, Fd, D), lambda i: (layer, 0, 0))],
        out_specs=[pl.BlockSpec((T, D), lambda i: (i, 0)), pl.BlockSpec(memory_space=pl.ANY)],
        out_shape=[jax.ShapeDtypeStruct((N, D), BF16),
                   jax.ShapeDtypeStruct((N * K * SLAB_ROWS, LANES), jnp.uint32)],
        scratch_shapes=[pltpu.VMEM((T * SLAB_ROWS, LANES), jnp.uint32), pltpu.VMEM((D, F2), BF16),
                        pltpu.VMEM((Fd, D), BF16), pltpu.SemaphoreType.DMA(())],
        compiler_params=_cparams("arbitrary"),
        name="dispatch_shared",
    )(pos_t, h_bf, sh_up, sh_down)


def _expert_kernel(layer, tile_ref, be_ref, lo_ref, hi_ref, first_ref, slot_ref, next_ref, half_ref, x_ref, wu_hbm, wd_hbm,
                   o_ref, wu_f32, wd_f32, wu_bf, wd_bf, sem):
    v = pl.program_id(0)
    tm = x_ref.shape[0] // SLAB_ROWS
    lo = lo_ref[v]
    hi = hi_ref[v]

    def weight_copies(e, slot):
        return (pltpu.make_async_copy(wu_hbm.at[layer, e], wu_f32.at[slot], sem.at[0, slot]),
                pltpu.make_async_copy(wd_hbm.at[layer, e], wd_f32.at[slot], sem.at[1, slot]))

    @pl.when(hi > lo)
    def _():
        changed = jnp.logical_or(v == 0, be_ref[v] != be_ref[jnp.maximum(v - 1, 0)])

        @pl.when(changed)
        def _():
            slot = slot_ref[v]

            @pl.when(v == 0)
            def _():
                for c in weight_copies(be_ref[v], slot):
                    c.start()

            for c in weight_copies(be_ref[v], slot):
                c.wait()

            @pl.when(next_ref[v] >= 0)
            def _():
                for c in weight_copies(next_ref[v], 1 - slot):
                    c.start()

            wu_bf[...] = wu_f32[slot].astype(BF16)
            wd_bf[...] = wd_f32[slot].astype(BF16)

        def ffn_rows(r0, n):
            base = r0 * SLAB_ROWS
            x = _slab_to_rows([x_ref[pl.ds(base + j, n, stride=SLAB_ROWS), :] for j in range(SLAB_ROWS)])
            y = _swiglu_bf(x.astype(BF16), wu_bf[...], wd_bf[...])
            row = r0 + lax.broadcasted_iota(jnp.int32, (n, 1), 0)
            mine = jnp.logical_and(row >= lo, row < hi)
            pieces = _rows_to_slab(y)

            @pl.when(first_ref[v] == 1)
            def _():
                for j in range(SLAB_ROWS):
                    o_ref[pl.ds(base + j, n, stride=SLAB_ROWS), :] = jnp.where(mine, pieces[j], jnp.uint32(0))

            @pl.when(first_ref[v] == 0)
            def _():
                for j in range(SLAB_ROWS):
                    keep = o_ref[pl.ds(base + j, n, stride=SLAB_ROWS), :]
                    o_ref[pl.ds(base + j, n, stride=SLAB_ROWS), :] = jnp.where(mine, pieces[j], keep)

        half = tm // 2
        mode = half_ref[v]

        @pl.when(mode == 0)
        def _():
            ffn_rows(0, tm)

        @pl.when(mode != 0)
        def _():
            r0 = pl.multiple_of((mode - 1) * half, half)

            @pl.when(first_ref[v] == 1)
            def _():
                other = pl.multiple_of((half - r0) * SLAB_ROWS, half * SLAB_ROWS)
                o_ref[pl.ds(other, half * SLAB_ROWS), :] = jnp.zeros((half * SLAB_ROWS, LANES), jnp.uint32)

            ffn_rows(r0, half)


def _lookup(table, idx):
    n = table.shape[0]
    return jnp.sum(jnp.where(idx[:, None] == jnp.arange(n)[None, :], table[None, :], 0), axis=1)


def _visit_tables(counts, tm, A):
    E = counts.shape[0]
    V = A // tm + E
    starts = jnp.cumsum(counts) - counts
    ends = starts + counts
    first_tile = starts // tm
    nvis = jnp.where(counts > 0, (ends - 1) // tm - first_tile + 1, 0)
    vend = jnp.cumsum(nvis)
    voff = vend - nvis
    total = vend[-1]
    v = jnp.arange(V, dtype=jnp.int32)
    e_v = jnp.minimum(jnp.sum(vend[None, :] <= v[:, None], axis=1), E - 1).astype(jnp.int32)
    valid = v < total
    e_last = jnp.sum(jnp.where(v == total - 1, e_v, 0))
    e_v = jnp.where(valid, e_v, e_last)
    tile_v = jnp.where(valid, _lookup(first_tile, e_v) + v - _lookup(voff, e_v), A // tm - 1)
    base = tile_v * tm
    lo_v = jnp.where(valid, jnp.maximum(_lookup(starts, e_v), base) - base, 0)
    hi_v = jnp.where(valid, jnp.minimum(_lookup(ends, e_v), base + tm) - base, 0)
    prev_tile = jnp.concatenate([jnp.full((1,), -1, tile_v.dtype), tile_v[:-1]])
    first_v = jnp.logical_and(valid, tile_v != prev_tile)
    has = nvis > 0
    ids = jnp.arange(E)
    slot_e = (jnp.cumsum(has) - 1) & 1
    later = jnp.logical_and(ids[None, :] > ids[:, None], has[None, :])
    next_e = jnp.min(jnp.where(later, ids[None, :], E), axis=1)
    next_e = jnp.where(next_e == E, -1, next_e)
    i32 = lambda a: a.astype(jnp.int32)
    half_v = jnp.where(hi_v <= tm // 2, 1, jnp.where(lo_v >= tm // 2, 2, 0))
    return starts, (i32(tile_v), i32(e_v), i32(lo_v), i32(hi_v), i32(first_v),
                    i32(_lookup(slot_e, e_v)), i32(_lookup(next_e, e_v)), i32(half_v))


def expert_ffn(xs, tables, w_up, w_down, layer):
    V = tables[0].shape[0]
    D, F2 = w_up.shape[-2:]
    Fd = w_down.shape[-2]
    blk = EXP_TM * SLAB_ROWS
    nt = len(tables)
    tile_map = lambda v, *t: (t[0][v], 0)
    grid_spec = pltpu.PrefetchScalarGridSpec(
        num_scalar_prefetch=nt,
        grid=(V,),
        in_specs=[pl.BlockSpec((blk, LANES), tile_map),
                  pl.BlockSpec(memory_space=pl.ANY), pl.BlockSpec(memory_space=pl.ANY)],
        out_specs=pl.BlockSpec((blk, LANES), tile_map),
        scratch_shapes=[pltpu.VMEM((2, D, F2), F32), pltpu.VMEM((2, Fd, D), F32),
                        pltpu.VMEM((D, F2), BF16), pltpu.VMEM((Fd, D), BF16),
                        pltpu.SemaphoreType.DMA((2, 2))],
    )
    return pl.pallas_call(
        functools.partial(_expert_kernel, layer),
        grid_spec=grid_spec,
        out_shape=jax.ShapeDtypeStruct(xs.shape, jnp.uint32),
        compiler_params=_cparams("arbitrary"),
        name="expert_ffn",
    )(*tables, xs, w_up, w_down)


def _combine_ln_kernel(pos_ref, nxt_ref, gate_ref, h_ref, sh_ref, g_ref, b_ref, y_hbm, of_ref, ob_ref,
                       ybuf, acc_lo, acc_hi, gate_rows, sem):
    T = h_ref.shape[0]
    i = pl.program_id(0)
    slot = i % 2

    last = i == pl.num_programs(0) - 1

    def issue_tile(p_ref, s):
        for t in range(T):
            for k in range(TOP_K):
                src0 = pl.multiple_of(p_ref[k, t] * SLAB_ROWS, SLAB_ROWS)
                pltpu.make_async_copy(y_hbm.at[pl.ds(src0, SLAB_ROWS)],
                                      ybuf.at[s, k, pl.ds(t * SLAB_ROWS, SLAB_ROWS)],
                                      sem.at[s]).start(priority=k % 2)

    def wait_tile(s):
        for k in range(TOP_K):
            pltpu.make_async_copy(y_hbm.at[pl.ds(0, T * SLAB_ROWS)], ybuf.at[s, k], sem.at[s]).wait()

    for k in range(TOP_K):
        for c in range(T // LANES):
            gk = gate_ref[k:k + 1, c * LANES:(c + 1) * LANES]
            gate_rows[k, c * LANES:(c + 1) * LANES, :] = jnp.broadcast_to(gk, (LANES, LANES)).T

    @pl.when(i == 0)
    def _():
        issue_tile(pos_ref, slot)

    wait_tile(slot)
    issue_tile(nxt_ref, 1 - slot)
    for t in range(T):
        r0 = t * SLAB_ROWS
        lo, hi = _unpack_pair(ybuf[slot, 0, pl.ds(r0, SLAB_ROWS), :])
        g = gate_rows[0, t:t + 1, :]
        lo, hi = g * lo, g * hi
        for k in range(1, TOP_K):
            l2, h2 = _unpack_pair(ybuf[slot, k, pl.ds(r0, SLAB_ROWS), :])
            g = gate_rows[k, t:t + 1, :]
            lo, hi = lo + g * l2, hi + g * h2
        acc_lo[pl.ds(r0, SLAB_ROWS), :] = lo
        acc_hi[pl.ds(r0, SLAB_ROWS), :] = hi

    @pl.when(last)
    def _():
        wait_tile(1 - slot)

    routed = jnp.concatenate([acc[pl.ds(j, T, stride=SLAB_ROWS), :] for j in range(SLAB_ROWS)
                              for acc in (acc_lo, acc_hi)], axis=-1)
    y = DEEPNORM_ALPHA * h_ref[...] + routed + sh_ref[...].astype(F32)
    y = _ln_rows(y, g_ref[...], b_ref[...])
    of_ref[...] = y
    ob_ref[...] = y.astype(BF16)


def combine_ln(pos_t, gate_t, h, shared, y, g, b):
    N, D = h.shape
    nt, K, T = pos_t.shape
    row = lambda i: (i, 0)
    const = lambda i: (0, 0)
    return pl.pallas_call(
        _combine_ln_kernel,
        grid=(nt,),
        in_specs=[pl.BlockSpec((None, K, T), lambda i: (i, 0, 0), memory_space=pltpu.SMEM),
                  pl.BlockSpec((None, K, T), lambda i: (jnp.minimum(i + 1, nt - 1), 0, 0), memory_space=pltpu.SMEM),
                  pl.BlockSpec((None, K, T), lambda i: (i, 0, 0)),
                  pl.BlockSpec((T, D), row), pl.BlockSpec((T, D), row),
                  pl.BlockSpec((1, D), const), pl.BlockSpec((1, D), const),
                  pl.BlockSpec(memory_space=pl.ANY)],
        out_specs=[pl.BlockSpec((T, D), row), pl.BlockSpec((T, D), row)],
        out_shape=[jax.ShapeDtypeStruct((N, D), F32), jax.ShapeDtypeStruct((N, D), BF16)],
        scratch_shapes=[pltpu.VMEM((2, K, T * SLAB_ROWS, LANES), jnp.uint32), pltpu.VMEM((T * SLAB_ROWS, LANES), F32),
                        pltpu.VMEM((T * SLAB_ROWS, LANES), F32), pltpu.VMEM((K, T, LANES), F32),
                        pltpu.SemaphoreType.DMA((2,))],
        compiler_params=_cparams("arbitrary"),
        name="combine_ln",
    )(pos_t, pos_t, gate_t, h, shared, g.astype(F32).reshape(1, D), b.astype(F32).reshape(1, D), y)


def moe_ffn_ln(h, h_bf, layer, router_w, router_bias, exp_w_up, exp_w_down, shared_w_up, shared_w_down, g, b):
    N, D = h.shape
    idx, rank, gate, counts = route(h, router_w[layer], router_bias[layer])
    starts, tables = _visit_tables(counts, EXP_TM, N * TOP_K)
    pos = slot_positions(starts, idx, rank)
    shared, xs = dispatch_shared(_tiles(pos, min(DISPATCH_T, N)), h_bf, shared_w_up, shared_w_down, layer)
    y = expert_ffn(xs, tables, exp_w_up, exp_w_down, layer)
    tc = min(COMBINE_T, N)
    return combine_ln(_tiles(pos, tc), _tiles(gate, tc), h, shared, y, g, b)


def kernel(x, mem, conv_w_in, conv_dw, conv_dw_b, conv_ln_g, conv_ln_b, fox_w_in, fox_b_f, mem_w_kv, w_out,
           ln_g, ln_b, router_w, router_bias, exp_w_up, exp_w_down, shared_w_up, shared_w_down):
    B, S, D = x.shape
    N = B * S
    h = x.reshape(N, D).astype(F32)
    h_bf = h.astype(BF16)
    mem_bf = mem.reshape(B * MEM_LEN, D).astype(BF16)
    nf = 3 * PRIMARY_WIDTH
    for i in range(DEPTH):
        j = i // N_MIXERS
        if i % N_MIXERS == 0:
            proj = matmul(h_bf, conv_w_in, j, 2 * CONV_CH + MEM_WIDTH, BF16).reshape(B, S, -1)
            prim = conv_mixer(proj, conv_dw[j], conv_dw_b[j], conv_ln_g[j], conv_ln_b[j])
            q_src, q_block = proj, 2 * CONV_CH // MEM_WIDTH
        else:
            proj = matmul(h_bf, fox_w_in, j, nf, BF16).reshape(B, S, nf)
            tail = trailing_columns(fox_w_in, j, nf)
            w_qm = tail[None, :, N_FOX_HEADS:N_FOX_HEADS + MEM_WIDTH]
            q_src, q_block = matmul(h_bf, w_qm, 0, MEM_WIDTH, BF16).reshape(B, S, MEM_WIDTH), 0
            c = forget_cumlog(h_bf.reshape(B, S, D), tail[:, :N_FOX_HEADS], fox_b_f[j])
            prim = fox_attention(proj, c)
        kv = matmul(mem_bf, mem_w_kv, i, 2 * MEM_WIDTH, BF16).reshape(B, MEM_LEN, 2 * MEM_WIDTH)
        memo = memory_attention(q_src, q_block, kv)
        h, h_bf = outproj_ln(prim.reshape(N, PRIMARY_WIDTH), memo.reshape(N, MEM_WIDTH), w_out, i, h,
                             ln_g[i, 0], ln_b[i, 0])
        h, h_bf = moe_ffn_ln(h, h_bf, i, router_w, router_bias, exp_w_up, exp_w_down,
                             shared_w_up, shared_w_down, ln_g[i, 1], ln_b[i, 1])
    return h.reshape(B, S, D)
```

```python
import functools

import jax
import jax.numpy as jnp
from jax import lax
from jax.experimental import pallas as pl
from jax.experimental.pallas import tpu as pltpu

F32 = jnp.float32
BF16 = jnp.bfloat16

D_MODEL = 2048
DEPTH = 4
N_MIXERS = 2
MEM_LEN = 256
HEAD_DIM = 128
N_MEM_HEADS = 4
MEM_WIDTH = N_MEM_HEADS * HEAD_DIM
PRIMARY_WIDTH = D_MODEL - MEM_WIDTH
CONV_CH = PRIMARY_WIDTH
CONV_WIDTH = 31
N_FOX_HEADS = PRIMARY_WIDTH // HEAD_DIM
N_EXPERTS = 64
TOP_K = 8
N_GROUPS = 8
TOPK_GROUPS = 4
D_EXPERT = 512
ROUTED_SCALE = 2.5
LN_EPS = 1e-5
DEEPNORM_ALPHA = (2 * DEPTH) ** 0.25
LOG2E = 1.4426950408889634

LANES = 128
VMEM_LIMIT = 56 * 1024 * 1024
MM_TM, MM_TN = 1024, 512
ROW_TILE = 256
CONV_TS = 256
CONV_HALO = 32
CONV_SUB = 64
FOX_TQ = 1024
FOX_TK = 1024
FOX_HP = 2
MEM_TQ = 512
EXP_TM = 256
SLAB_ROWS = D_MODEL // LANES // 2
ROUTE_T = 256
DISPATCH_T = 256
COMBINE_T = 256


def _cparams(*sem):
    return pltpu.CompilerParams(dimension_semantics=sem, vmem_limit_bytes=VMEM_LIMIT)


def _ln_rows(y, g, b):
    mu = jnp.mean(y, axis=-1, keepdims=True)
    d = y - mu
    var = jnp.mean(d * d, axis=-1, keepdims=True)
    return d * lax.rsqrt(var + LN_EPS) * g + b


def _mm_kernel(a_ref, w_ref, o_ref, w_bf):
    @pl.when(pl.program_id(1) == 0)
    def _():
        w_bf[...] = w_ref[...].astype(BF16)

    o_ref[...] = jnp.dot(a_ref[...], w_bf[...], preferred_element_type=F32).astype(o_ref.dtype)


def matmul(a, w, layer, n_cols, out_dtype, tm=MM_TM, tn=MM_TN):
    M, K = a.shape
    tm, tn = min(tm, M), min(tn, n_cols)
    assert M % tm == 0 and n_cols % tn == 0 and w.shape[1] == K
    return pl.pallas_call(
        _mm_kernel,
        grid=(n_cols // tn, M // tm),
        in_specs=[pl.BlockSpec((tm, K), lambda j, i: (i, 0)),
                  pl.BlockSpec((None, K, tn), lambda j, i: (layer, 0, j))],
        out_specs=pl.BlockSpec((tm, tn), lambda j, i: (i, j)),
        out_shape=jax.ShapeDtypeStruct((M, n_cols), out_dtype),
        scratch_shapes=[pltpu.VMEM((K, tn), BF16)],
        compiler_params=_cparams("arbitrary", "arbitrary"),
        name="matmul",
    )(a, w)


def _cols_kernel(c0, w_ref, o_ref):
    n = w_ref.shape[1] - c0
    o_ref[:, :n] = w_ref[:, c0:]
    o_ref[:, n:] = jnp.zeros((o_ref.shape[0], o_ref.shape[1] - n), o_ref.dtype)


def trailing_columns(w, layer, c0):
    _, K, C = w.shape
    n_pad = -(-(C - c0) // LANES) * LANES
    tk = min(256, K)
    return pl.pallas_call(
        functools.partial(_cols_kernel, c0),
        grid=(K // tk,),
        in_specs=[pl.BlockSpec((None, tk, C), lambda i: (layer, i, 0))],
        out_specs=pl.BlockSpec((tk, n_pad), lambda i: (i, 0)),
        out_shape=jax.ShapeDtypeStruct((K, n_pad), w.dtype),
        compiler_params=_cparams("arbitrary"),
        name="trailing_columns",
    )(w)


def _conv_kernel(a_ref, g_ref, ah_ref, gh_ref, dw_ref, dwb_ref, lg_ref, lb_ref, o_ref, u_scr, y_scr):
    ts = a_ref.shape[0]
    nchunk = CONV_CH // LANES
    first = pl.program_id(1) == 0
    uh = ah_ref[...].astype(F32) * jax.nn.sigmoid(gh_ref[...].astype(F32))
    uh = jnp.where(first, 0.0, uh)
    u = a_ref[...].astype(F32) * jax.nn.sigmoid(g_ref[...].astype(F32))
    for c in range(nchunk):
        sl = slice(c * LANES, (c + 1) * LANES)
        u_scr[c, 0:CONV_HALO, :] = uh[:, sl]
        u_scr[c, CONV_HALO:CONV_HALO + ts, :] = u[:, sl]

    off = CONV_HALO - (CONV_WIDTH - 1)

    def chunk_body(c, carry):
        w = dw_ref[c]
        bias = dwb_ref[c]
        for r in range(ts // CONV_SUB):
            acc = jnp.broadcast_to(bias, (CONV_SUB, LANES))
            for j in range(CONV_WIDTH):
                acc = acc + w[j:j + 1, :] * u_scr[c, pl.ds(off + j + r * CONV_SUB, CONV_SUB), :]
            y_scr[c, r * CONV_SUB:(r + 1) * CONV_SUB, :] = acc
        return carry

    lax.fori_loop(0, nchunk, chunk_body, 0)

    s1 = jnp.zeros((ts, 1), F32)
    for c in range(nchunk):
        s1 = s1 + jnp.sum(y_scr[c], axis=-1, keepdims=True)
    mu = s1 * (1.0 / CONV_CH)
    s2 = jnp.zeros((ts, 1), F32)
    for c in range(nchunk):
        d = y_scr[c] - mu
        s2 = s2 + jnp.sum(d * d, axis=-1, keepdims=True)
    rstd = lax.rsqrt(s2 * (1.0 / CONV_CH) + LN_EPS)
    for c in range(nchunk):
        sl = slice(c * LANES, (c + 1) * LANES)
        z = (y_scr[c] - mu) * rstd * lg_ref[:, sl] + lb_ref[:, sl]
        o_ref[:, sl] = (z * jax.nn.sigmoid(z)).astype(o_ref.dtype)


def conv_mixer(proj, dw, dw_b, ln_g, ln_b):
    B, S, _ = proj.shape
    ts = min(CONV_TS, S)
    nchunk = CONV_CH // LANES
    hb = ts // CONV_HALO
    dw_p = jnp.zeros((CONV_HALO, CONV_CH), F32).at[:CONV_WIDTH].set(dw.astype(F32))
    dw_c = dw_p.reshape(CONV_HALO, nchunk, LANES).transpose(1, 0, 2)
    dwb_c = dw_b.astype(F32).reshape(nchunk, 1, LANES)
    halo_idx = lambda b, i: (b, jnp.maximum(i * hb - 1, 0), 0)
    halo_idx_g = lambda b, i: (b, jnp.maximum(i * hb - 1, 0), 1)
    return pl.pallas_call(
        _conv_kernel,
        grid=(B, S // ts),
        in_specs=[
            pl.BlockSpec((None, ts, CONV_CH), lambda b, i: (b, i, 0)),
            pl.BlockSpec((None, ts, CONV_CH), lambda b, i: (b, i, 1)),
            pl.BlockSpec((None, CONV_HALO, CONV_CH), halo_idx),
            pl.BlockSpec((None, CONV_HALO, CONV_CH), halo_idx_g),
            pl.BlockSpec((nchunk, CONV_HALO, LANES), lambda b, i: (0, 0, 0)),
            pl.BlockSpec((nchunk, 1, LANES), lambda b, i: (0, 0, 0)),
            pl.BlockSpec((1, CONV_CH), lambda b, i: (0, 0)),
            pl.BlockSpec((1, CONV_CH), lambda b, i: (0, 0)),
        ],
        out_specs=pl.BlockSpec((None, ts, CONV_CH), lambda b, i: (b, i, 0)),
        out_shape=jax.ShapeDtypeStruct((B, S, CONV_CH), BF16),
        scratch_shapes=[pltpu.VMEM((nchunk, CONV_HALO + ts, LANES), F32),
                        pltpu.VMEM((nchunk, ts, LANES), F32)],
        compiler_params=_cparams("arbitrary", "arbitrary"),
        name="conv_mixer",
    )(proj, proj, proj, proj, dw_c, dwb_c, ln_g.astype(F32).reshape(1, -1), ln_b.astype(F32).reshape(1, -1))


def _fgate_kernel(h_ref, w_ref, b_ref, c_ref, carry_ref):
    ts = h_ref.shape[0]

    @pl.when(pl.program_id(1) == 0)
    def _():
        carry_ref[...] = jnp.zeros_like(carry_ref)

    f = jnp.dot(h_ref[...], w_ref[...], preferred_element_type=F32) + b_ref[...]
    ls = jax.nn.log_sigmoid(f)
    row = lax.broadcasted_iota(jnp.int32, (ts, ts), 0)
    col = lax.broadcasted_iota(jnp.int32, (ts, ts), 1)
    tri = jnp.where(row >= col, 1.0, 0.0).astype(BF16)
    hi = ls.astype(BF16)
    r1 = ls - hi.astype(F32)
    mid = r1.astype(BF16)
    lo = (r1 - mid.astype(F32)).astype(BF16)
    cs = (jnp.dot(tri, hi, preferred_element_type=F32)
          + jnp.dot(tri, mid, preferred_element_type=F32)
          + jnp.dot(tri, lo, preferred_element_type=F32))
    c = cs + carry_ref[...]
    c_ref[...] = c
    carry_ref[...] = c[ts - 1:ts, :]


def forget_cumlog(h_bf, w_f, b_f):
    B, S, D = h_bf.shape
    H = w_f.shape[1]
    ts = min(256, S)
    w_p = jnp.zeros((D, LANES), BF16).at[:, :H].set(w_f.astype(BF16))
    b_p = jnp.zeros((1, LANES), F32).at[0, :H].set(b_f.astype(F32))
    return pl.pallas_call(
        _fgate_kernel,
        grid=(B, S // ts),
        in_specs=[pl.BlockSpec((None, ts, D), lambda b, i: (b, i, 0)),
                  pl.BlockSpec((D, LANES), lambda b, i: (0, 0)),
                  pl.BlockSpec((1, LANES), lambda b, i: (0, 0))],
        out_specs=pl.BlockSpec((None, ts, LANES), lambda b, i: (b, i, 0)),
        out_shape=jax.ShapeDtypeStruct((B, S, LANES), F32),
        scratch_shapes=[pltpu.VMEM((1, LANES), F32)],
        compiler_params=_cparams("arbitrary", "arbitrary"),
        name="forget_cumlog",
    )(h_bf, w_p, b_p)


def _fox_kernel(q_ref, k_ref, v_ref, cq_ref, ck_ref, o_ref):
    tq = q_ref.shape[0]
    tk = ck_ref.shape[-1]
    i = pl.program_id(2)
    qscale = HEAD_DIM ** -0.5 * LOG2E
    heads = range(FOX_HP)
    hs = [slice(h * HEAD_DIM, (h + 1) * HEAD_DIM) for h in heads]
    qs = [(q_ref[:, hs[h]].astype(F32) * qscale).astype(BF16) for h in heads]
    cqs = [jnp.concatenate([jnp.broadcast_to(cq_ref[h, r], (LANES, LANES)).T[:, 0:1] for r in range(tq // LANES)],
                           axis=0) for h in heads]

    def step(j, carry, masked):
        start = pl.multiple_of(j * tk, tk)
        ss = []
        for h in heads:
            k = k_ref[pl.ds(start, tk), hs[h]]
            ss.append(lax.dot_general(qs[h], k, (((1,), (1,)), ((), ())), preferred_element_type=F32))
        if masked:
            row = i * tq + lax.broadcasted_iota(jnp.int32, (tq, tk), 0)
            col = j * tk + lax.broadcasted_iota(jnp.int32, (tq, tk), 1)
            causal = row >= col
        ps, ms, alphas = [], [], []
        for h in heads:
            m = carry[h][0]
            s = ss[h] - ck_ref[h, j]
            if masked:
                s = jnp.where(causal, s, -jnp.inf)
            m_new = jnp.maximum(m, jnp.max(s, axis=-1, keepdims=True) + cqs[h])
            p = jnp.exp2(s - (m_new - cqs[h]))
            alpha = jnp.exp2(m - m_new)
            ps.append(p.astype(BF16))
            ms.append(m_new)
            alphas.append(alpha)
        out = []
        for h in heads:
            v1 = jnp.concatenate([v_ref[pl.ds(start, tk), hs[h]], ones], axis=-1)
            acc = alphas[h] * carry[h][1] + jnp.dot(ps[h], v1, preferred_element_type=F32)
            out.append((ms[h], acc))
        return tuple(out)

    ones = jnp.ones((tk, HEAD_DIM), BF16)
    init = tuple((jnp.full((tq, 1), -jnp.inf, F32), jnp.zeros((tq, 2 * HEAD_DIM), F32)) for _ in heads)
    n_full = (i * tq) // tk
    carry = lax.fori_loop(0, n_full, lambda j, c: step(j, c, False), init)
    for d in range(-(-tq // tk)):
        carry = step(n_full + d, carry, True)
    for h in heads:
        acc = carry[h][1]
        o_ref[:, hs[h]] = (acc[:, :HEAD_DIM] / acc[:, HEAD_DIM:]).astype(o_ref.dtype)


def fox_attention(proj, c):
    B, S, _ = proj.shape
    H = N_FOX_HEADS
    tq, tk = min(FOX_TQ, S), min(FOX_TK, S)
    nq, nk = S // tq, S // tk
    ng = H // FOX_HP
    w = FOX_HP * HEAD_DIM
    c_h = jnp.transpose(c[:, :, :H], (0, 2, 1)) * LOG2E
    c_q = c_h.reshape(B, H, S // LANES, 1, LANES)
    c_row = c_h.reshape(B, H, nk, 1, tk)
    return pl.pallas_call(
        _fox_kernel,
        grid=(B, ng, nq),
        in_specs=[
            pl.BlockSpec((None, tq, w), lambda b, g, i: (b, i, g)),
            pl.BlockSpec((None, S, w), lambda b, g, i: (b, 0, ng + g)),
            pl.BlockSpec((None, S, w), lambda b, g, i: (b, 0, 2 * ng + g)),
            pl.BlockSpec((None, FOX_HP, tq // LANES, 1, LANES), lambda b, g, i: (b, g, i, 0, 0)),
            pl.BlockSpec((None, FOX_HP, nk, 1, tk), lambda b, g, i: (b, g, 0, 0, 0)),
        ],
        out_specs=pl.BlockSpec((None, tq, w), lambda b, g, i: (b, i, g)),
        out_shape=jax.ShapeDtypeStruct((B, S, PRIMARY_WIDTH), BF16),
        compiler_params=_cparams("arbitrary", "arbitrary", "arbitrary"),
        name="fox_attention",
    )(proj, proj, proj, c_q, c_row)


def _mem_attn_kernel(q_ref, kv_ref, o_ref):
    scale = HEAD_DIM ** -0.5
    for h in range(N_MEM_HEADS):
        sl = slice(h * HEAD_DIM, (h + 1) * HEAD_DIM)
        slv = slice(MEM_WIDTH + h * HEAD_DIM, MEM_WIDTH + (h + 1) * HEAD_DIM)
        s = lax.dot_general(q_ref[:, sl], kv_ref[:, sl], (((1,), (1,)), ((), ())),
                            preferred_element_type=F32) * scale
        m = jnp.max(s, axis=-1, keepdims=True)
        p = jnp.exp(s - m)
        l = jnp.sum(p, axis=-1, keepdims=True)
        o = jnp.dot(p.astype(BF16), kv_ref[:, slv], preferred_element_type=F32)
        o_ref[:, sl] = (o / l).astype(o_ref.dtype)


def memory_attention(proj, q_block, kv):
    B, S, _ = proj.shape
    M = kv.shape[1]
    tq = min(MEM_TQ, S)
    return pl.pallas_call(
        _mem_attn_kernel,
        grid=(B, S // tq),
        in_specs=[pl.BlockSpec((None, tq, MEM_WIDTH), lambda b, i: (b, i, q_block)),
                  pl.BlockSpec((None, M, 2 * MEM_WIDTH), lambda b, i: (b, 0, 0))],
        out_specs=pl.BlockSpec((None, tq, MEM_WIDTH), lambda b, i: (b, i, 0)),
        out_shape=jax.ShapeDtypeStruct((B, S, MEM_WIDTH), BF16),
        compiler_params=_cparams("arbitrary", "arbitrary"),
        name="memory_attention",
    )(proj, kv)


def _outproj_ln_kernel(p_ref, m_ref, wt_ref, wb_ref, h_ref, g_ref, b_ref, of_ref, ob_ref, wt_bf, wb_bf):
    @pl.when(pl.program_id(0) == 0)
    def _():
        wt_bf[...] = wt_ref[...].astype(BF16)
        wb_bf[...] = wb_ref[...].astype(BF16)

    mix = (jnp.dot(p_ref[...], wt_bf[...], preferred_element_type=F32)
           + jnp.dot(m_ref[...], wb_bf[...], preferred_element_type=F32))
    y = _ln_rows(DEEPNORM_ALPHA * h_ref[...] + mix, g_ref[...], b_ref[...])
    of_ref[...] = y
    ob_ref[...] = y.astype(BF16)


def outproj_ln(prim, memo, w_out, layer, h, g, b):
    N, D = h.shape
    tm = min(ROW_TILE, N)
    row = lambda i: (i, 0)
    const = lambda i: (0, 0)
    once = pl.Buffered(1)
    return pl.pallas_call(
        _outproj_ln_kernel,
        grid=(N // tm,),
        in_specs=[pl.BlockSpec((tm, PRIMARY_WIDTH), row), pl.BlockSpec((tm, MEM_WIDTH), row),
                  pl.BlockSpec((None, PRIMARY_WIDTH, D), lambda i: (layer, 0, 0), pipeline_mode=once),
                  pl.BlockSpec((None, MEM_WIDTH, D), lambda i: (layer, PRIMARY_WIDTH // MEM_WIDTH, 0),
                               pipeline_mode=once),
                  pl.BlockSpec((tm, D), row), pl.BlockSpec((1, D), const), pl.BlockSpec((1, D), const)],
        out_specs=[pl.BlockSpec((tm, D), row), pl.BlockSpec((tm, D), row)],
        out_shape=[jax.ShapeDtypeStruct((N, D), F32), jax.ShapeDtypeStruct((N, D), BF16)],
        scratch_shapes=[pltpu.VMEM((PRIMARY_WIDTH, D), BF16), pltpu.VMEM((MEM_WIDTH, D), BF16)],
        compiler_params=_cparams("arbitrary"),
        name="outproj_ln",
    )(prim, memo, w_out, w_out, h, g.astype(F32).reshape(1, D), b.astype(F32).reshape(1, D))


def _route_kernel(h_ref, w_ref, b_ref, idx_ref, rank_ref, gate_ref, cnt_ref, carry_ref):
    T = h_ref.shape[0]
    E = w_ref.shape[0]
    gsz = E // N_GROUPS
    neg = -jnp.inf

    @pl.when(pl.program_id(0) == 0)
    def _():
        carry_ref[...] = jnp.zeros_like(carry_ref)

    def split(a):
        hi = a.astype(BF16)
        return hi, (a - hi.astype(F32)).astype(BF16)

    nt = lambda a, b: lax.dot_general(a, b, (((1,), (1,)), ((), ())), preferred_element_type=F32)
    w_hi, w_lo = split(w_ref[...])
    h_hi, h_lo = split(h_ref[...])
    logits = nt(w_hi, h_hi) + (nt(w_hi, h_lo) + nt(w_lo, h_hi))
    scores = jax.nn.sigmoid(logits)
    choice = scores + b_ref[...]

    io8 = lax.broadcasted_iota(jnp.int32, (gsz, T), 0)
    rows = []
    for g in range(N_GROUPS):
        blk = choice[g * gsz:(g + 1) * gsz, :]
        m1 = jnp.max(blk, axis=0, keepdims=True)
        f1 = jnp.min(jnp.where(blk == m1, io8, gsz), axis=0, keepdims=True)
        m2 = jnp.max(jnp.where(io8 == f1, neg, blk), axis=0, keepdims=True)
        rows.append(jnp.broadcast_to(m1 + m2, (gsz, T)))
    gs = jnp.concatenate(rows, axis=0)
    eid = lax.broadcasted_iota(jnp.int32, (E, T), 0)
    gid = eid // gsz
    emask = jnp.zeros((E, T), jnp.bool_)
    for _ in range(TOPK_GROUPS):
        m = jnp.max(gs, axis=0, keepdims=True)
        g = jnp.min(jnp.where(gs == m, gid, N_GROUPS), axis=0, keepdims=True)
        hit = gid == g
        emask = jnp.logical_or(emask, hit)
        gs = jnp.where(hit, neg, gs)

    masked = jnp.where(emask, choice, neg)
    sel = jnp.zeros((E, T), jnp.bool_)
    ids, ws, hits = [], [], []
    for _ in range(TOP_K):
        m = jnp.max(masked, axis=0, keepdims=True)
        i_k = jnp.min(jnp.where(masked == m, eid, E), axis=0, keepdims=True)
        hit = eid == i_k
        ids.append(i_k)
        ws.append(jnp.sum(jnp.where(hit, scores, 0.0), axis=0, keepdims=True))
        hits.append(hit)
        masked = jnp.where(hit, neg, masked)
        sel = jnp.logical_or(sel, hit)
    wsum = ws[0]
    for w in ws[1:]:
        wsum = wsum + w

    sel_f = jnp.where(sel, 1.0, 0.0)
    r = lax.broadcasted_iota(jnp.int32, (T, T), 0)
    c = lax.broadcasted_iota(jnp.int32, (T, T), 1)
    tri = jnp.where(r < c, 1.0, 0.0).astype(BF16)
    before = jnp.dot(sel_f.astype(BF16), tri, preferred_element_type=F32) + carry_ref[...]
    for k in range(TOP_K):
        idx_ref[k:k + 1, :] = ids[k]
        rank_ref[k:k + 1, :] = jnp.sum(jnp.where(hits[k], before, 0.0), axis=0, keepdims=True).astype(jnp.int32)
        gate_ref[k:k + 1, :] = ws[k] / wsum * ROUTED_SCALE
    total = carry_ref[...] + jnp.sum(sel_f, axis=1, keepdims=True)
    carry_ref[...] = total
    cnt_ref[...] = jnp.broadcast_to(total, cnt_ref.shape).astype(jnp.int32)


def route(h, w_r, r_bias):
    N, D = h.shape
    E = w_r.shape[1]
    T = min(ROUTE_T, N)
    kn = lambda i: (0, i)
    idx, rank, gate, cnt = pl.pallas_call(
        _route_kernel,
        grid=(N // T,),
        in_specs=[pl.BlockSpec((T, D), lambda i: (i, 0)), pl.BlockSpec((E, D), lambda i: (0, 0)),
                  pl.BlockSpec((E, 1), lambda i: (0, 0))],
        out_specs=[pl.BlockSpec((TOP_K, T), kn), pl.BlockSpec((TOP_K, T), kn), pl.BlockSpec((TOP_K, T), kn),
                   pl.BlockSpec((E, LANES), lambda i: (0, 0))],
        out_shape=[jax.ShapeDtypeStruct((TOP_K, N), jnp.int32), jax.ShapeDtypeStruct((TOP_K, N), jnp.int32),
                   jax.ShapeDtypeStruct((TOP_K, N), F32), jax.ShapeDtypeStruct((E, LANES), jnp.int32)],
        scratch_shapes=[pltpu.VMEM((E, 1), F32)],
        compiler_params=_cparams("arbitrary"),
        name="route",
    )(h, w_r.astype(F32).T, r_bias.astype(F32).reshape(E, 1))
    return idx, rank, gate, cnt[:, 0]


def _pos_kernel(starts_ref, idx_ref, rank_ref, pos_ref):
    idx = idx_ref[...]
    acc = rank_ref[...]
    for e in range(N_EXPERTS):
        acc = acc + jnp.where(idx == e, starts_ref[e], 0)
    pos_ref[...] = acc


def slot_positions(starts, idx, rank):
    K, N = idx.shape
    T = min(2048, N)
    grid_spec = pltpu.PrefetchScalarGridSpec(
        num_scalar_prefetch=1, grid=(N // T,),
        in_specs=[pl.BlockSpec((K, T), lambda i, s: (0, i)), pl.BlockSpec((K, T), lambda i, s: (0, i))],
        out_specs=pl.BlockSpec((K, T), lambda i, s: (0, i)))
    return pl.pallas_call(
        _pos_kernel, grid_spec=grid_spec, out_shape=jax.ShapeDtypeStruct((K, N), jnp.int32),
        compiler_params=_cparams("arbitrary"), name="slot_positions",
    )(starts.astype(jnp.int32), idx, rank)


def _pack_pair(a, b):
    ua = lax.bitcast_convert_type(a.astype(BF16).astype(F32), jnp.uint32)
    ub = lax.bitcast_convert_type(b.astype(BF16).astype(F32), jnp.uint32)
    return ub | (ua >> 16)


def _unpack_pair(w):
    lo = lax.bitcast_convert_type(w << 16, F32)
    hi = lax.bitcast_convert_type(w & jnp.uint32(0xFFFF0000), F32)
    return lo, hi


def _rows_to_slab(x):
    return [_pack_pair(x[:, 2 * j * LANES:(2 * j + 1) * LANES], x[:, (2 * j + 1) * LANES:(2 * j + 2) * LANES])
            for j in range(SLAB_ROWS)]


def _slab_to_rows(pieces):
    return jnp.concatenate([half for w in pieces for half in _unpack_pair(w)], axis=-1)


def _tiles(a, T):
    K, N = a.shape
    return a.reshape(K, N // T, T).transpose(1, 0, 2)


def _swiglu_bf(x, wu, wd):
    f = wd.shape[0]
    h = jnp.dot(x, wu, preferred_element_type=F32)
    a = h[:, :f]
    act = (a * jax.nn.sigmoid(a)) * h[:, f:]
    return jnp.dot(act.astype(BF16), wd, preferred_element_type=F32)


def _dispatch_kernel(pos_ref, hb_ref, su_ref, sd_ref, sh_ref, xs_hbm, slab, su_bf, sd_bf, sem):
    T = hb_ref.shape[0]

    @pl.when(pl.program_id(0) == 0)
    def _():
        su_bf[...] = su_ref[...].astype(BF16)
        sd_bf[...] = sd_ref[...].astype(BF16)

    hb = hb_ref[...]
    for j, piece in enumerate(_rows_to_slab(hb.astype(F32))):
        slab[pl.ds(j, T, stride=SLAB_ROWS), :] = piece

    for t in range(T):
        src = slab.at[pl.ds(t * SLAB_ROWS, SLAB_ROWS)]
        for k in range(TOP_K):
            dst = xs_hbm.at[pl.ds(pl.multiple_of(pos_ref[k, t] * SLAB_ROWS, SLAB_ROWS), SLAB_ROWS)]
            pltpu.make_async_copy(src, dst, sem).start(priority=k % 2)
    sh_ref[...] = _swiglu_bf(hb, su_bf[...], sd_bf[...]).astype(sh_ref.dtype)
    for k in range(TOP_K):
        pltpu.make_async_copy(slab, xs_hbm.at[pl.ds(0, T * SLAB_ROWS)], sem).wait()


def dispatch_shared(pos_t, h_bf, sh_up, sh_down, layer):
    N, D = h_bf.shape
    nt, K, T = pos_t.shape
    F2 = sh_up.shape[-1]
    Fd = sh_down.shape[1]
    return pl.pallas_call(
        _dispatch_kernel,
        grid=(nt,),
        in_specs=[pl.BlockSpec((None, K, T), lambda i: (i, 0, 0), memory_space=pltpu.SMEM),
                  pl.BlockSpec((T, D), lambda i: (i, 0)),
                  pl.BlockSpec((None, D, F2), lambda i: (layer, 0, 0)),
                  pl.BlockSpec((None, Fd, D), lambda i: (layer, 0, 0))],
        out_specs=[pl.BlockSpec((T, D), lambda i: (i, 0)), pl.BlockSpec(memory_space=pl.ANY)],
        out_shape=[jax.ShapeDtypeStruct((N, D), BF16),
                   jax.ShapeDtypeStruct((N * K * SLAB_ROWS, LANES), jnp.uint32)],
        scratch_shapes=[pltpu.VMEM((T * SLAB_ROWS, LANES), jnp.uint32), pltpu.VMEM((D, F2), BF16),
                        pltpu.VMEM((Fd, D), BF16), pltpu.SemaphoreType.DMA(())],
        compiler_params=_cparams("arbitrary"),
        name="dispatch_shared",
    )(pos_t, h_bf, sh_up, sh_down)


def _expert_kernel(layer, tile_ref, be_ref, lo_ref, hi_ref, first_ref, slot_ref, next_ref, half_ref, x_ref, wu_hbm, wd_hbm,
                   o_ref, wu_f32, wd_f32, wu_bf, wd_bf, sem):
    v = pl.program_id(0)
    tm = x_ref.shape[0] // SLAB_ROWS
    lo = lo_ref[v]
    hi = hi_ref[v]

    def weight_copies(e, slot):
        return (pltpu.make_async_copy(wu_hbm.at[layer, e], wu_f32.at[slot], sem.at[0, slot]),
                pltpu.make_async_copy(wd_hbm.at[layer, e], wd_f32.at[slot], sem.at[1, slot]))

    @pl.when(hi > lo)
    def _():
        changed = jnp.logical_or(v == 0, be_ref[v] != be_ref[jnp.maximum(v - 1, 0)])

        @pl.when(changed)
        def _():
            slot = slot_ref[v]

            @pl.when(v == 0)
            def _():
                for c in weight_copies(be_ref[v], slot):
                    c.start()

            for c in weight_copies(be_ref[v], slot):
                c.wait()

            @pl.when(next_ref[v] >= 0)
            def _():
                for c in weight_copies(next_ref[v], 1 - slot):
                    c.start()

            wu_bf[...] = wu_f32[slot].astype(BF16)
            wd_bf[...] = wd_f32[slot].astype(BF16)

        def ffn_rows(r0, n):
            base = r0 * SLAB_ROWS
            x = _slab_to_rows([x_ref[pl.ds(base + j, n, stride=SLAB_ROWS), :] for j in range(SLAB_ROWS)])
            y = _swiglu_bf(x.astype(BF16), wu_bf[...], wd_bf[...])
            row = r0 + lax.broadcasted_iota(jnp.int32, (n, 1), 0)
            mine = jnp.logical_and(row >= lo, row < hi)
            pieces = _rows_to_slab(y)

            @pl.when(first_ref[v] == 1)
            def _():
                for j in range(SLAB_ROWS):
                    o_ref[pl.ds(base + j, n, stride=SLAB_ROWS), :] = jnp.where(mine, pieces[j], jnp.uint32(0))

            @pl.when(first_ref[v] == 0)
            def _():
                for j in range(SLAB_ROWS):
                    keep = o_ref[pl.ds(base + j, n, stride=SLAB_ROWS), :]
                    o_ref[pl.ds(base + j, n, stride=SLAB_ROWS), :] = jnp.where(mine, pieces[j], keep)

        half = tm // 2
        mode = half_ref[v]

        @pl.when(mode == 0)
        def _():
            ffn_rows(0, tm)

        @pl.when(mode != 0)
        def _():
            r0 = pl.multiple_of((mode - 1) * half, half)

            @pl.when(first_ref[v] == 1)
            def _():
                other = pl.multiple_of((half - r0) * SLAB_ROWS, half * SLAB_ROWS)
                o_ref[pl.ds(other, half * SLAB_ROWS), :] = jnp.zeros((half * SLAB_ROWS, LANES), jnp.uint32)

            ffn_rows(r0, half)


def _lookup(table, idx):
    n = table.shape[0]
    return jnp.sum(jnp.where(idx[:, None] == jnp.arange(n)[None, :], table[None, :], 0), axis=1)


def _visit_tables(counts, tm, A):
    E = counts.shape[0]
    V = A // tm + E
    starts = jnp.cumsum(counts) - counts
    ends = starts + counts
    first_tile = starts // tm
    nvis = jnp.where(counts > 0, (ends - 1) // tm - first_tile + 1, 0)
    vend = jnp.cumsum(nvis)
    voff = vend - nvis
    total = vend[-1]
    v = jnp.arange(V, dtype=jnp.int32)
    e_v = jnp.minimum(jnp.sum(vend[None, :] <= v[:, None], axis=1), E - 1).astype(jnp.int32)
    valid = v < total
    e_last = jnp.sum(jnp.where(v == total - 1, e_v, 0))
    e_v = jnp.where(valid, e_v, e_last)
    tile_v = jnp.where(valid, _lookup(first_tile, e_v) + v - _lookup(voff, e_v), A // tm - 1)
    base = tile_v * tm
    lo_v = jnp.where(valid, jnp.maximum(_lookup(starts, e_v), base) - base, 0)
    hi_v = jnp.where(valid, jnp.minimum(_lookup(ends, e_v), base + tm) - base, 0)
    prev_tile = jnp.concatenate([jnp.full((1,), -1, tile_v.dtype), tile_v[:-1]])
    first_v = jnp.logical_and(valid, tile_v != prev_tile)
    has = nvis > 0
    ids = jnp.arange(E)
    slot_e = (jnp.cumsum(has) - 1) & 1
    later = jnp.logical_and(ids[None, :] > ids[:, None], has[None, :])
    next_e = jnp.min(jnp.where(later, ids[None, :], E), axis=1)
    next_e = jnp.where(next_e == E, -1, next_e)
    i32 = lambda a: a.astype(jnp.int32)
    half_v = jnp.where(hi_v <= tm // 2, 1, jnp.where(lo_v >= tm // 2, 2, 0))
    return starts, (i32(tile_v), i32(e_v), i32(lo_v), i32(hi_v), i32(first_v),
                    i32(_lookup(slot_e, e_v)), i32(_lookup(next_e, e_v)), i32(half_v))


def expert_ffn(xs, tables, w_up, w_down, layer):
    V = tables[0].shape[0]
    D, F2 = w_up.shape[-2:]
    Fd = w_down.shape[-2]
    blk = EXP_TM * SLAB_ROWS
    nt = len(tables)
    tile_map = lambda v, *t: (t[0][v], 0)
    grid_spec = pltpu.PrefetchScalarGridSpec(
        num_scalar_prefetch=nt,
        grid=(V,),
        in_specs=[pl.BlockSpec((blk, LANES), tile_map),
                  pl.BlockSpec(memory_space=pl.ANY), pl.BlockSpec(memory_space=pl.ANY)],
        out_specs=pl.BlockSpec((blk, LANES), tile_map),
        scratch_shapes=[pltpu.VMEM((2, D, F2), F32), pltpu.VMEM((2, Fd, D), F32),
                        pltpu.VMEM((D, F2), BF16), pltpu.VMEM((Fd, D), BF16),
                        pltpu.SemaphoreType.DMA((2, 2))],
    )
    return pl.pallas_call(
        functools.partial(_expert_kernel, layer),
        grid_spec=grid_spec,
        out_shape=jax.ShapeDtypeStruct(xs.shape, jnp.uint32),
        compiler_params=_cparams("arbitrary"),
        name="expert_ffn",
    )(*tables, xs, w_up, w_down)


def _combine_ln_kernel(pos_ref, nxt_ref, gate_ref, h_ref, sh_ref, g_ref, b_ref, y_hbm, of_ref, ob_ref,
                       ybuf, acc_lo, acc_hi, gate_rows, sem):
    T = h_ref.shape[0]
    i = pl.program_id(0)
    slot = i % 2

    last = i == pl.num_programs(0) - 1

    def issue_tile(p_ref, s):
        for t in range(T):
            for k in range(TOP_K):
                src0 = pl.multiple_of(p_ref[k, t] * SLAB_ROWS, SLAB_ROWS)
                pltpu.make_async_copy(y_hbm.at[pl.ds(src0, SLAB_ROWS)],
                                      ybuf.at[s, k, pl.ds(t * SLAB_ROWS, SLAB_ROWS)],
                                      sem.at[s]).start(priority=k % 2)

    def wait_tile(s):
        for k in range(TOP_K):
            pltpu.make_async_copy(y_hbm.at[pl.ds(0, T * SLAB_ROWS)], ybuf.at[s, k], sem.at[s]).wait()

    for k in range(TOP_K):
        for c in range(T // LANES):
            gk = gate_ref[k:k + 1, c * LANES:(c + 1) * LANES]
            gate_rows[k, c * LANES:(c + 1) * LANES, :] = jnp.broadcast_to(gk, (LANES, LANES)).T

    @pl.when(i == 0)
    def _():
        issue_tile(pos_ref, slot)

    wait_tile(slot)
    issue_tile(nxt_ref, 1 - slot)
    for t in range(T):
        r0 = t * SLAB_ROWS
        lo, hi = _unpack_pair(ybuf[slot, 0, pl.ds(r0, SLAB_ROWS), :])
        g = gate_rows[0, t:t + 1, :]
        lo, hi = g * lo, g * hi
        for k in range(1, TOP_K):
            l2, h2 = _unpack_pair(ybuf[slot, k, pl.ds(r0, SLAB_ROWS), :])
            g = gate_rows[k, t:t + 1, :]
            lo, hi = lo + g * l2, hi + g * h2
        acc_lo[pl.ds(r0, SLAB_ROWS), :] = lo
        acc_hi[pl.ds(r0, SLAB_ROWS), :] = hi

    @pl.when(last)
    def _():
        wait_tile(1 - slot)

    routed = jnp.concatenate([acc[pl.ds(j, T, stride=SLAB_ROWS), :] for j in range(SLAB_ROWS)
                              for acc in (acc_lo, acc_hi)], axis=-1)
    y = DEEPNORM_ALPHA * h_ref[...] + routed + sh_ref[...].astype(F32)
    y = _ln_rows(y, g_ref[...], b_ref[...])
    of_ref[...] = y
    ob_ref[...] = y.astype(BF16)


def combine_ln(pos_t, gate_t, h, shared, y, g, b):
    N, D = h.shape
    nt, K, T = pos_t.shape
    row = lambda i: (i, 0)
    const = lambda i: (0, 0)
    return pl.pallas_call(
        _combine_ln_kernel,
        grid=(nt,),
        in_specs=[pl.BlockSpec((None, K, T), lambda i: (i, 0, 0), memory_space=pltpu.SMEM),
                  pl.BlockSpec((None, K, T), lambda i: (jnp.minimum(i + 1, nt - 1), 0, 0), memory_space=pltpu.SMEM),
                  pl.BlockSpec((None, K, T), lambda i: (i, 0, 0)),
                  pl.BlockSpec((T, D), row), pl.BlockSpec((T, D), row),
                  pl.BlockSpec((1, D), const), pl.BlockSpec((1, D), const),
                  pl.BlockSpec(memory_space=pl.ANY)],
        out_specs=[pl.BlockSpec((T, D), row), pl.BlockSpec((T, D), row)],
        out_shape=[jax.ShapeDtypeStruct((N, D), F32), jax.ShapeDtypeStruct((N, D), BF16)],
        scratch_shapes=[pltpu.VMEM((2, K, T * SLAB_ROWS, LANES), jnp.uint32), pltpu.VMEM((T * SLAB_ROWS, LANES), F32),
                        pltpu.VMEM((T * SLAB_ROWS, LANES), F32), pltpu.VMEM((K, T, LANES), F32),
                        pltpu.SemaphoreType.DMA((2,))],
        compiler_params=_cparams("arbitrary"),
        name="combine_ln",
    )(pos_t, pos_t, gate_t, h, shared, g.astype(F32).reshape(1, D), b.astype(F32).reshape(1, D), y)


def moe_ffn_ln(h, h_bf, layer, router_w, router_bias, exp_w_up, exp_w_down, shared_w_up, shared_w_down, g, b):
    N, D = h.shape
    idx, rank, gate, counts = route(h, router_w[layer], router_bias[layer])
    starts, tables = _visit_tables(counts, EXP_TM, N * TOP_K)
    pos = slot_positions(starts, idx, rank)
    shared, xs = dispatch_shared(_tiles(pos, min(DISPATCH_T, N)), h_bf, shared_w_up, shared_w_down, layer)
    y = expert_ffn(xs, tables, exp_w_up, exp_w_down, layer)
    tc = min(COMBINE_T, N)
    return combine_ln(_tiles(pos, tc), _tiles(gate, tc), h, shared, y, g, b)


def kernel(x, mem, conv_w_in, conv_dw, conv_dw_b, conv_ln_g, conv_ln_b, fox_w_in, fox_b_f, mem_w_kv, w_out,
           ln_g, ln_b, router_w, router_bias, exp_w_up, exp_w_down, shared_w_up, shared_w_down):
    B, S, D = x.shape
    N = B * S
    h = x.reshape(N, D).astype(F32)
    h_bf = h.astype(BF16)
    mem_bf = mem.reshape(B * MEM_LEN, D).astype(BF16)
    nf = 3 * PRIMARY_WIDTH
    for i in range(DEPTH):
        j = i // N_MIXERS
        if i % N_MIXERS == 0:
            proj = matmul(h_bf, conv_w_in, j, 2 * CONV_CH + MEM_WIDTH, BF16).reshape(B, S, -1)
            prim = conv_mixer(proj, conv_dw[j], conv_dw_b[j], conv_ln_g[j], conv_ln_b[j])
            q_src, q_block = proj, 2 * CONV_CH // MEM_WIDTH
        else:
            proj = matmul(h_bf, fox_w_in, j, nf, BF16).reshape(B, S, nf)
            tail = trailing_columns(fox_w_in, j, nf)
            w_qm = tail[None, :, N_FOX_HEADS:N_FOX_HEADS + MEM_WIDTH]
            q_src, q_block = matmul(h_bf, w_qm, 0, MEM_WIDTH, BF16).reshape(B, S, MEM_WIDTH), 0
            c = forget_cumlog(h_bf.reshape(B, S, D), tail[:, :N_FOX_HEADS], fox_b_f[j])
            prim = fox_attention(proj, c)
        kv = matmul(mem_bf, mem_w_kv, i, 2 * MEM_WIDTH, BF16).reshape(B, MEM_LEN, 2 * MEM_WIDTH)
        memo = memory_attention(q_src, q_block, kv)
        h, h_bf = outproj_ln(prim.reshape(N, PRIMARY_WIDTH), memo.reshape(N, MEM_WIDTH), w_out, i, h,
                             ln_g[i, 0], ln_b[i, 0])
        h, h_bf = moe_ffn_ln(h, h_bf, i, router_w, router_bias, exp_w_up, exp_w_down,
                             shared_w_up, shared_w_down, ln_g[i, 1], ln_b[i, 1])
    return h.reshape(B, S, D)
```

```python
import functools

import jax
import jax.numpy as jnp
from jax import lax
from jax.experimental import pallas as pl
from jax.experimental.pallas import tpu as pltpu

F32 = jnp.float32
BF16 = jnp.bfloat16

D_MODEL = 2048
DEPTH = 4
N_MIXERS = 2
MEM_LEN = 256
HEAD_DIM = 128
N_MEM_HEADS = 4
MEM_WIDTH = N_MEM_HEADS * HEAD_DIM
PRIMARY_WIDTH = D_MODEL - MEM_WIDTH
CONV_CH = PRIMARY_WIDTH
CONV_WIDTH = 31
N_FOX_HEADS = PRIMARY_WIDTH // HEAD_DIM
N_EXPERTS = 64
TOP_K = 8
N_GROUPS = 8
TOPK_GROUPS = 4
D_EXPERT = 512
ROUTED_SCALE = 2.5
LN_EPS = 1e-5
DEEPNORM_ALPHA = (2 * DEPTH) ** 0.25
LOG2E = 1.4426950408889634

LANES = 128
VMEM_LIMIT = 56 * 1024 * 1024
MM_TM, MM_TN = 1024, 512
ROW_TILE = 256
CONV_TS = 256
CONV_HALO = 32
CONV_SUB = 64
FOX_TQ = 2048
FOX_TK = 1024
FOX_HP = 2
MEM_TQ = 512
EXP_TM = 256
SLAB_ROWS = D_MODEL // LANES // 2
ROUTE_T = 256
DISPATCH_T = 256
COMBINE_T = 128


def _cparams(*sem):
    return pltpu.CompilerParams(dimension_semantics=sem, vmem_limit_bytes=VMEM_LIMIT)


def _ln_rows(y, g, b):
    mu = jnp.mean(y, axis=-1, keepdims=True)
    d = y - mu
    var = jnp.mean(d * d, axis=-1, keepdims=True)
    return d * lax.rsqrt(var + LN_EPS) * g + b


def _mm_kernel(a_ref, w_ref, o_ref, w_bf):
    @pl.when(pl.program_id(1) == 0)
    def _():
        w_bf[...] = w_ref[...].astype(BF16)

    o_ref[...] = jnp.dot(a_ref[...], w_bf[...], preferred_element_type=F32).astype(o_ref.dtype)


def matmul(a, w, layer, n_cols, out_dtype, tm=MM_TM, tn=MM_TN):
    M, K = a.shape
    tm, tn = min(tm, M), min(tn, n_cols)
    assert M % tm == 0 and n_cols % tn == 0 and w.shape[1] == K
    return pl.pallas_call(
        _mm_kernel,
        grid=(n_cols // tn, M // tm),
        in_specs=[pl.BlockSpec((tm, K), lambda j, i: (i, 0)),
                  pl.BlockSpec((None, K, tn), lambda j, i: (layer, 0, j))],
        out_specs=pl.BlockSpec((tm, tn), lambda j, i: (i, j)),
        out_shape=jax.ShapeDtypeStruct((M, n_cols), out_dtype),
        scratch_shapes=[pltpu.VMEM((K, tn), BF16)],
        compiler_params=_cparams("arbitrary", "arbitrary"),
        name="matmul",
    )(a, w)


def _cols_kernel(c0, w_ref, o_ref):
    n = w_ref.shape[1] - c0
    o_ref[:, :n] = w_ref[:, c0:]
    o_ref[:, n:] = jnp.zeros((o_ref.shape[0], o_ref.shape[1] - n), o_ref.dtype)


def trailing_columns(w, layer, c0):
    _, K, C = w.shape
    n_pad = -(-(C - c0) // LANES) * LANES
    tk = min(256, K)
    return pl.pallas_call(
        functools.partial(_cols_kernel, c0),
        grid=(K // tk,),
        in_specs=[pl.BlockSpec((None, tk, C), lambda i: (layer, i, 0))],
        out_specs=pl.BlockSpec((tk, n_pad), lambda i: (i, 0)),
        out_shape=jax.ShapeDtypeStruct((K, n_pad), w.dtype),
        compiler_params=_cparams("arbitrary"),
        name="trailing_columns",
    )(w)


def _conv_kernel(a_ref, g_ref, ah_ref, gh_ref, dw_ref, dwb_ref, lg_ref, lb_ref, o_ref, u_scr, y_scr):
    ts = a_ref.shape[0]
    nchunk = CONV_CH // LANES
    first = pl.program_id(1) == 0
    uh = ah_ref[...].astype(F32) * jax.nn.sigmoid(gh_ref[...].astype(F32))
    uh = jnp.where(first, 0.0, uh)
    u = a_ref[...].astype(F32) * jax.nn.sigmoid(g_ref[...].astype(F32))
    for c in range(nchunk):
        sl = slice(c * LANES, (c + 1) * LANES)
        u_scr[c, 0:CONV_HALO, :] = uh[:, sl]
        u_scr[c, CONV_HALO:CONV_HALO + ts, :] = u[:, sl]

    off = CONV_HALO - (CONV_WIDTH - 1)

    def chunk_body(c, carry):
        w = dw_ref[c]
        bias = dwb_ref[c]
        for r in range(ts // CONV_SUB):
            acc = jnp.broadcast_to(bias, (CONV_SUB, LANES))
            for j in range(CONV_WIDTH):
                acc = acc + w[j:j + 1, :] * u_scr[c, pl.ds(off + j + r * CONV_SUB, CONV_SUB), :]
            y_scr[c, r * CONV_SUB:(r + 1) * CONV_SUB, :] = acc
        return carry

    lax.fori_loop(0, nchunk, chunk_body, 0)

    s1 = jnp.zeros((ts, 1), F32)
    for c in range(nchunk):
        s1 = s1 + jnp.sum(y_scr[c], axis=-1, keepdims=True)
    mu = s1 * (1.0 / CONV_CH)
    s2 = jnp.zeros((ts, 1), F32)
    for c in range(nchunk):
        d = y_scr[c] - mu
        s2 = s2 + jnp.sum(d * d, axis=-1, keepdims=True)
    rstd = lax.rsqrt(s2 * (1.0 / CONV_CH) + LN_EPS)
    for c in range(nchunk):
        sl = slice(c * LANES, (c + 1) * LANES)
        z = (y_scr[c] - mu) * rstd * lg_ref[:, sl] + lb_ref[:, sl]
        o_ref[:, sl] = (z * jax.nn.sigmoid(z)).astype(o_ref.dtype)


def conv_mixer(proj, dw, dw_b, ln_g, ln_b):
    B, S, _ = proj.shape
    ts = min(CONV_TS, S)
    nchunk = CONV_CH // LANES
    hb = ts // CONV_HALO
    dw_p = jnp.zeros((CONV_HALO, CONV_CH), F32).at[:CONV_WIDTH].set(dw.astype(F32))
    dw_c = dw_p.reshape(CONV_HALO, nchunk, LANES).transpose(1, 0, 2)
    dwb_c = dw_b.astype(F32).reshape(nchunk, 1, LANES)
    halo_idx = lambda b, i: (b, jnp.maximum(i * hb - 1, 0), 0)
    halo_idx_g = lambda b, i: (b, jnp.maximum(i * hb - 1, 0), 1)
    return pl.pallas_call(
        _conv_kernel,
        grid=(B, S // ts),
        in_specs=[
            pl.BlockSpec((None, ts, CONV_CH), lambda b, i: (b, i, 0)),
            pl.BlockSpec((None, ts, CONV_CH), lambda b, i: (b, i, 1)),
            pl.BlockSpec((None, CONV_HALO, CONV_CH), halo_idx),
            pl.BlockSpec((None, CONV_HALO, CONV_CH), halo_idx_g),
            pl.BlockSpec((nchunk, CONV_HALO, LANES), lambda b, i: (0, 0, 0)),
            pl.BlockSpec((nchunk, 1, LANES), lambda b, i: (0, 0, 0)),
            pl.BlockSpec((1, CONV_CH), lambda b, i: (0, 0)),
            pl.BlockSpec((1, CONV_CH), lambda b, i: (0, 0)),
        ],
        out_specs=pl.BlockSpec((None, ts, CONV_CH), lambda b, i: (b, i, 0)),
        out_shape=jax.ShapeDtypeStruct((B, S, CONV_CH), BF16),
        scratch_shapes=[pltpu.VMEM((nchunk, CONV_HALO + ts, LANES), F32),
                        pltpu.VMEM((nchunk, ts, LANES), F32)],
        compiler_params=_cparams("arbitrary", "arbitrary"),
        name="conv_mixer",
    )(proj, proj, proj, proj, dw_c, dwb_c, ln_g.astype(F32).reshape(1, -1), ln_b.astype(F32).reshape(1, -1))


def _fgate_kernel(h_ref, w_ref, b_ref, c_ref, carry_ref):
    ts = h_ref.shape[0]

    @pl.when(pl.program_id(1) == 0)
    def _():
        carry_ref[...] = jnp.zeros_like(carry_ref)

    f = jnp.dot(h_ref[...], w_ref[...], preferred_element_type=F32) + b_ref[...]
    ls = jax.nn.log_sigmoid(f)
    row = lax.broadcasted_iota(jnp.int32, (ts, ts), 0)
    col = lax.broadcasted_iota(jnp.int32, (ts, ts), 1)
    tri = jnp.where(row >= col, 1.0, 0.0).astype(BF16)
    hi = ls.astype(BF16)
    r1 = ls - hi.astype(F32)
    mid = r1.astype(BF16)
    lo = (r1 - mid.astype(F32)).astype(BF16)
    cs = (jnp.dot(tri, hi, preferred_element_type=F32)
          + jnp.dot(tri, mid, preferred_element_type=F32)
          + jnp.dot(tri, lo, preferred_element_type=F32))
    c = cs + carry_ref[...]
    c_ref[...] = c
    carry_ref[...] = c[ts - 1:ts, :]


def forget_cumlog(h_bf, w_f, b_f):
    B, S, D = h_bf.shape
    H = w_f.shape[1]
    ts = min(256, S)
    w_p = jnp.zeros((D, LANES), BF16).at[:, :H].set(w_f.astype(BF16))
    b_p = jnp.zeros((1, LANES), F32).at[0, :H].set(b_f.astype(F32))
    return pl.pallas_call(
        _fgate_kernel,
        grid=(B, S // ts),
        in_specs=[pl.BlockSpec((None, ts, D), lambda b, i: (b, i, 0)),
                  pl.BlockSpec((D, LANES), lambda b, i: (0, 0)),
                  pl.BlockSpec((1, LANES), lambda b, i: (0, 0))],
        out_specs=pl.BlockSpec((None, ts, LANES), lambda b, i: (b, i, 0)),
        out_shape=jax.ShapeDtypeStruct((B, S, LANES), F32),
        scratch_shapes=[pltpu.VMEM((1, LANES), F32)],
        compiler_params=_cparams("arbitrary", "arbitrary"),
        name="forget_cumlog",
    )(h_bf, w_p, b_p)


def _fox_kernel(q_ref, k_ref, v_ref, cq_ref, ck_ref, o_ref):
    tq = q_ref.shape[0]
    tk = ck_ref.shape[-1]
    i = pl.program_id(2)
    qscale = HEAD_DIM ** -0.5 * LOG2E
    heads = range(FOX_HP)
    hs = [slice(h * HEAD_DIM, (h + 1) * HEAD_DIM) for h in heads]
    qs = [(q_ref[:, hs[h]].astype(F32) * qscale).astype(BF16) for h in heads]
    cqs = [jnp.concatenate([jnp.broadcast_to(cq_ref[h, r], (LANES, LANES)).T[:, 0:1] for r in range(tq // LANES)],
                           axis=0) for h in heads]

    def step(j, carry, masked):
        start = pl.multiple_of(j * tk, tk)
        ss = []
        for h in heads:
            k = k_ref[pl.ds(start, tk), hs[h]]
            ss.append(lax.dot_general(qs[h], k, (((1,), (1,)), ((), ())), preferred_element_type=F32))
        if masked:
            row = i * tq + lax.broadcasted_iota(jnp.int32, (tq, tk), 0)
            col = j * tk + lax.broadcasted_iota(jnp.int32, (tq, tk), 1)
            causal = row >= col
        ps, ms, alphas = [], [], []
        for h in heads:
            m = carry[h][0]
            s = ss[h] - ck_ref[h, j]
            if masked:
                s = jnp.where(causal, s, -jnp.inf)
            m_new = jnp.maximum(m, jnp.max(s, axis=-1, keepdims=True) + cqs[h])
            p = jnp.exp2(s - (m_new - cqs[h]))
            alpha = jnp.exp2(m - m_new)
            ps.append(p.astype(BF16))
            ms.append(m_new)
            alphas.append(alpha)
        out = []
        for h in heads:
            v1 = jnp.concatenate([v_ref[pl.ds(start, tk), hs[h]], ones], axis=-1)
            acc = alphas[h] * carry[h][1] + jnp.dot(ps[h], v1, preferred_element_type=F32)
            out.append((ms[h], acc))
        return tuple(out)

    ones = jnp.ones((tk, HEAD_DIM), BF16)
    init = tuple((jnp.full((tq, 1), -jnp.inf, F32), jnp.zeros((tq, 2 * HEAD_DIM), F32)) for _ in heads)
    n_full = (i * tq) // tk
    carry = lax.fori_loop(0, n_full, lambda j, c: step(j, c, False), init)
    for d in range(-(-tq // tk)):
        carry = step(n_full + d, carry, True)
    for h in heads:
        acc = carry[h][1]
        o_ref[:, hs[h]] = (acc[:, :HEAD_DIM] / acc[:, HEAD_DIM:]).astype(o_ref.dtype)


def fox_attention(proj, c):
    B, S, _ = proj.shape
    H = N_FOX_HEADS
    tq, tk = min(FOX_TQ, S), min(FOX_TK, S)
    nq, nk = S // tq, S // tk
    ng = H // FOX_HP
    w = FOX_HP * HEAD_DIM
    c_h = jnp.transpose(c[:, :, :H], (0, 2, 1)) * LOG2E
    c_q = c_h.reshape(B, H, S // LANES, 1, LANES)
    c_row = c_h.reshape(B, H, nk, 1, tk)
    return pl.pallas_call(
        _fox_kernel,
        grid=(B, ng, nq),
        in_specs=[
            pl.BlockSpec((None, tq, w), lambda b, g, i: (b, i, g)),
            pl.BlockSpec((None, S, w), lambda b, g, i: (b, 0, ng + g)),
            pl.BlockSpec((None, S, w), lambda b, g, i: (b, 0, 2 * ng + g)),
            pl.BlockSpec((None, FOX_HP, tq // LANES, 1, LANES), lambda b, g, i: (b, g, i, 0, 0)),
            pl.BlockSpec((None, FOX_HP, nk, 1, tk), lambda b, g, i: (b, g, 0, 0, 0)),
        ],
        out_specs=pl.BlockSpec((None, tq, w), lambda b, g, i: (b, i, g)),
        out_shape=jax.ShapeDtypeStruct((B, S, PRIMARY_WIDTH), BF16),
        compiler_params=_cparams("arbitrary", "arbitrary", "arbitrary"),
        name="fox_attention",
    )(proj, proj, proj, c_q, c_row)


def _mem_attn_kernel(q_ref, kv_ref, o_ref):
    scale = HEAD_DIM ** -0.5
    for h in range(N_MEM_HEADS):
        sl = slice(h * HEAD_DIM, (h + 1) * HEAD_DIM)
        slv = slice(MEM_WIDTH + h * HEAD_DIM, MEM_WIDTH + (h + 1) * HEAD_DIM)
        s = lax.dot_general(q_ref[:, sl], kv_ref[:, sl], (((1,), (1,)), ((), ())),
                            preferred_element_type=F32) * scale
        m = jnp.max(s, axis=-1, keepdims=True)
        p = jnp.exp(s - m)
        l = jnp.sum(p, axis=-1, keepdims=True)
        o = jnp.dot(p.astype(BF16), kv_ref[:, slv], preferred_element_type=F32)
        o_ref[:, sl] = (o / l).astype(o_ref.dtype)


def memory_attention(proj, q_block, kv):
    B, S, _ = proj.shape
    M = kv.shape[1]
    tq = min(MEM_TQ, S)
    return pl.pallas_call(
        _mem_attn_kernel,
        grid=(B, S // tq),
        in_specs=[pl.BlockSpec((None, tq, MEM_WIDTH), lambda b, i: (b, i, q_block)),
                  pl.BlockSpec((None, M, 2 * MEM_WIDTH), lambda b, i: (b, 0, 0))],
        out_specs=pl.BlockSpec((None, tq, MEM_WIDTH), lambda b, i: (b, i, 0)),
        out_shape=jax.ShapeDtypeStruct((B, S, MEM_WIDTH), BF16),
        compiler_params=_cparams("arbitrary", "arbitrary"),
        name="memory_attention",
    )(proj, kv)


def _outproj_ln_kernel(p_ref, m_ref, wt_ref, wb_ref, h_ref, g_ref, b_ref, of_ref, ob_ref, wt_bf, wb_bf):
    @pl.when(pl.program_id(0) == 0)
    def _():
        wt_bf[...] = wt_ref[...].astype(BF16)
        wb_bf[...] = wb_ref[...].astype(BF16)

    mix = (jnp.dot(p_ref[...], wt_bf[...], preferred_element_type=F32)
           + jnp.dot(m_ref[...], wb_bf[...], preferred_element_type=F32))
    y = _ln_rows(DEEPNORM_ALPHA * h_ref[...] + mix, g_ref[...], b_ref[...])
    of_ref[...] = y
    ob_ref[...] = y.astype(BF16)


def outproj_ln(prim, memo, w_out, layer, h, g, b):
    N, D = h.shape
    tm = min(ROW_TILE, N)
    row = lambda i: (i, 0)
    const = lambda i: (0, 0)
    once = pl.Buffered(1)
    return pl.pallas_call(
        _outproj_ln_kernel,
        grid=(N // tm,),
        in_specs=[pl.BlockSpec((tm, PRIMARY_WIDTH), row), pl.BlockSpec((tm, MEM_WIDTH), row),
                  pl.BlockSpec((None, PRIMARY_WIDTH, D), lambda i: (layer, 0, 0), pipeline_mode=once),
                  pl.BlockSpec((None, MEM_WIDTH, D), lambda i: (layer, PRIMARY_WIDTH // MEM_WIDTH, 0),
                               pipeline_mode=once),
                  pl.BlockSpec((tm, D), row), pl.BlockSpec((1, D), const), pl.BlockSpec((1, D), const)],
        out_specs=[pl.BlockSpec((tm, D), row), pl.BlockSpec((tm, D), row)],
        out_shape=[jax.ShapeDtypeStruct((N, D), F32), jax.ShapeDtypeStruct((N, D), BF16)],
        scratch_shapes=[pltpu.VMEM((PRIMARY_WIDTH, D), BF16), pltpu.VMEM((MEM_WIDTH, D), BF16)],
        compiler_params=_cparams("arbitrary"),
        name="outproj_ln",
    )(prim, memo, w_out, w_out, h, g.astype(F32).reshape(1, D), b.astype(F32).reshape(1, D))


def _route_kernel(h_ref, w_ref, b_ref, idx_ref, rank_ref, gate_ref, cnt_ref, carry_ref):
    T = h_ref.shape[0]
    E = w_ref.shape[0]
    gsz = E // N_GROUPS
    neg = -jnp.inf

    @pl.when(pl.program_id(0) == 0)
    def _():
        carry_ref[...] = jnp.zeros_like(carry_ref)

    def split(a):
        hi = a.astype(BF16)
        return hi, (a - hi.astype(F32)).astype(BF16)

    nt = lambda a, b: lax.dot_general(a, b, (((1,), (1,)), ((), ())), preferred_element_type=F32)
    w_hi, w_lo = split(w_ref[...])
    h_hi, h_lo = split(h_ref[...])
    logits = nt(w_hi, h_hi) + (nt(w_hi, h_lo) + nt(w_lo, h_hi))
    scores = jax.nn.sigmoid(logits)
    choice = scores + b_ref[...]

    io8 = lax.broadcasted_iota(jnp.int32, (gsz, T), 0)
    rows = []
    for g in range(N_GROUPS):
        blk = choice[g * gsz:(g + 1) * gsz, :]
        m1 = jnp.max(blk, axis=0, keepdims=True)
        f1 = jnp.min(jnp.where(blk == m1, io8, gsz), axis=0, keepdims=True)
        m2 = jnp.max(jnp.where(io8 == f1, neg, blk), axis=0, keepdims=True)
        rows.append(jnp.broadcast_to(m1 + m2, (gsz, T)))
    gs = jnp.concatenate(rows, axis=0)
    eid = lax.broadcasted_iota(jnp.int32, (E, T), 0)
    gid = eid // gsz
    emask = jnp.zeros((E, T), jnp.bool_)
    for _ in range(TOPK_GROUPS):
        m = jnp.max(gs, axis=0, keepdims=True)
        g = jnp.min(jnp.where(gs == m, gid, N_GROUPS), axis=0, keepdims=True)
        hit = gid == g
        emask = jnp.logical_or(emask, hit)
        gs = jnp.where(hit, neg, gs)

    masked = jnp.where(emask, choice, neg)
    sel = jnp.zeros((E, T), jnp.bool_)
    ids, ws, hits = [], [], []
    for _ in range(TOP_K):
        m = jnp.max(masked, axis=0, keepdims=True)
        i_k = jnp.min(jnp.where(masked == m, eid, E), axis=0, keepdims=True)
        hit = eid == i_k
        ids.append(i_k)
        ws.append(jnp.sum(jnp.where(hit, scores, 0.0), axis=0, keepdims=True))
        hits.append(hit)
        masked = jnp.where(hit, neg, masked)
        sel = jnp.logical_or(sel, hit)
    wsum = ws[0]
    for w in ws[1:]:
        wsum = wsum + w

    sel_f = jnp.where(sel, 1.0, 0.0)
    r = lax.broadcasted_iota(jnp.int32, (T, T), 0)
    c = lax.broadcasted_iota(jnp.int32, (T, T), 1)
    tri = jnp.where(r < c, 1.0, 0.0).astype(BF16)
    before = jnp.dot(sel_f.astype(BF16), tri, preferred_element_type=F32) + carry_ref[...]
    for k in range(TOP_K):
        idx_ref[k:k + 1, :] = ids[k]
        rank_ref[k:k + 1, :] = jnp.sum(jnp.where(hits[k], before, 0.0), axis=0, keepdims=True).astype(jnp.int32)
        gate_ref[k:k + 1, :] = ws[k] / wsum * ROUTED_SCALE
    total = carry_ref[...] + jnp.sum(sel_f, axis=1, keepdims=True)
    carry_ref[...] = total
    cnt_ref[...] = jnp.broadcast_to(total, cnt_ref.shape).astype(jnp.int32)


def route(h, w_r, r_bias):
    N, D = h.shape
    E = w_r.shape[1]
    T = min(ROUTE_T, N)
    kn = lambda i: (0, i)
    idx, rank, gate, cnt = pl.pallas_call(
        _route_kernel,
        grid=(N // T,),
        in_specs=[pl.BlockSpec((T, D), lambda i: (i, 0)), pl.BlockSpec((E, D), lambda i: (0, 0)),
                  pl.BlockSpec((E, 1), lambda i: (0, 0))],
        out_specs=[pl.BlockSpec((TOP_K, T), kn), pl.BlockSpec((TOP_K, T), kn), pl.BlockSpec((TOP_K, T), kn),
                   pl.BlockSpec((E, LANES), lambda i: (0, 0))],
        out_shape=[jax.ShapeDtypeStruct((TOP_K, N), jnp.int32), jax.ShapeDtypeStruct((TOP_K, N), jnp.int32),
                   jax.ShapeDtypeStruct((TOP_K, N), F32), jax.ShapeDtypeStruct((E, LANES), jnp.int32)],
        scratch_shapes=[pltpu.VMEM((E, 1), F32)],
        compiler_params=_cparams("arbitrary"),
        name="route",
    )(h, w_r.astype(F32).T, r_bias.astype(F32).reshape(E, 1))
    return idx, rank, gate, cnt[:, 0]


def _pos_kernel(starts_ref, idx_ref, rank_ref, pos_ref):
    idx = idx_ref[...]
    acc = rank_ref[...]
    for e in range(N_EXPERTS):
        acc = acc + jnp.where(idx == e, starts_ref[e], 0)
    pos_ref[...] = acc


def slot_positions(starts, idx, rank):
    K, N = idx.shape
    T = min(2048, N)
    grid_spec = pltpu.PrefetchScalarGridSpec(
        num_scalar_prefetch=1, grid=(N // T,),
        in_specs=[pl.BlockSpec((K, T), lambda i, s: (0, i)), pl.BlockSpec((K, T), lambda i, s: (0, i))],
        out_specs=pl.BlockSpec((K, T), lambda i, s: (0, i)))
    return pl.pallas_call(
        _pos_kernel, grid_spec=grid_spec, out_shape=jax.ShapeDtypeStruct((K, N), jnp.int32),
        compiler_params=_cparams("arbitrary"), name="slot_positions",
    )(starts.astype(jnp.int32), idx, rank)


def _pack_pair(a, b):
    ua = lax.bitcast_convert_type(a.astype(BF16).astype(F32), jnp.uint32)
    ub = lax.bitcast_convert_type(b.astype(BF16).astype(F32), jnp.uint32)
    return ub | (ua >> 16)


def _unpack_pair(w):
    lo = lax.bitcast_convert_type(w << 16, F32)
    hi = lax.bitcast_convert_type(w & jnp.uint32(0xFFFF0000), F32)
    return lo, hi


def _rows_to_slab(x):
    return [_pack_pair(x[:, 2 * j * LANES:(2 * j + 1) * LANES], x[:, (2 * j + 1) * LANES:(2 * j + 2) * LANES])
            for j in range(SLAB_ROWS)]


def _slab_to_rows(pieces):
    return jnp.concatenate([half for w in pieces for half in _unpack_pair(w)], axis=-1)


def _tiles(a, T):
    K, N = a.shape
    return a.reshape(K, N // T, T).transpose(1, 0, 2)


def _swiglu_bf(x, wu, wd):
    f = wd.shape[0]
    h = jnp.dot(x, wu, preferred_element_type=F32)
    a = h[:, :f]
    act = (a * jax.nn.sigmoid(a)) * h[:, f:]
    return jnp.dot(act.astype(BF16), wd, preferred_element_type=F32)


def _dispatch_kernel(pos_ref, hb_ref, su_ref, sd_ref, sh_ref, xs_hbm, slab, su_bf, sd_bf, sem):
    T = hb_ref.shape[0]

    @pl.when(pl.program_id(0) == 0)
    def _():
        su_bf[...] = su_ref[...].astype(BF16)
        sd_bf[...] = sd_ref[...].astype(BF16)

    hb = hb_ref[...]
    for j, piece in enumerate(_rows_to_slab(hb.astype(F32))):
        slab[pl.ds(j, T, stride=SLAB_ROWS), :] = piece

    for t in range(T):
        src = slab.at[pl.ds(t * SLAB_ROWS, SLAB_ROWS)]
        for k in range(TOP_K):
            dst = xs_hbm.at[pl.ds(pl.multiple_of(pos_ref[k, t] * SLAB_ROWS, SLAB_ROWS), SLAB_ROWS)]
            pltpu.make_async_copy(src, dst, sem).start(priority=k % 2)
    sh_ref[...] = _swiglu_bf(hb, su_bf[...], sd_bf[...]).astype(sh_ref.dtype)
    for k in range(TOP_K):
        pltpu.make_async_copy(slab, xs_hbm.at[pl.ds(0, T * SLAB_ROWS)], sem).wait()


def dispatch_shared(pos_t, h_bf, sh_up, sh_down, layer):
    N, D = h_bf.shape
    nt, K, T = pos_t.shape
    F2 = sh_up.shape[-1]
    Fd = sh_down.shape[1]
    return pl.pallas_call(
        _dispatch_kernel,
        grid=(nt,),
        in_specs=[pl.BlockSpec((None, K, T), lambda i: (i, 0, 0), memory_space=pltpu.SMEM),
                  pl.BlockSpec((T, D), lambda i: (i, 0)),
                  pl.BlockSpec((None, D, F2), lambda i: (layer, 0, 0)),
                  pl.BlockSpec((None, Fd, D), lambda i: (layer, 0, 0))],
        out_specs=[pl.BlockSpec((T, D), lambda i: (i, 0)), pl.BlockSpec(memory_space=pl.ANY)],
        out_shape=[jax.ShapeDtypeStruct((N, D), BF16),
                   jax.ShapeDtypeStruct((N * K * SLAB_ROWS, LANES), jnp.uint32)],
        scratch_shapes=[pltpu.VMEM((T * SLAB_ROWS, LANES), jnp.uint32), pltpu.VMEM((D, F2), BF16),
                        pltpu.VMEM((Fd, D), BF16), pltpu.SemaphoreType.DMA(())],
        compiler_params=_cparams("arbitrary"),
        name="dispatch_shared",
    )(pos_t, h_bf, sh_up, sh_down)


def _expert_kernel(layer, tile_ref, be_ref, lo_ref, hi_ref, first_ref, slot_ref, next_ref, half_ref, x_ref, wu_hbm, wd_hbm,
                   o_ref, wu_f32, wd_f32, wu_bf, wd_bf, sem):
    v = pl.program_id(0)
    tm = x_ref.shape[0] // SLAB_ROWS
    lo = lo_ref[v]
    hi = hi_ref[v]

    def weight_copies(e, slot):
        return (pltpu.make_async_copy(wu_hbm.at[layer, e], wu_f32.at[slot], sem.at[0, slot]),
                pltpu.make_async_copy(wd_hbm.at[layer, e], wd_f32.at[slot], sem.at[1, slot]))

    @pl.when(hi > lo)
    def _():
        changed = jnp.logical_or(v == 0, be_ref[v] != be_ref[jnp.maximum(v - 1, 0)])

        @pl.when(changed)
        def _():
            slot = slot_ref[v]

            @pl.when(v == 0)
            def _():
                for c in weight_copies(be_ref[v], slot):
                    c.start()

            for c in weight_copies(be_ref[v], slot):
                c.wait()

            @pl.when(next_ref[v] >= 0)
            def _():
                for c in weight_copies(next_ref[v], 1 - slot):
                    c.start()

            wu_bf[...] = wu_f32[slot].astype(BF16)
            wd_bf[...] = wd_f32[slot].astype(BF16)

        def ffn_rows(r0, n):
            base = r0 * SLAB_ROWS
            x = _slab_to_rows([x_ref[pl.ds(base + j, n, stride=SLAB_ROWS), :] for j in range(SLAB_ROWS)])
            y = _swiglu_bf(x.astype(BF16), wu_bf[...], wd_bf[...])
            row = r0 + lax.broadcasted_iota(jnp.int32, (n, 1), 0)
            mine = jnp.logical_and(row >= lo, row < hi)
            pieces = _rows_to_slab(y)

            @pl.when(first_ref[v] == 1)
            def _():
                for j in range(SLAB_ROWS):
                    o_ref[pl.ds(base + j, n, stride=SLAB_ROWS), :] = jnp.where(mine, pieces[j], jnp.uint32(0))

            @pl.when(first_ref[v] == 0)
            def _():
                for j in range(SLAB_ROWS):
                    keep = o_ref[pl.ds(base + j, n, stride=SLAB_ROWS), :]
                    o_ref[pl.ds(base + j, n, stride=SLAB_ROWS), :] = jnp.where(mine, pieces[j], keep)

        half = tm // 2
        mode = half_ref[v]

        @pl.when(mode == 0)
        def _():
            ffn_rows(0, tm)

        @pl.when(mode != 0)
        def _():
            r0 = pl.multiple_of((mode - 1) * half, half)

            @pl.when(first_ref[v] == 1)
            def _():
                other = pl.multiple_of((half - r0) * SLAB_ROWS, half * SLAB_ROWS)
                o_ref[pl.ds(other, half * SLAB_ROWS), :] = jnp.zeros((half * SLAB_ROWS, LANES), jnp.uint32)

            ffn_rows(r0, half)


def _lookup(table, idx):
    n = table.shape[0]
    return jnp.sum(jnp.where(idx[:, None] == jnp.arange(n)[None, :], table[None, :], 0), axis=1)


def _visit_tables(counts, tm, A):
    E = counts.shape[0]
    V = A // tm + E
    starts = jnp.cumsum(counts) - counts
    ends = starts + counts
    first_tile = starts // tm
    nvis = jnp.where(counts > 0, (ends - 1) // tm - first_tile + 1, 0)
    vend = jnp.cumsum(nvis)
    voff = vend - nvis
    total = vend[-1]
    v = jnp.arange(V, dtype=jnp.int32)
    e_v = jnp.minimum(jnp.sum(vend[None, :] <= v[:, None], axis=1), E - 1).astype(jnp.int32)
    valid = v < total
    e_last = jnp.sum(jnp.where(v == total - 1, e_v, 0))
    e_v = jnp.where(valid, e_v, e_last)
    tile_v = jnp.where(valid, _lookup(first_tile, e_v) + v - _lookup(voff, e_v), A // tm - 1)
    base = tile_v * tm
    lo_v = jnp.where(valid, jnp.maximum(_lookup(starts, e_v), base) - base, 0)
    hi_v = jnp.where(valid, jnp.minimum(_lookup(ends, e_v), base + tm) - base, 0)
    prev_tile = jnp.concatenate([jnp.full((1,), -1, tile_v.dtype), tile_v[:-1]])
    first_v = jnp.logical_and(valid, tile_v != prev_tile)
    has = nvis > 0
    ids = jnp.arange(E)
    slot_e = (jnp.cumsum(has) - 1) & 1
    later = jnp.logical_and(ids[None, :] > ids[:, None], has[None, :])
    next_e = jnp.min(jnp.where(later, ids[None, :], E), axis=1)
    next_e = jnp.where(next_e == E, -1, next_e)
    i32 = lambda a: a.astype(jnp.int32)
    half_v = jnp.where(hi_v <= tm // 2, 1, jnp.where(lo_v >= tm // 2, 2, 0))
    return starts, (i32(tile_v), i32(e_v), i32(lo_v), i32(hi_v), i32(first_v),
                    i32(_lookup(slot_e, e_v)), i32(_lookup(next_e, e_v)), i32(half_v))


def expert_ffn(xs, tables, w_up, w_down, layer):
    V = tables[0].shape[0]
    D, F2 = w_up.shape[-2:]
    Fd = w_down.shape[-2]
    blk = EXP_TM * SLAB_ROWS
    nt = len(tables)
    tile_map = lambda v, *t: (t[0][v], 0)
    grid_spec = pltpu.PrefetchScalarGridSpec(
        num_scalar_prefetch=nt,
        grid=(V,),
        in_specs=[pl.BlockSpec((blk, LANES), tile_map),
                  pl.BlockSpec(memory_space=pl.ANY), pl.BlockSpec(memory_space=pl.ANY)],
        out_specs=pl.BlockSpec((blk, LANES), tile_map),
        scratch_shapes=[pltpu.VMEM((2, D, F2), F32), pltpu.VMEM((2, Fd, D), F32),
                        pltpu.VMEM((D, F2), BF16), pltpu.VMEM((Fd, D), BF16),
                        pltpu.SemaphoreType.DMA((2, 2))],
    )
    return pl.pallas_call(
        functools.partial(_expert_kernel, layer),
        grid_spec=grid_spec,
        out_shape=jax.ShapeDtypeStruct(xs.shape, jnp.uint32),
        compiler_params=_cparams("arbitrary"),
        name="expert_ffn",
    )(*tables, xs, w_up, w_down)


def _combine_ln_kernel(pos_ref, nxt_ref, gate_ref, h_ref, sh_ref, g_ref, b_ref, y_hbm, of_ref, ob_ref,
                       ybuf, acc_lo, acc_hi, gate_rows, sem):
    T = h_ref.shape[0]
    i = pl.program_id(0)
    slot = i % 2

    last = i == pl.num_programs(0) - 1

    def issue_tile(p_ref, s):
        for t in range(T):
            for k in range(TOP_K):
                src0 = pl.multiple_of(p_ref[k, t] * SLAB_ROWS, SLAB_ROWS)
                pltpu.make_async_copy(y_hbm.at[pl.ds(src0, SLAB_ROWS)],
                                      ybuf.at[s, k, pl.ds(t * SLAB_ROWS, SLAB_ROWS)],
                                      sem.at[s]).start(priority=k % 2)

    def wait_tile(s):
        for k in range(TOP_K):
            pltpu.make_async_copy(y_hbm.at[pl.ds(0, T * SLAB_ROWS)], ybuf.at[s, k], sem.at[s]).wait()

    for k in range(TOP_K):
        for c in range(T // LANES):
            gk = gate_ref[k:k + 1, c * LANES:(c + 1) * LANES]
            gate_rows[k, c * LANES:(c + 1) * LANES, :] = jnp.broadcast_to(gk, (LANES, LANES)).T

    @pl.when(i == 0)
    def _():
        issue_tile(pos_ref, slot)

    wait_tile(slot)
    issue_tile(nxt_ref, 1 - slot)
    for t in range(T):
        r0 = t * SLAB_ROWS
        lo, hi = _unpack_pair(ybuf[slot, 0, pl.ds(r0, SLAB_ROWS), :])
        g = gate_rows[0, t:t + 1, :]
        lo, hi = g * lo, g * hi
        for k in range(1, TOP_K):
            l2, h2 = _unpack_pair(ybuf[slot, k, pl.ds(r0, SLAB_ROWS), :])
            g = gate_rows[k, t:t + 1, :]
            lo, hi = lo + g * l2, hi + g * h2
        acc_lo[pl.ds(r0, SLAB_ROWS), :] = lo
        acc_hi[pl.ds(r0, SLAB_ROWS), :] = hi

    @pl.when(last)
    def _():
        wait_tile(1 - slot)

    routed = jnp.concatenate([acc[pl.ds(j, T, stride=SLAB_ROWS), :] for j in range(SLAB_ROWS)
                              for acc in (acc_lo, acc_hi)], axis=-1)
    y = DEEPNORM_ALPHA * h_ref[...] + routed + sh_ref[...].astype(F32)
    y = _ln_rows(y, g_ref[...], b_ref[...])
    of_ref[...] = y
    ob_ref[...] = y.astype(BF16)


def combine_ln(pos_t, gate_t, h, shared, y, g, b):
    N, D = h.shape
    nt, K, T = pos_t.shape
    row = lambda i: (i, 0)
    const = lambda i: (0, 0)
    return pl.pallas_call(
        _combine_ln_kernel,
        grid=(nt,),
        in_specs=[pl.BlockSpec((None, K, T), lambda i: (i, 0, 0), memory_space=pltpu.SMEM),
                  pl.BlockSpec((None, K, T), lambda i: (jnp.minimum(i + 1, nt - 1), 0, 0), memory_space=pltpu.SMEM),
                  pl.BlockSpec((None, K, T), lambda i: (i, 0, 0)),
                  pl.BlockSpec((T, D), row), pl.BlockSpec((T, D), row),
                  pl.BlockSpec((1, D), const), pl.BlockSpec((1, D), const),
                  pl.BlockSpec(memory_space=pl.ANY)],
        out_specs=[pl.BlockSpec((T, D), row), pl.BlockSpec((T, D), row)],
        out_shape=[jax.ShapeDtypeStruct((N, D), F32), jax.ShapeDtypeStruct((N, D), BF16)],
        scratch_shapes=[pltpu.VMEM((2, K, T * SLAB_ROWS, LANES), jnp.uint32), pltpu.VMEM((T * SLAB_ROWS, LANES), F32),
                        pltpu.VMEM((T * SLAB_ROWS, LANES), F32), pltpu.VMEM((K, T, LANES), F32),
                        pltpu.SemaphoreType.DMA((2,))],
        compiler_params=_cparams("arbitrary"),
        name="combine_ln",
    )(pos_t, pos_t, gate_t, h, shared, g.astype(F32).reshape(1, D), b.astype(F32).reshape(1, D), y)


def moe_ffn_ln(h, h_bf, layer, router_w, router_bias, exp_w_up, exp_w_down, shared_w_up, shared_w_down, g, b):
    N, D = h.shape
    idx, rank, gate, counts = route(h, router_w[layer], router_bias[layer])
    starts, tables = _visit_tables(counts, EXP_TM, N * TOP_K)
    pos = slot_positions(starts, idx, rank)
    shared, xs = dispatch_shared(_tiles(pos, min(DISPATCH_T, N)), h_bf, shared_w_up, shared_w_down, layer)
    y = expert_ffn(xs, tables, exp_w_up, exp_w_down, layer)
    tc = min(COMBINE_T, N)
    return combine_ln(_tiles(pos, tc), _tiles(gate, tc), h, shared, y, g, b)


def kernel(x, mem, conv_w_in, conv_dw, conv_dw_b, conv_ln_g, conv_ln_b, fox_w_in, fox_b_f, mem_w_kv, w_out,
           ln_g, ln_b, router_w, router_bias, exp_w_up, exp_w_down, shared_w_up, shared_w_down):
    B, S, D = x.shape
    N = B * S
    h = x.reshape(N, D).astype(F32)
    h_bf = h.astype(BF16)
    mem_bf = mem.reshape(B * MEM_LEN, D).astype(BF16)
    nf = 3 * PRIMARY_WIDTH
    for i in range(DEPTH):
        j = i // N_MIXERS
        if i % N_MIXERS == 0:
            proj = matmul(h_bf, conv_w_in, j, 2 * CONV_CH + MEM_WIDTH, BF16).reshape(B, S, -1)
            prim = conv_mixer(proj, conv_dw[j], conv_dw_b[j], conv_ln_g[j], conv_ln_b[j])
            q_src, q_block = proj, 2 * CONV_CH // MEM_WIDTH
        else:
            proj = matmul(h_bf, fox_w_in, j, nf, BF16).reshape(B, S, nf)
            tail = trailing_columns(fox_w_in, j, nf)
            w_qm = tail[None, :, N_FOX_HEADS:N_FOX_HEADS + MEM_WIDTH]
            q_src, q_block = matmul(h_bf, w_qm, 0, MEM_WIDTH, BF16).reshape(B, S, MEM_WIDTH), 0
            c = forget_cumlog(h_bf.reshape(B, S, D), tail[:, :N_FOX_HEADS], fox_b_f[j])
            prim = fox_attention(proj, c)
        kv = matmul(mem_bf, mem_w_kv, i, 2 * MEM_WIDTH, BF16).reshape(B, MEM_LEN, 2 * MEM_WIDTH)
        memo = memory_attention(q_src, q_block, kv)
        h, h_bf = outproj_ln(prim.reshape(N, PRIMARY_WIDTH), memo.reshape(N, MEM_WIDTH), w_out, i, h,
                             ln_g[i, 0], ln_b[i, 0])
        h, h_bf = moe_ffn_ln(h, h_bf, i, router_w, router_bias, exp_w_up, exp_w_down,
                             shared_w_up, shared_w_down, ln_g[i, 1], ln_b[i, 1])
    return h.reshape(B, S, D)
```

```python
import functools

import jax
import jax.numpy as jnp
from jax import lax
from jax.experimental import pallas as pl
from jax.experimental.pallas import tpu as pltpu

F32 = jnp.float32
BF16 = jnp.bfloat16

D_MODEL = 2048
DEPTH = 4
N_MIXERS = 2
MEM_LEN = 256
HEAD_DIM = 128
N_MEM_HEADS = 4
MEM_WIDTH = N_MEM_HEADS * HEAD_DIM
PRIMARY_WIDTH = D_MODEL - MEM_WIDTH
CONV_CH = PRIMARY_WIDTH
CONV_WIDTH = 31
N_FOX_HEADS = PRIMARY_WIDTH // HEAD_DIM
N_EXPERTS = 64
TOP_K = 8
N_GROUPS = 8
TOPK_GROUPS = 4
D_EXPERT = 512
ROUTED_SCALE = 2.5
LN_EPS = 1e-5
DEEPNORM_ALPHA = (2 * DEPTH) ** 0.25
LOG2E = 1.4426950408889634

LANES = 128
VMEM_LIMIT = 56 * 1024 * 1024
MM_TM, MM_TN = 1024, 512
ROW_TILE = 256
CONV_TS = 256
CONV_HALO = 32
CONV_SUB = 64
FOX_TQ = 2048
FOX_TK = 1024
FOX_HP = 2
MEM_TQ = 512
EXP_TM = 512
EXP_SPLIT_LEVELS = 2
SLAB_ROWS = D_MODEL // LANES // 2
ROUTE_T = 256
DISPATCH_T = 256
COMBINE_T = 128


def _cparams(*sem):
    return pltpu.CompilerParams(dimension_semantics=sem, vmem_limit_bytes=VMEM_LIMIT)


def _ln_rows(y, g, b):
    mu = jnp.mean(y, axis=-1, keepdims=True)
    d = y - mu
    var = jnp.mean(d * d, axis=-1, keepdims=True)
    return d * lax.rsqrt(var + LN_EPS) * g + b


def _mm_kernel(a_ref, w_ref, o_ref, w_bf):
    @pl.when(pl.program_id(1) == 0)
    def _():
        w_bf[...] = w_ref[...].astype(BF16)

    o_ref[...] = jnp.dot(a_ref[...], w_bf[...], preferred_element_type=F32).astype(o_ref.dtype)


def matmul(a, w, layer, n_cols, out_dtype, tm=MM_TM, tn=MM_TN):
    M, K = a.shape
    tm, tn = min(tm, M), min(tn, n_cols)
    assert M % tm == 0 and n_cols % tn == 0 and w.shape[1] == K
    return pl.pallas_call(
        _mm_kernel,
        grid=(n_cols // tn, M // tm),
        in_specs=[pl.BlockSpec((tm, K), lambda j, i: (i, 0)),
                  pl.BlockSpec((None, K, tn), lambda j, i: (layer, 0, j))],
        out_specs=pl.BlockSpec((tm, tn), lambda j, i: (i, j)),
        out_shape=jax.ShapeDtypeStruct((M, n_cols), out_dtype),
        scratch_shapes=[pltpu.VMEM((K, tn), BF16)],
        compiler_params=_cparams("arbitrary", "arbitrary"),
        name="matmul",
    )(a, w)


def _cols_kernel(c0, w_ref, o_ref):
    n = w_ref.shape[1] - c0
    o_ref[:, :n] = w_ref[:, c0:]
    o_ref[:, n:] = jnp.zeros((o_ref.shape[0], o_ref.shape[1] - n), o_ref.dtype)


def trailing_columns(w, layer, c0):
    _, K, C = w.shape
    n_pad = -(-(C - c0) // LANES) * LANES
    tk = min(256, K)
    return pl.pallas_call(
        functools.partial(_cols_kernel, c0),
        grid=(K // tk,),
        in_specs=[pl.BlockSpec((None, tk, C), lambda i: (layer, i, 0))],
        out_specs=pl.BlockSpec((tk, n_pad), lambda i: (i, 0)),
        out_shape=jax.ShapeDtypeStruct((K, n_pad), w.dtype),
        compiler_params=_cparams("arbitrary"),
        name="trailing_columns",
    )(w)


def _conv_kernel(a_ref, g_ref, ah_ref, gh_ref, dw_ref, dwb_ref, lg_ref, lb_ref, o_ref, u_scr, y_scr):
    ts = a_ref.shape[0]
    nchunk = CONV_CH // LANES
    first = pl.program_id(1) == 0
    uh = ah_ref[...].astype(F32) * jax.nn.sigmoid(gh_ref[...].astype(F32))
    uh = jnp.where(first, 0.0, uh)
    u = a_ref[...].astype(F32) * jax.nn.sigmoid(g_ref[...].astype(F32))
    for c in range(nchunk):
        sl = slice(c * LANES, (c + 1) * LANES)
        u_scr[c, 0:CONV_HALO, :] = uh[:, sl]
        u_scr[c, CONV_HALO:CONV_HALO + ts, :] = u[:, sl]

    off = CONV_HALO - (CONV_WIDTH - 1)

    def chunk_body(c, carry):
        w = dw_ref[c]
        bias = dwb_ref[c]
        for r in range(ts // CONV_SUB):
            acc = jnp.broadcast_to(bias, (CONV_SUB, LANES))
            for j in range(CONV_WIDTH):
                acc = acc + w[j:j + 1, :] * u_scr[c, pl.ds(off + j + r * CONV_SUB, CONV_SUB), :]
            y_scr[c, r * CONV_SUB:(r + 1) * CONV_SUB, :] = acc
        return carry

    lax.fori_loop(0, nchunk, chunk_body, 0)

    s1 = jnp.zeros((ts, 1), F32)
    for c in range(nchunk):
        s1 = s1 + jnp.sum(y_scr[c], axis=-1, keepdims=True)
    mu = s1 * (1.0 / CONV_CH)
    s2 = jnp.zeros((ts, 1), F32)
    for c in range(nchunk):
        d = y_scr[c] - mu
        s2 = s2 + jnp.sum(d * d, axis=-1, keepdims=True)
    rstd = lax.rsqrt(s2 * (1.0 / CONV_CH) + LN_EPS)
    for c in range(nchunk):
        sl = slice(c * LANES, (c + 1) * LANES)
        z = (y_scr[c] - mu) * rstd * lg_ref[:, sl] + lb_ref[:, sl]
        o_ref[:, sl] = (z * jax.nn.sigmoid(z)).astype(o_ref.dtype)


def conv_mixer(proj, dw, dw_b, ln_g, ln_b):
    B, S, _ = proj.shape
    ts = min(CONV_TS, S)
    nchunk = CONV_CH // LANES
    hb = ts // CONV_HALO
    dw_p = jnp.zeros((CONV_HALO, CONV_CH), F32).at[:CONV_WIDTH].set(dw.astype(F32))
    dw_c = dw_p.reshape(CONV_HALO, nchunk, LANES).transpose(1, 0, 2)
    dwb_c = dw_b.astype(F32).reshape(nchunk, 1, LANES)
    halo_idx = lambda b, i: (b, jnp.maximum(i * hb - 1, 0), 0)
    halo_idx_g = lambda b, i: (b, jnp.maximum(i * hb - 1, 0), 1)
    return pl.pallas_call(
        _conv_kernel,
        grid=(B, S // ts),
        in_specs=[
            pl.BlockSpec((None, ts, CONV_CH), lambda b, i: (b, i, 0)),
            pl.BlockSpec((None, ts, CONV_CH), lambda b, i: (b, i, 1)),
            pl.BlockSpec((None, CONV_HALO, CONV_CH), halo_idx),
            pl.BlockSpec((None, CONV_HALO, CONV_CH), halo_idx_g),
            pl.BlockSpec((nchunk, CONV_HALO, LANES), lambda b, i: (0, 0, 0)),
            pl.BlockSpec((nchunk, 1, LANES), lambda b, i: (0, 0, 0)),
            pl.BlockSpec((1, CONV_CH), lambda b, i: (0, 0)),
            pl.BlockSpec((1, CONV_CH), lambda b, i: (0, 0)),
        ],
        out_specs=pl.BlockSpec((None, ts, CONV_CH), lambda b, i: (b, i, 0)),
        out_shape=jax.ShapeDtypeStruct((B, S, CONV_CH), BF16),
        scratch_shapes=[pltpu.VMEM((nchunk, CONV_HALO + ts, LANES), F32),
                        pltpu.VMEM((nchunk, ts, LANES), F32)],
        compiler_params=_cparams("arbitrary", "arbitrary"),
        name="conv_mixer",
    )(proj, proj, proj, proj, dw_c, dwb_c, ln_g.astype(F32).reshape(1, -1), ln_b.astype(F32).reshape(1, -1))


def _fgate_kernel(h_ref, w_ref, b_ref, c_ref, carry_ref):
    ts = h_ref.shape[0]

    @pl.when(pl.program_id(1) == 0)
    def _():
        carry_ref[...] = jnp.zeros_like(carry_ref)

    f = jnp.dot(h_ref[...], w_ref[...], preferred_element_type=F32) + b_ref[...]
    ls = jax.nn.log_sigmoid(f)
    row = lax.broadcasted_iota(jnp.int32, (ts, ts), 0)
    col = lax.broadcasted_iota(jnp.int32, (ts, ts), 1)
    tri = jnp.where(row >= col, 1.0, 0.0).astype(BF16)
    hi = ls.astype(BF16)
    r1 = ls - hi.astype(F32)
    mid = r1.astype(BF16)
    lo = (r1 - mid.astype(F32)).astype(BF16)
    cs = (jnp.dot(tri, hi, preferred_element_type=F32)
          + jnp.dot(tri, mid, preferred_element_type=F32)
          + jnp.dot(tri, lo, preferred_element_type=F32))
    c = cs + carry_ref[...]
    c_ref[...] = c
    carry_ref[...] = c[ts - 1:ts, :]


def forget_cumlog(h_bf, w_f, b_f):
    B, S, D = h_bf.shape
    H = w_f.shape[1]
    ts = min(256, S)
    w_p = jnp.zeros((D, LANES), BF16).at[:, :H].set(w_f.astype(BF16))
    b_p = jnp.zeros((1, LANES), F32).at[0, :H].set(b_f.astype(F32))
    return pl.pallas_call(
        _fgate_kernel,
        grid=(B, S // ts),
        in_specs=[pl.BlockSpec((None, ts, D), lambda b, i: (b, i, 0)),
                  pl.BlockSpec((D, LANES), lambda b, i: (0, 0)),
                  pl.BlockSpec((1, LANES), lambda b, i: (0, 0))],
        out_specs=pl.BlockSpec((None, ts, LANES), lambda b, i: (b, i, 0)),
        out_shape=jax.ShapeDtypeStruct((B, S, LANES), F32),
        scratch_shapes=[pltpu.VMEM((1, LANES), F32)],
        compiler_params=_cparams("arbitrary", "arbitrary"),
        name="forget_cumlog",
    )(h_bf, w_p, b_p)


def _fox_kernel(q_ref, k_ref, v_ref, cq_ref, ck_ref, o_ref):
    tq = q_ref.shape[0]
    tk = ck_ref.shape[-1]
    i = pl.program_id(2)
    qscale = HEAD_DIM ** -0.5 * LOG2E
    heads = range(FOX_HP)
    hs = [slice(h * HEAD_DIM, (h + 1) * HEAD_DIM) for h in heads]
    qs = [(q_ref[:, hs[h]].astype(F32) * qscale).astype(BF16) for h in heads]
    cqs = [jnp.concatenate([jnp.broadcast_to(cq_ref[h, r], (LANES, LANES)).T[:, 0:1] for r in range(tq // LANES)],
                           axis=0) for h in heads]

    def step(j, carry, masked):
        start = pl.multiple_of(j * tk, tk)
        ss = []
        for h in heads:
            k = k_ref[pl.ds(start, tk), hs[h]]
            ss.append(lax.dot_general(qs[h], k, (((1,), (1,)), ((), ())), preferred_element_type=F32))
        if masked:
            row = i * tq + lax.broadcasted_iota(jnp.int32, (tq, tk), 0)
            col = j * tk + lax.broadcasted_iota(jnp.int32, (tq, tk), 1)
            causal = row >= col
        ps, ms, alphas = [], [], []
        for h in heads:
            m = carry[h][0]
            s = ss[h] - ck_ref[h, j]
            if masked:
                s = jnp.where(causal, s, -jnp.inf)
            m_new = jnp.maximum(m, jnp.max(s, axis=-1, keepdims=True) + cqs[h])
            p = jnp.exp2(s - (m_new - cqs[h]))
            alpha = jnp.exp2(m - m_new)
            ps.append(p.astype(BF16))
            ms.append(m_new)
            alphas.append(alpha)
        out = []
        for h in heads:
            v1 = jnp.concatenate([v_ref[pl.ds(start, tk), hs[h]], ones], axis=-1)
            acc = alphas[h] * carry[h][1] + jnp.dot(ps[h], v1, preferred_element_type=F32)
            out.append((ms[h], acc))
        return tuple(out)

    ones = jnp.ones((tk, HEAD_DIM), BF16)
    init = tuple((jnp.full((tq, 1), -jnp.inf, F32), jnp.zeros((tq, 2 * HEAD_DIM), F32)) for _ in heads)
    n_full = (i * tq) // tk
    carry = lax.fori_loop(0, n_full, lambda j, c: step(j, c, False), init)
    for d in range(-(-tq // tk)):
        carry = step(n_full + d, carry, True)
    for h in heads:
        acc = carry[h][1]
        o_ref[:, hs[h]] = (acc[:, :HEAD_DIM] / acc[:, HEAD_DIM:]).astype(o_ref.dtype)


def fox_attention(proj, c):
    B, S, _ = proj.shape
    H = N_FOX_HEADS
    tq, tk = min(FOX_TQ, S), min(FOX_TK, S)
    nq, nk = S // tq, S // tk
    ng = H // FOX_HP
    w = FOX_HP * HEAD_DIM
    c_h = jnp.transpose(c[:, :, :H], (0, 2, 1)) * LOG2E
    c_q = c_h.reshape(B, H, S // LANES, 1, LANES)
    c_row = c_h.reshape(B, H, nk, 1, tk)
    return pl.pallas_call(
        _fox_kernel,
        grid=(B, ng, nq),
        in_specs=[
            pl.BlockSpec((None, tq, w), lambda b, g, i: (b, i, g)),
            pl.BlockSpec((None, S, w), lambda b, g, i: (b, 0, ng + g)),
            pl.BlockSpec((None, S, w), lambda b, g, i: (b, 0, 2 * ng + g)),
            pl.BlockSpec((None, FOX_HP, tq // LANES, 1, LANES), lambda b, g, i: (b, g, i, 0, 0)),
            pl.BlockSpec((None, FOX_HP, nk, 1, tk), lambda b, g, i: (b, g, 0, 0, 0)),
        ],
        out_specs=pl.BlockSpec((None, tq, w), lambda b, g, i: (b, i, g)),
        out_shape=jax.ShapeDtypeStruct((B, S, PRIMARY_WIDTH), BF16),
        compiler_params=_cparams("arbitrary", "arbitrary", "arbitrary"),
        name="fox_attention",
    )(proj, proj, proj, c_q, c_row)


def _mem_attn_kernel(q_ref, kv_ref, o_ref):
    scale = HEAD_DIM ** -0.5
    for h in range(N_MEM_HEADS):
        sl = slice(h * HEAD_DIM, (h + 1) * HEAD_DIM)
        slv = slice(MEM_WIDTH + h * HEAD_DIM, MEM_WIDTH + (h + 1) * HEAD_DIM)
        s = lax.dot_general(q_ref[:, sl], kv_ref[:, sl], (((1,), (1,)), ((), ())),
                            preferred_element_type=F32) * scale
        m = jnp.max(s, axis=-1, keepdims=True)
        p = jnp.exp(s - m)
        l = jnp.sum(p, axis=-1, keepdims=True)
        o = jnp.dot(p.astype(BF16), kv_ref[:, slv], preferred_element_type=F32)
        o_ref[:, sl] = (o / l).astype(o_ref.dtype)


def memory_attention(proj, q_block, kv):
    B, S, _ = proj.shape
    M = kv.shape[1]
    tq = min(MEM_TQ, S)
    return pl.pallas_call(
        _mem_attn_kernel,
        grid=(B, S // tq),
        in_specs=[pl.BlockSpec((None, tq, MEM_WIDTH), lambda b, i: (b, i, q_block)),
                  pl.BlockSpec((None, M, 2 * MEM_WIDTH), lambda b, i: (b, 0, 0))],
        out_specs=pl.BlockSpec((None, tq, MEM_WIDTH), lambda b, i: (b, i, 0)),
        out_shape=jax.ShapeDtypeStruct((B, S, MEM_WIDTH), BF16),
        compiler_params=_cparams("arbitrary", "arbitrary"),
        name="memory_attention",
    )(proj, kv)


def _outproj_ln_kernel(p_ref, m_ref, wt_ref, wb_ref, h_ref, g_ref, b_ref, of_ref, ob_ref, wt_bf, wb_bf):
    @pl.when(pl.program_id(0) == 0)
    def _():
        wt_bf[...] = wt_ref[...].astype(BF16)
        wb_bf[...] = wb_ref[...].astype(BF16)

    mix = (jnp.dot(p_ref[...], wt_bf[...], preferred_element_type=F32)
           + jnp.dot(m_ref[...], wb_bf[...], preferred_element_type=F32))
    y = _ln_rows(DEEPNORM_ALPHA * h_ref[...] + mix, g_ref[...], b_ref[...])
    of_ref[...] = y
    ob_ref[...] = y.astype(BF16)


def outproj_ln(prim, memo, w_out, layer, h, g, b):
    N, D = h.shape
    tm = min(ROW_TILE, N)
    row = lambda i: (i, 0)
    const = lambda i: (0, 0)
    once = pl.Buffered(1)
    return pl.pallas_call(
        _outproj_ln_kernel,
        grid=(N // tm,),
        in_specs=[pl.BlockSpec((tm, PRIMARY_WIDTH), row), pl.BlockSpec((tm, MEM_WIDTH), row),
                  pl.BlockSpec((None, PRIMARY_WIDTH, D), lambda i: (layer, 0, 0), pipeline_mode=once),
                  pl.BlockSpec((None, MEM_WIDTH, D), lambda i: (layer, PRIMARY_WIDTH // MEM_WIDTH, 0),
                               pipeline_mode=once),
                  pl.BlockSpec((tm, D), row), pl.BlockSpec((1, D), const), pl.BlockSpec((1, D), const)],
        out_specs=[pl.BlockSpec((tm, D), row), pl.BlockSpec((tm, D), row)],
        out_shape=[jax.ShapeDtypeStruct((N, D), F32), jax.ShapeDtypeStruct((N, D), BF16)],
        scratch_shapes=[pltpu.VMEM((PRIMARY_WIDTH, D), BF16), pltpu.VMEM((MEM_WIDTH, D), BF16)],
        compiler_params=_cparams("arbitrary"),
        name="outproj_ln",
    )(prim, memo, w_out, w_out, h, g.astype(F32).reshape(1, D), b.astype(F32).reshape(1, D))


def _route_kernel(h_ref, w_ref, b_ref, idx_ref, rank_ref, gate_ref, cnt_ref, carry_ref):
    T = h_ref.shape[0]
    E = w_ref.shape[0]
    gsz = E // N_GROUPS
    neg = -jnp.inf

    @pl.when(pl.program_id(0) == 0)
    def _():
        carry_ref[...] = jnp.zeros_like(carry_ref)

    def split(a):
        hi = a.astype(BF16)
        return hi, (a - hi.astype(F32)).astype(BF16)

    nt = lambda a, b: lax.dot_general(a, b, (((1,), (1,)), ((), ())), preferred_element_type=F32)
    w_hi, w_lo = split(w_ref[...])
    h_hi, h_lo = split(h_ref[...])
    logits = nt(w_hi, h_hi) + (nt(w_hi, h_lo) + nt(w_lo, h_hi))
    scores = jax.nn.sigmoid(logits)
    choice = scores + b_ref[...]

    io8 = lax.broadcasted_iota(jnp.int32, (gsz, T), 0)
    rows = []
    for g in range(N_GROUPS):
        blk = choice[g * gsz:(g + 1) * gsz, :]
        m1 = jnp.max(blk, axis=0, keepdims=True)
        f1 = jnp.min(jnp.where(blk == m1, io8, gsz), axis=0, keepdims=True)
        m2 = jnp.max(jnp.where(io8 == f1, neg, blk), axis=0, keepdims=True)
        rows.append(jnp.broadcast_to(m1 + m2, (gsz, T)))
    gs = jnp.concatenate(rows, axis=0)
    eid = lax.broadcasted_iota(jnp.int32, (E, T), 0)
    gid = eid // gsz
    emask = jnp.zeros((E, T), jnp.bool_)
    for _ in range(TOPK_GROUPS):
        m = jnp.max(gs, axis=0, keepdims=True)
        g = jnp.min(jnp.where(gs == m, gid, N_GROUPS), axis=0, keepdims=True)
        hit = gid == g
        emask = jnp.logical_or(emask, hit)
        gs = jnp.where(hit, neg, gs)

    masked = jnp.where(emask, choice, neg)
    sel = jnp.zeros((E, T), jnp.bool_)
    ids, ws, hits = [], [], []
    for _ in range(TOP_K):
        m = jnp.max(masked, axis=0, keepdims=True)
        i_k = jnp.min(jnp.where(masked == m, eid, E), axis=0, keepdims=True)
        hit = eid == i_k
        ids.append(i_k)
        ws.append(jnp.sum(jnp.where(hit, scores, 0.0), axis=0, keepdims=True))
        hits.append(hit)
        masked = jnp.where(hit, neg, masked)
        sel = jnp.logical_or(sel, hit)
    wsum = ws[0]
    for w in ws[1:]:
        wsum = wsum + w

    sel_f = jnp.where(sel, 1.0, 0.0)
    r = lax.broadcasted_iota(jnp.int32, (T, T), 0)
    c = lax.broadcasted_iota(jnp.int32, (T, T), 1)
    tri = jnp.where(r < c, 1.0, 0.0).astype(BF16)
    before = jnp.dot(sel_f.astype(BF16), tri, preferred_element_type=F32) + carry_ref[...]
    for k in range(TOP_K):
        idx_ref[k:k + 1, :] = ids[k]
        rank_ref[k:k + 1, :] = jnp.sum(jnp.where(hits[k], before, 0.0), axis=0, keepdims=True).astype(jnp.int32)
        gate_ref[k:k + 1, :] = ws[k] / wsum * ROUTED_SCALE
    total = carry_ref[...] + jnp.sum(sel_f, axis=1, keepdims=True)
    carry_ref[...] = total
    cnt_ref[...] = jnp.broadcast_to(total, cnt_ref.shape).astype(jnp.int32)


def route(h, w_r, r_bias):
    N, D = h.shape
    E = w_r.shape[1]
    T = min(ROUTE_T, N)
    kn = lambda i: (0, i)
    idx, rank, gate, cnt = pl.pallas_call(
        _route_kernel,
        grid=(N // T,),
        in_specs=[pl.BlockSpec((T, D), lambda i: (i, 0)), pl.BlockSpec((E, D), lambda i: (0, 0)),
                  pl.BlockSpec((E, 1), lambda i: (0, 0))],
        out_specs=[pl.BlockSpec((TOP_K, T), kn), pl.BlockSpec((TOP_K, T), kn), pl.BlockSpec((TOP_K, T), kn),
                   pl.BlockSpec((E, LANES), lambda i: (0, 0))],
        out_shape=[jax.ShapeDtypeStruct((TOP_K, N), jnp.int32), jax.ShapeDtypeStruct((TOP_K, N), jnp.int32),
                   jax.ShapeDtypeStruct((TOP_K, N), F32), jax.ShapeDtypeStruct((E, LANES), jnp.int32)],
        scratch_shapes=[pltpu.VMEM((E, 1), F32)],
        compiler_params=_cparams("arbitrary"),
        name="route",
    )(h, w_r.astype(F32).T, r_bias.astype(F32).reshape(E, 1))
    return idx, rank, gate, cnt[:, 0]


def _pos_kernel(starts_ref, idx_ref, rank_ref, pos_ref):
    idx = idx_ref[...]
    acc = rank_ref[...]
    for e in range(N_EXPERTS):
        acc = acc + jnp.where(idx == e, starts_ref[e], 0)
    pos_ref[...] = acc


def slot_positions(starts, idx, rank):
    K, N = idx.shape
    T = min(2048, N)
    grid_spec = pltpu.PrefetchScalarGridSpec(
        num_scalar_prefetch=1, grid=(N // T,),
        in_specs=[pl.BlockSpec((K, T), lambda i, s: (0, i)), pl.BlockSpec((K, T), lambda i, s: (0, i))],
        out_specs=pl.BlockSpec((K, T), lambda i, s: (0, i)))
    return pl.pallas_call(
        _pos_kernel, grid_spec=grid_spec, out_shape=jax.ShapeDtypeStruct((K, N), jnp.int32),
        compiler_params=_cparams("arbitrary"), name="slot_positions",
    )(starts.astype(jnp.int32), idx, rank)


def _pack_pair(a, b):
    ua = lax.bitcast_convert_type(a.astype(BF16).astype(F32), jnp.uint32)
    ub = lax.bitcast_convert_type(b.astype(BF16).astype(F32), jnp.uint32)
    return ub | (ua >> 16)


def _unpack_pair(w):
    lo = lax.bitcast_convert_type(w << 16, F32)
    hi = lax.bitcast_convert_type(w & jnp.uint32(0xFFFF0000), F32)
    return lo, hi


def _rows_to_slab(x):
    return [_pack_pair(x[:, 2 * j * LANES:(2 * j + 1) * LANES], x[:, (2 * j + 1) * LANES:(2 * j + 2) * LANES])
            for j in range(SLAB_ROWS)]


def _slab_to_rows(pieces):
    return jnp.concatenate([half for w in pieces for half in _unpack_pair(w)], axis=-1)


def _tiles(a, T):
    K, N = a.shape
    return a.reshape(K, N // T, T).transpose(1, 0, 2)


def _swiglu_bf(x, wu, wd):
    f = wd.shape[0]
    h = jnp.dot(x, wu, preferred_element_type=F32)
    a = h[:, :f]
    act = (a * jax.nn.sigmoid(a)) * h[:, f:]
    return jnp.dot(act.astype(BF16), wd, preferred_element_type=F32)


def _dispatch_kernel(pos_ref, hb_ref, su_ref, sd_ref, sh_ref, xs_hbm, slab, su_bf, sd_bf, sem):
    T = hb_ref.shape[0]

    @pl.when(pl.program_id(0) == 0)
    def _():
        su_bf[...] = su_ref[...].astype(BF16)
        sd_bf[...] = sd_ref[...].astype(BF16)

    hb = hb_ref[...]
    for j, piece in enumerate(_rows_to_slab(hb.astype(F32))):
        slab[pl.ds(j, T, stride=SLAB_ROWS), :] = piece

    for t in range(T):
        src = slab.at[pl.ds(t * SLAB_ROWS, SLAB_ROWS)]
        for k in range(TOP_K):
            dst = xs_hbm.at[pl.ds(pl.multiple_of(pos_ref[k, t] * SLAB_ROWS, SLAB_ROWS), SLAB_ROWS)]
            pltpu.make_async_copy(src, dst, sem).start(priority=k % 2)
    sh_ref[...] = _swiglu_bf(hb, su_bf[...], sd_bf[...]).astype(sh_ref.dtype)
    for k in range(TOP_K):
        pltpu.make_async_copy(slab, xs_hbm.at[pl.ds(0, T * SLAB_ROWS)], sem).wait()


def dispatch_shared(pos_t, h_bf, sh_up, sh_down, layer):
    N, D = h_bf.shape
    nt, K, T = pos_t.shape
    F2 = sh_up.shape[-1]
    Fd = sh_down.shape[1]
    return pl.pallas_call(
        _dispatch_kernel,
        grid=(nt,),
        in_specs=[pl.BlockSpec((None, K, T), lambda i: (i, 0, 0), memory_space=pltpu.SMEM),
                  pl.BlockSpec((T, D), lambda i: (i, 0)),
                  pl.BlockSpec((None, D, F2), lambda i: (layer, 0, 0)),
                  pl.BlockSpec((None, Fd, D), lambda i: (layer, 0, 0))],
        out_specs=[pl.BlockSpec((T, D), lambda i: (i, 0)), pl.BlockSpec(memory_space=pl.ANY)],
        out_shape=[jax.ShapeDtypeStruct((N, D), BF16),
                   jax.ShapeDtypeStruct((N * K * SLAB_ROWS, LANES), jnp.uint32)],
        scratch_shapes=[pltpu.VMEM((T * SLAB_ROWS, LANES), jnp.uint32), pltpu.VMEM((D, F2), BF16),
                        pltpu.VMEM((Fd, D), BF16), pltpu.SemaphoreType.DMA(())],
        compiler_params=_cparams("arbitrary"),
        name="dispatch_shared",
    )(pos_t, h_bf, sh_up, sh_down)


def _expert_kernel(layer, tile_ref, be_ref, lo_ref, hi_ref, first_ref, slot_ref, next_ref, unit_ref, x_ref, wu_hbm, wd_hbm,
                   o_ref, wu_f32, wd_f32, wu_bf, wd_bf, sem):
    v = pl.program_id(0)
    tm = x_ref.shape[0] // SLAB_ROWS
    lo = lo_ref[v]
    hi = hi_ref[v]

    def weight_copies(e, slot):
        return (pltpu.make_async_copy(wu_hbm.at[layer, e], wu_f32.at[slot], sem.at[0, slot]),
                pltpu.make_async_copy(wd_hbm.at[layer, e], wd_f32.at[slot], sem.at[1, slot]))

    @pl.when(hi > lo)
    def _():
        changed = jnp.logical_or(v == 0, be_ref[v] != be_ref[jnp.maximum(v - 1, 0)])

        @pl.when(changed)
        def _():
            slot = slot_ref[v]

            @pl.when(v == 0)
            def _():
                for c in weight_copies(be_ref[v], slot):
                    c.start()

            for c in weight_copies(be_ref[v], slot):
                c.wait()

            @pl.when(next_ref[v] >= 0)
            def _():
                for c in weight_copies(next_ref[v], 1 - slot):
                    c.start()

            wu_bf[...] = wu_f32[slot].astype(BF16)
            wd_bf[...] = wd_f32[slot].astype(BF16)

        def ffn_rows(r0, n, may_be_first):
            base = r0 * SLAB_ROWS
            x = _slab_to_rows([x_ref[pl.ds(base + j, n, stride=SLAB_ROWS), :] for j in range(SLAB_ROWS)])
            y = _swiglu_bf(x.astype(BF16), wu_bf[...], wd_bf[...])
            row = r0 + lax.broadcasted_iota(jnp.int32, (n, 1), 0)
            mine = jnp.logical_and(row >= lo, row < hi)
            pieces = _rows_to_slab(y)

            def merge():
                for j in range(SLAB_ROWS):
                    keep = o_ref[pl.ds(base + j, n, stride=SLAB_ROWS), :]
                    o_ref[pl.ds(base + j, n, stride=SLAB_ROWS), :] = jnp.where(mine, pieces[j], keep)

            if not may_be_first:
                merge()
                return

            @pl.when(first_ref[v] == 1)
            def _():
                for j in range(SLAB_ROWS):
                    o_ref[pl.ds(base + j, n, stride=SLAB_ROWS), :] = jnp.where(mine, pieces[j], jnp.uint32(0))

            pl.when(first_ref[v] == 0)(merge)

        unit = unit_ref[v]

        @pl.when(unit == 0)
        def _():
            ffn_rows(0, tm, True)

        @pl.when(jnp.logical_and(unit != 0, first_ref[v] == 1))
        def _():
            o_ref[...] = jnp.zeros_like(o_ref)

        code = 1
        for level in range(1, EXP_SPLIT_LEVELS + 1):
            n = tm >> level
            for part in range(1 << level):
                pl.when(unit == code)(functools.partial(ffn_rows, part * n, n, False))
                code += 1


def _lookup(table, idx):
    n = table.shape[0]
    return jnp.sum(jnp.where(idx[:, None] == jnp.arange(n)[None, :], table[None, :], 0), axis=1)


def _visit_tables(counts, tm, A):
    E = counts.shape[0]
    V = A // tm + E
    starts = jnp.cumsum(counts) - counts
    ends = starts + counts
    first_tile = starts // tm
    nvis = jnp.where(counts > 0, (ends - 1) // tm - first_tile + 1, 0)
    vend = jnp.cumsum(nvis)
    voff = vend - nvis
    total = vend[-1]
    v = jnp.arange(V, dtype=jnp.int32)
    e_v = jnp.minimum(jnp.sum(vend[None, :] <= v[:, None], axis=1), E - 1).astype(jnp.int32)
    valid = v < total
    e_last = jnp.sum(jnp.where(v == total - 1, e_v, 0))
    e_v = jnp.where(valid, e_v, e_last)
    tile_v = jnp.where(valid, _lookup(first_tile, e_v) + v - _lookup(voff, e_v), A // tm - 1)
    base = tile_v * tm
    lo_v = jnp.where(valid, jnp.maximum(_lookup(starts, e_v), base) - base, 0)
    hi_v = jnp.where(valid, jnp.minimum(_lookup(ends, e_v), base + tm) - base, 0)
    prev_tile = jnp.concatenate([jnp.full((1,), -1, tile_v.dtype), tile_v[:-1]])
    first_v = jnp.logical_and(valid, tile_v != prev_tile)
    has = nvis > 0
    ids = jnp.arange(E)
    slot_e = (jnp.cumsum(has) - 1) & 1
    later = jnp.logical_and(ids[None, :] > ids[:, None], has[None, :])
    next_e = jnp.min(jnp.where(later, ids[None, :], E), axis=1)
    next_e = jnp.where(next_e == E, -1, next_e)
    i32 = lambda a: a.astype(jnp.int32)
    unit_v = jnp.zeros_like(lo_v)
    code = 1
    for level in range(1, EXP_SPLIT_LEVELS + 1):
        n = tm >> level
        fits = jnp.logical_and(hi_v > lo_v, lo_v // n == (hi_v - 1) // n)
        unit_v = jnp.where(fits, code + lo_v // n, unit_v)
        code += 1 << level
    return starts, (i32(tile_v), i32(e_v), i32(lo_v), i32(hi_v), i32(first_v),
                    i32(_lookup(slot_e, e_v)), i32(_lookup(next_e, e_v)), i32(unit_v))


def expert_ffn(xs, tables, w_up, w_down, layer):
    V = tables[0].shape[0]
    D, F2 = w_up.shape[-2:]
    Fd = w_down.shape[-2]
    blk = EXP_TM * SLAB_ROWS
    nt = len(tables)
    tile_map = lambda v, *t: (t[0][v], 0)
    grid_spec = pltpu.PrefetchScalarGridSpec(
        num_scalar_prefetch=nt,
        grid=(V,),
        in_specs=[pl.BlockSpec((blk, LANES), tile_map),
                  pl.BlockSpec(memory_space=pl.ANY), pl.BlockSpec(memory_space=pl.ANY)],
        out_specs=pl.BlockSpec((blk, LANES), tile_map),
        scratch_shapes=[pltpu.VMEM((2, D, F2), F32), pltpu.VMEM((2, Fd, D), F32),
                        pltpu.VMEM((D, F2), BF16), pltpu.VMEM((Fd, D), BF16),
                        pltpu.SemaphoreType.DMA((2, 2))],
    )
    return pl.pallas_call(
        functools.partial(_expert_kernel, layer),
        grid_spec=grid_spec,
        out_shape=jax.ShapeDtypeStruct(xs.shape, jnp.uint32),
        compiler_params=_cparams("arbitrary"),
        name="expert_ffn",
    )(*tables, xs, w_up, w_down)


def _combine_ln_kernel(pos_ref, nxt_ref, gate_ref, h_ref, sh_ref, g_ref, b_ref, y_hbm, of_ref, ob_ref,
                       ybuf, acc_lo, acc_hi, gate_rows, sem):
    T = h_ref.shape[0]
    i = pl.program_id(0)
    slot = i % 2

    last = i == pl.num_programs(0) - 1

    def issue_tile(p_ref, s):
        for t in range(T):
            for k in range(TOP_K):
                src0 = pl.multiple_of(p_ref[k, t] * SLAB_ROWS, SLAB_ROWS)
                pltpu.make_async_copy(y_hbm.at[pl.ds(src0, SLAB_ROWS)],
                                      ybuf.at[s, k, pl.ds(t * SLAB_ROWS, SLAB_ROWS)],
                                      sem.at[s]).start(priority=k % 2)

    def wait_tile(s):
        for k in range(TOP_K):
            pltpu.make_async_copy(y_hbm.at[pl.ds(0, T * SLAB_ROWS)], ybuf.at[s, k], sem.at[s]).wait()

    for k in range(TOP_K):
        for c in range(T // LANES):
            gk = gate_ref[k:k + 1, c * LANES:(c + 1) * LANES]
            gate_rows[k, c * LANES:(c + 1) * LANES, :] = jnp.broadcast_to(gk, (LANES, LANES)).T

    @pl.when(i == 0)
    def _():
        issue_tile(pos_ref, slot)

    wait_tile(slot)
    issue_tile(nxt_ref, 1 - slot)
    for t in range(T):
        r0 = t * SLAB_ROWS
        lo, hi = _unpack_pair(ybuf[slot, 0, pl.ds(r0, SLAB_ROWS), :])
        g = gate_rows[0, t:t + 1, :]
        lo, hi = g * lo, g * hi
        for k in range(1, TOP_K):
            l2, h2 = _unpack_pair(ybuf[slot, k, pl.ds(r0, SLAB_ROWS), :])
            g = gate_rows[k, t:t + 1, :]
            lo, hi = lo + g * l2, hi + g * h2
        acc_lo[pl.ds(r0, SLAB_ROWS), :] = lo
        acc_hi[pl.ds(r0, SLAB_ROWS), :] = hi

    @pl.when(last)
    def _():
        wait_tile(1 - slot)

    routed = jnp.concatenate([acc[pl.ds(j, T, stride=SLAB_ROWS), :] for j in range(SLAB_ROWS)
                              for acc in (acc_lo, acc_hi)], axis=-1)
    y = DEEPNORM_ALPHA * h_ref[...] + routed + sh_ref[...].astype(F32)
    y = _ln_rows(y, g_ref[...], b_ref[...])
    of_ref[...] = y
    ob_ref[...] = y.astype(BF16)


def combine_ln(pos_t, gate_t, h, shared, y, g, b):
    N, D = h.shape
    nt, K, T = pos_t.shape
    row = lambda i: (i, 0)
    const = lambda i: (0, 0)
    return pl.pallas_call(
        _combine_ln_kernel,
        grid=(nt,),
        in_specs=[pl.BlockSpec((None, K, T), lambda i: (i, 0, 0), memory_space=pltpu.SMEM),
                  pl.BlockSpec((None, K, T), lambda i: (jnp.minimum(i + 1, nt - 1), 0, 0), memory_space=pltpu.SMEM),
                  pl.BlockSpec((None, K, T), lambda i: (i, 0, 0)),
                  pl.BlockSpec((T, D), row), pl.BlockSpec((T, D), row),
                  pl.BlockSpec((1, D), const), pl.BlockSpec((1, D), const),
                  pl.BlockSpec(memory_space=pl.ANY)],
        out_specs=[pl.BlockSpec((T, D), row), pl.BlockSpec((T, D), row)],
        out_shape=[jax.ShapeDtypeStruct((N, D), F32), jax.ShapeDtypeStruct((N, D), BF16)],
        scratch_shapes=[pltpu.VMEM((2, K, T * SLAB_ROWS, LANES), jnp.uint32), pltpu.VMEM((T * SLAB_ROWS, LANES), F32),
                        pltpu.VMEM((T * SLAB_ROWS, LANES), F32), pltpu.VMEM((K, T, LANES), F32),
                        pltpu.SemaphoreType.DMA((2,))],
        compiler_params=_cparams("arbitrary"),
        name="combine_ln",
    )(pos_t, pos_t, gate_t, h, shared, g.astype(F32).reshape(1, D), b.astype(F32).reshape(1, D), y)


def moe_ffn_ln(h, h_bf, layer, router_w, router_bias, exp_w_up, exp_w_down, shared_w_up, shared_w_down, g, b):
    N, D = h.shape
    idx, rank, gate, counts = route(h, router_w[layer], router_bias[layer])
    starts, tables = _visit_tables(counts, EXP_TM, N * TOP_K)
    pos = slot_positions(starts, idx, rank)
    shared, xs = dispatch_shared(_tiles(pos, min(DISPATCH_T, N)), h_bf, shared_w_up, shared_w_down, layer)
    y = expert_ffn(xs, tables, exp_w_up, exp_w_down, layer)
    tc = min(COMBINE_T, N)
    return combine_ln(_tiles(pos, tc), _tiles(gate, tc), h, shared, y, g, b)


def kernel(x, mem, conv_w_in, conv_dw, conv_dw_b, conv_ln_g, conv_ln_b, fox_w_in, fox_b_f, mem_w_kv, w_out,
           ln_g, ln_b, router_w, router_bias, exp_w_up, exp_w_down, shared_w_up, shared_w_down):
    B, S, D = x.shape
    N = B * S
    h = x.reshape(N, D).astype(F32)
    h_bf = h.astype(BF16)
    mem_bf = mem.reshape(B * MEM_LEN, D).astype(BF16)
    nf = 3 * PRIMARY_WIDTH
    for i in range(DEPTH):
        j = i // N_MIXERS
        if i % N_MIXERS == 0:
            proj = matmul(h_bf, conv_w_in, j, 2 * CONV_CH + MEM_WIDTH, BF16).reshape(B, S, -1)
            prim = conv_mixer(proj, conv_dw[j], conv_dw_b[j], conv_ln_g[j], conv_ln_b[j])
            q_src, q_block = proj, 2 * CONV_CH // MEM_WIDTH
        else:
            proj = matmul(h_bf, fox_w_in, j, nf, BF16).reshape(B, S, nf)
            tail = trailing_columns(fox_w_in, j, nf)
            w_qm = tail[None, :, N_FOX_HEADS:N_FOX_HEADS + MEM_WIDTH]
            q_src, q_block = matmul(h_bf, w_qm, 0, MEM_WIDTH, BF16).reshape(B, S, MEM_WIDTH), 0
            c = forget_cumlog(h_bf.reshape(B, S, D), tail[:, :N_FOX_HEADS], fox_b_f[j])
            prim = fox_attention(proj, c)
        kv = matmul(mem_bf, mem_w_kv, i, 2 * MEM_WIDTH, BF16).reshape(B, MEM_LEN, 2 * MEM_WIDTH)
        memo = memory_attention(q_src, q_block, kv)
        h, h_bf = outproj_ln(prim.reshape(N, PRIMARY_WIDTH), memo.reshape(N, MEM_WIDTH), w_out, i, h,
                             ln_g[i, 0], ln_b[i, 0])
        h, h_bf = moe_ffn_ln(h, h_bf, i, router_w, router_bias, exp_w_up, exp_w_down,
                             shared_w_up, shared_w_down, ln_g[i, 1], ln_b[i, 1])
    return h.reshape(B, S, D)
```

```python
import functools

import jax
import jax.numpy as jnp
from jax import lax
from jax.experimental import pallas as pl
from jax.experimental.pallas import tpu as pltpu

F32 = jnp.float32
BF16 = jnp.bfloat16

D_MODEL = 2048
DEPTH = 4
N_MIXERS = 2
MEM_LEN = 256
HEAD_DIM = 128
N_MEM_HEADS = 4
MEM_WIDTH = N_MEM_HEADS * HEAD_DIM
PRIMARY_WIDTH = D_MODEL - MEM_WIDTH
CONV_CH = PRIMARY_WIDTH
CONV_WIDTH = 31
N_FOX_HEADS = PRIMARY_WIDTH // HEAD_DIM
N_EXPERTS = 64
TOP_K = 8
N_GROUPS = 8
TOPK_GROUPS = 4
D_EXPERT = 512
ROUTED_SCALE = 2.5
LN_EPS = 1e-5
DEEPNORM_ALPHA = (2 * DEPTH) ** 0.25
LOG2E = 1.4426950408889634

LANES = 128
VMEM_LIMIT = 56 * 1024 * 1024
MM_TM, MM_TN = 2048, 512
ROW_TILE = 512
CONV_TS = 256
CONV_HALO = 32
CONV_SUB = 64
FOX_TQ = 2048
FOX_TK = 1024
FOX_HP = 2
MEM_TQ = 512
EXP_TM = 512
EXP_SPLIT_LEVELS = 2
SLAB_ROWS = D_MODEL // LANES // 2
ROUTE_T = 256
DISPATCH_T = 256
COMBINE_T = 128


def _cparams(*sem):
    return pltpu.CompilerParams(dimension_semantics=sem, vmem_limit_bytes=VMEM_LIMIT)


def _ln_rows(y, g, b):
    mu = jnp.mean(y, axis=-1, keepdims=True)
    d = y - mu
    var = jnp.mean(d * d, axis=-1, keepdims=True)
    return d * lax.rsqrt(var + LN_EPS) * g + b


def _mm_kernel(a_ref, w_ref, o_ref, w_bf):
    @pl.when(pl.program_id(1) == 0)
    def _():
        w_bf[...] = w_ref[...].astype(BF16)

    o_ref[...] = jnp.dot(a_ref[...], w_bf[...], preferred_element_type=F32).astype(o_ref.dtype)


def matmul(a, w, layer, n_cols, out_dtype, tm=MM_TM, tn=MM_TN):
    M, K = a.shape
    tm, tn = min(tm, M), min(tn, n_cols)
    assert M % tm == 0 and n_cols % tn == 0 and w.shape[1] == K
    return pl.pallas_call(
        _mm_kernel,
        grid=(n_cols // tn, M // tm),
        in_specs=[pl.BlockSpec((tm, K), lambda j, i: (i, 0)),
                  pl.BlockSpec((None, K, tn), lambda j, i: (layer, 0, j))],
        out_specs=pl.BlockSpec((tm, tn), lambda j, i: (i, j)),
        out_shape=jax.ShapeDtypeStruct((M, n_cols), out_dtype),
        scratch_shapes=[pltpu.VMEM((K, tn), BF16)],
        compiler_params=_cparams("arbitrary", "arbitrary"),
        name="matmul",
    )(a, w)


def _cols_kernel(c0, w_ref, o_ref):
    n = w_ref.shape[1] - c0
    o_ref[:, :n] = w_ref[:, c0:]
    o_ref[:, n:] = jnp.zeros((o_ref.shape[0], o_ref.shape[1] - n), o_ref.dtype)


def trailing_columns(w, layer, c0):
    _, K, C = w.shape
    n_pad = -(-(C - c0) // LANES) * LANES
    tk = min(256, K)
    return pl.pallas_call(
        functools.partial(_cols_kernel, c0),
        grid=(K // tk,),
        in_specs=[pl.BlockSpec((None, tk, C), lambda i: (layer, i, 0))],
        out_specs=pl.BlockSpec((tk, n_pad), lambda i: (i, 0)),
        out_shape=jax.ShapeDtypeStruct((K, n_pad), w.dtype),
        compiler_params=_cparams("arbitrary"),
        name="trailing_columns",
    )(w)


def _conv_kernel(a_ref, g_ref, ah_ref, gh_ref, dw_ref, dwb_ref, lg_ref, lb_ref, o_ref, u_scr, y_scr):
    ts = a_ref.shape[0]
    nchunk = CONV_CH // LANES
    first = pl.program_id(1) == 0
    uh = ah_ref[...].astype(F32) * jax.nn.sigmoid(gh_ref[...].astype(F32))
    uh = jnp.where(first, 0.0, uh)
    u = a_ref[...].astype(F32) * jax.nn.sigmoid(g_ref[...].astype(F32))
    for c in range(nchunk):
        sl = slice(c * LANES, (c + 1) * LANES)
        u_scr[c, 0:CONV_HALO, :] = uh[:, sl]
        u_scr[c, CONV_HALO:CONV_HALO + ts, :] = u[:, sl]

    off = CONV_HALO - (CONV_WIDTH - 1)

    def chunk_body(c, carry):
        w = dw_ref[c]
        bias = dwb_ref[c]
        for r in range(ts // CONV_SUB):
            acc = jnp.broadcast_to(bias, (CONV_SUB, LANES))
            for j in range(CONV_WIDTH):
                acc = acc + w[j:j + 1, :] * u_scr[c, pl.ds(off + j + r * CONV_SUB, CONV_SUB), :]
            y_scr[c, r * CONV_SUB:(r + 1) * CONV_SUB, :] = acc
        return carry

    lax.fori_loop(0, nchunk, chunk_body, 0)

    s1 = jnp.zeros((ts, 1), F32)
    for c in range(nchunk):
        s1 = s1 + jnp.sum(y_scr[c], axis=-1, keepdims=True)
    mu = s1 * (1.0 / CONV_CH)
    s2 = jnp.zeros((ts, 1), F32)
    for c in range(nchunk):
        d = y_scr[c] - mu
        s2 = s2 + jnp.sum(d * d, axis=-1, keepdims=True)
    rstd = lax.rsqrt(s2 * (1.0 / CONV_CH) + LN_EPS)
    for c in range(nchunk):
        sl = slice(c * LANES, (c + 1) * LANES)
        z = (y_scr[c] - mu) * rstd * lg_ref[:, sl] + lb_ref[:, sl]
        o_ref[:, sl] = (z * jax.nn.sigmoid(z)).astype(o_ref.dtype)


def conv_mixer(proj, dw, dw_b, ln_g, ln_b):
    B, S, _ = proj.shape
    ts = min(CONV_TS, S)
    nchunk = CONV_CH // LANES
    hb = ts // CONV_HALO
    dw_p = jnp.zeros((CONV_HALO, CONV_CH), F32).at[:CONV_WIDTH].set(dw.astype(F32))
    dw_c = dw_p.reshape(CONV_HALO, nchunk, LANES).transpose(1, 0, 2)
    dwb_c = dw_b.astype(F32).reshape(nchunk, 1, LANES)
    halo_idx = lambda b, i: (b, jnp.maximum(i * hb - 1, 0), 0)
    halo_idx_g = lambda b, i: (b, jnp.maximum(i * hb - 1, 0), 1)
    return pl.pallas_call(
        _conv_kernel,
        grid=(B, S // ts),
        in_specs=[
            pl.BlockSpec((None, ts, CONV_CH), lambda b, i: (b, i, 0)),
            pl.BlockSpec((None, ts, CONV_CH), lambda b, i: (b, i, 1)),
            pl.BlockSpec((None, CONV_HALO, CONV_CH), halo_idx),
            pl.BlockSpec((None, CONV_HALO, CONV_CH), halo_idx_g),
            pl.BlockSpec((nchunk, CONV_HALO, LANES), lambda b, i: (0, 0, 0)),
            pl.BlockSpec((nchunk, 1, LANES), lambda b, i: (0, 0, 0)),
            pl.BlockSpec((1, CONV_CH), lambda b, i: (0, 0)),
            pl.BlockSpec((1, CONV_CH), lambda b, i: (0, 0)),
        ],
        out_specs=pl.BlockSpec((None, ts, CONV_CH), lambda b, i: (b, i, 0)),
        out_shape=jax.ShapeDtypeStruct((B, S, CONV_CH), BF16),
        scratch_shapes=[pltpu.VMEM((nchunk, CONV_HALO + ts, LANES), F32),
                        pltpu.VMEM((nchunk, ts, LANES), F32)],
        compiler_params=_cparams("arbitrary", "arbitrary"),
        name="conv_mixer",
    )(proj, proj, proj, proj, dw_c, dwb_c, ln_g.astype(F32).reshape(1, -1), ln_b.astype(F32).reshape(1, -1))


def _fgate_kernel(h_ref, w_ref, b_ref, c_ref, carry_ref):
    ts = h_ref.shape[0]

    @pl.when(pl.program_id(1) == 0)
    def _():
        carry_ref[...] = jnp.zeros_like(carry_ref)

    f = jnp.dot(h_ref[...], w_ref[...], preferred_element_type=F32) + b_ref[...]
    ls = jax.nn.log_sigmoid(f)
    row = lax.broadcasted_iota(jnp.int32, (ts, ts), 0)
    col = lax.broadcasted_iota(jnp.int32, (ts, ts), 1)
    tri = jnp.where(row >= col, 1.0, 0.0).astype(BF16)
    hi = ls.astype(BF16)
    r1 = ls - hi.astype(F32)
    mid = r1.astype(BF16)
    lo = (r1 - mid.astype(F32)).astype(BF16)
    cs = (jnp.dot(tri, hi, preferred_element_type=F32)
          + jnp.dot(tri, mid, preferred_element_type=F32)
          + jnp.dot(tri, lo, preferred_element_type=F32))
    c = cs + carry_ref[...]
    c_ref[...] = c
    carry_ref[...] = c[ts - 1:ts, :]


def forget_cumlog(h_bf, w_f, b_f):
    B, S, D = h_bf.shape
    H = w_f.shape[1]
    ts = min(256, S)
    w_p = jnp.zeros((D, LANES), BF16).at[:, :H].set(w_f.astype(BF16))
    b_p = jnp.zeros((1, LANES), F32).at[0, :H].set(b_f.astype(F32))
    return pl.pallas_call(
        _fgate_kernel,
        grid=(B, S // ts),
        in_specs=[pl.BlockSpec((None, ts, D), lambda b, i: (b, i, 0)),
                  pl.BlockSpec((D, LANES), lambda b, i: (0, 0)),
                  pl.BlockSpec((1, LANES), lambda b, i: (0, 0))],
        out_specs=pl.BlockSpec((None, ts, LANES), lambda b, i: (b, i, 0)),
        out_shape=jax.ShapeDtypeStruct((B, S, LANES), F32),
        scratch_shapes=[pltpu.VMEM((1, LANES), F32)],
        compiler_params=_cparams("arbitrary", "arbitrary"),
        name="forget_cumlog",
    )(h_bf, w_p, b_p)


def _fox_kernel(q_ref, k_ref, v_ref, cq_ref, ck_ref, o_ref):
    tq = q_ref.shape[0]
    tk = ck_ref.shape[-1]
    i = pl.program_id(2)
    qscale = HEAD_DIM ** -0.5 * LOG2E
    heads = range(FOX_HP)
    hs = [slice(h * HEAD_DIM, (h + 1) * HEAD_DIM) for h in heads]
    qs = [(q_ref[:, hs[h]].astype(F32) * qscale).astype(BF16) for h in heads]
    cqs = [jnp.concatenate([jnp.broadcast_to(cq_ref[h, r], (LANES, LANES)).T[:, 0:1] for r in range(tq // LANES)],
                           axis=0) for h in heads]

    def step(j, carry, masked):
        start = pl.multiple_of(j * tk, tk)
        ss = []
        for h in heads:
            k = k_ref[pl.ds(start, tk), hs[h]]
            ss.append(lax.dot_general(qs[h], k, (((1,), (1,)), ((), ())), preferred_element_type=F32))
        if masked:
            row = i * tq + lax.broadcasted_iota(jnp.int32, (tq, tk), 0)
            col = j * tk + lax.broadcasted_iota(jnp.int32, (tq, tk), 1)
            causal = row >= col
        ps, ms, alphas = [], [], []
        for h in heads:
            m = carry[h][0]
            s = ss[h] - ck_ref[h, j]
            if masked:
                s = jnp.where(causal, s, -jnp.inf)
            m_new = jnp.maximum(m, jnp.max(s, axis=-1, keepdims=True) + cqs[h])
            p = jnp.exp2(s - (m_new - cqs[h]))
            alpha = jnp.exp2(m - m_new)
            ps.append(p.astype(BF16))
            ms.append(m_new)
            alphas.append(alpha)
        out = []
        for h in heads:
            v1 = jnp.concatenate([v_ref[pl.ds(start, tk), hs[h]], ones], axis=-1)
            acc = alphas[h] * carry[h][1] + jnp.dot(ps[h], v1, preferred_element_type=F32)
            out.append((ms[h], acc))
        return tuple(out)

    ones = jnp.ones((tk, HEAD_DIM), BF16)
    init = tuple((jnp.full((tq, 1), -jnp.inf, F32), jnp.zeros((tq, 2 * HEAD_DIM), F32)) for _ in heads)
    n_full = (i * tq) // tk
    carry = lax.fori_loop(0, n_full, lambda j, c: step(j, c, False), init)
    for d in range(-(-tq // tk)):
        carry = step(n_full + d, carry, True)
    for h in heads:
        acc = carry[h][1]
        o_ref[:, hs[h]] = (acc[:, :HEAD_DIM] / acc[:, HEAD_DIM:]).astype(o_ref.dtype)


def fox_attention(proj, c):
    B, S, _ = proj.shape
    H = N_FOX_HEADS
    tq, tk = min(FOX_TQ, S), min(FOX_TK, S)
    nq, nk = S // tq, S // tk
    ng = H // FOX_HP
    w = FOX_HP * HEAD_DIM
    c_h = jnp.transpose(c[:, :, :H], (0, 2, 1)) * LOG2E
    c_q = c_h.reshape(B, H, S // LANES, 1, LANES)
    c_row = c_h.reshape(B, H, nk, 1, tk)
    return pl.pallas_call(
        _fox_kernel,
        grid=(B, ng, nq),
        in_specs=[
            pl.BlockSpec((None, tq, w), lambda b, g, i: (b, i, g)),
            pl.BlockSpec((None, S, w), lambda b, g, i: (b, 0, ng + g)),
            pl.BlockSpec((None, S, w), lambda b, g, i: (b, 0, 2 * ng + g)),
            pl.BlockSpec((None, FOX_HP, tq // LANES, 1, LANES), lambda b, g, i: (b, g, i, 0, 0)),
            pl.BlockSpec((None, FOX_HP, nk, 1, tk), lambda b, g, i: (b, g, 0, 0, 0)),
        ],
        out_specs=pl.BlockSpec((None, tq, w), lambda b, g, i: (b, i, g)),
        out_shape=jax.ShapeDtypeStruct((B, S, PRIMARY_WIDTH), BF16),
        compiler_params=_cparams("arbitrary", "arbitrary", "arbitrary"),
        name="fox_attention",
    )(proj, proj, proj, c_q, c_row)


def _mem_attn_kernel(q_ref, kv_ref, o_ref):
    scale = HEAD_DIM ** -0.5
    for h in range(N_MEM_HEADS):
        sl = slice(h * HEAD_DIM, (h + 1) * HEAD_DIM)
        slv = slice(MEM_WIDTH + h * HEAD_DIM, MEM_WIDTH + (h + 1) * HEAD_DIM)
        s = lax.dot_general(q_ref[:, sl], kv_ref[:, sl], (((1,), (1,)), ((), ())),
                            preferred_element_type=F32) * scale
        m = jnp.max(s, axis=-1, keepdims=True)
        p = jnp.exp(s - m)
        l = jnp.sum(p, axis=-1, keepdims=True)
        o = jnp.dot(p.astype(BF16), kv_ref[:, slv], preferred_element_type=F32)
        o_ref[:, sl] = (o / l).astype(o_ref.dtype)


def memory_attention(proj, q_block, kv):
    B, S, _ = proj.shape
    M = kv.shape[1]
    tq = min(MEM_TQ, S)
    return pl.pallas_call(
        _mem_attn_kernel,
        grid=(B, S // tq),
        in_specs=[pl.BlockSpec((None, tq, MEM_WIDTH), lambda b, i: (b, i, q_block)),
                  pl.BlockSpec((None, M, 2 * MEM_WIDTH), lambda b, i: (b, 0, 0))],
        out_specs=pl.BlockSpec((None, tq, MEM_WIDTH), lambda b, i: (b, i, 0)),
        out_shape=jax.ShapeDtypeStruct((B, S, MEM_WIDTH), BF16),
        compiler_params=_cparams("arbitrary", "arbitrary"),
        name="memory_attention",
    )(proj, kv)


def _outproj_ln_kernel(p_ref, m_ref, wt_ref, wb_ref, h_ref, g_ref, b_ref, of_ref, ob_ref, wt_bf, wb_bf):
    @pl.when(pl.program_id(0) == 0)
    def _():
        wt_bf[...] = wt_ref[...].astype(BF16)
        wb_bf[...] = wb_ref[...].astype(BF16)

    mix = (jnp.dot(p_ref[...], wt_bf[...], preferred_element_type=F32)
           + jnp.dot(m_ref[...], wb_bf[...], preferred_element_type=F32))
    y = _ln_rows(DEEPNORM_ALPHA * h_ref[...] + mix, g_ref[...], b_ref[...])
    of_ref[...] = y
    ob_ref[...] = y.astype(BF16)


def outproj_ln(prim, memo, w_out, layer, h, g, b):
    N, D = h.shape
    tm = min(ROW_TILE, N)
    row = lambda i: (i, 0)
    const = lambda i: (0, 0)
    once = pl.Buffered(1)
    return pl.pallas_call(
        _outproj_ln_kernel,
        grid=(N // tm,),
        in_specs=[pl.BlockSpec((tm, PRIMARY_WIDTH), row), pl.BlockSpec((tm, MEM_WIDTH), row),
                  pl.BlockSpec((None, PRIMARY_WIDTH, D), lambda i: (layer, 0, 0), pipeline_mode=once),
                  pl.BlockSpec((None, MEM_WIDTH, D), lambda i: (layer, PRIMARY_WIDTH // MEM_WIDTH, 0),
                               pipeline_mode=once),
                  pl.BlockSpec((tm, D), row), pl.BlockSpec((1, D), const), pl.BlockSpec((1, D), const)],
        out_specs=[pl.BlockSpec((tm, D), row), pl.BlockSpec((tm, D), row)],
        out_shape=[jax.ShapeDtypeStruct((N, D), F32), jax.ShapeDtypeStruct((N, D), BF16)],
        scratch_shapes=[pltpu.VMEM((PRIMARY_WIDTH, D), BF16), pltpu.VMEM((MEM_WIDTH, D), BF16)],
        compiler_params=_cparams("arbitrary"),
        name="outproj_ln",
    )(prim, memo, w_out, w_out, h, g.astype(F32).reshape(1, D), b.astype(F32).reshape(1, D))


def _route_kernel(h_ref, w_ref, b_ref, idx_ref, rank_ref, gate_ref, cnt_ref, carry_ref):
    T = h_ref.shape[0]
    E = w_ref.shape[0]
    gsz = E // N_GROUPS
    neg = -jnp.inf

    @pl.when(pl.program_id(0) == 0)
    def _():
        carry_ref[...] = jnp.zeros_like(carry_ref)

    def split(a):
        hi = a.astype(BF16)
        return hi, (a - hi.astype(F32)).astype(BF16)

    nt = lambda a, b: lax.dot_general(a, b, (((1,), (1,)), ((), ())), preferred_element_type=F32)
    w_hi, w_lo = split(w_ref[...])
    h_hi, h_lo = split(h_ref[...])
    logits = nt(w_hi, h_hi) + (nt(w_hi, h_lo) + nt(w_lo, h_hi))
    scores = jax.nn.sigmoid(logits)
    choice = scores + b_ref[...]

    io8 = lax.broadcasted_iota(jnp.int32, (gsz, T), 0)
    rows = []
    for g in range(N_GROUPS):
        blk = choice[g * gsz:(g + 1) * gsz, :]
        m1 = jnp.max(blk, axis=0, keepdims=True)
        f1 = jnp.min(jnp.where(blk == m1, io8, gsz), axis=0, keepdims=True)
        m2 = jnp.max(jnp.where(io8 == f1, neg, blk), axis=0, keepdims=True)
        rows.append(jnp.broadcast_to(m1 + m2, (gsz, T)))
    gs = jnp.concatenate(rows, axis=0)
    eid = lax.broadcasted_iota(jnp.int32, (E, T), 0)
    gid = eid // gsz
    emask = jnp.zeros((E, T), jnp.bool_)
    for _ in range(TOPK_GROUPS):
        m = jnp.max(gs, axis=0, keepdims=True)
        g = jnp.min(jnp.where(gs == m, gid, N_GROUPS), axis=0, keepdims=True)
        hit = gid == g
        emask = jnp.logical_or(emask, hit)
        gs = jnp.where(hit, neg, gs)

    masked = jnp.where(emask, choice, neg)
    sel = jnp.zeros((E, T), jnp.bool_)
    ids, ws, hits = [], [], []
    for _ in range(TOP_K):
        m = jnp.max(masked, axis=0, keepdims=True)
        i_k = jnp.min(jnp.where(masked == m, eid, E), axis=0, keepdims=True)
        hit = eid == i_k
        ids.append(i_k)
        ws.append(jnp.sum(jnp.where(hit, scores, 0.0), axis=0, keepdims=True))
        hits.append(hit)
        masked = jnp.where(hit, neg, masked)
        sel = jnp.logical_or(sel, hit)
    wsum = ws[0]
    for w in ws[1:]:
        wsum = wsum + w

    sel_f = jnp.where(sel, 1.0, 0.0)
    r = lax.broadcasted_iota(jnp.int32, (T, T), 0)
    c = lax.broadcasted_iota(jnp.int32, (T, T), 1)
    tri = jnp.where(r < c, 1.0, 0.0).astype(BF16)
    before = jnp.dot(sel_f.astype(BF16), tri, preferred_element_type=F32) + carry_ref[...]
    for k in range(TOP_K):
        idx_ref[k:k + 1, :] = ids[k]
        rank_ref[k:k + 1, :] = jnp.sum(jnp.where(hits[k], before, 0.0), axis=0, keepdims=True).astype(jnp.int32)
        gate_ref[k:k + 1, :] = ws[k] / wsum * ROUTED_SCALE
    total = carry_ref[...] + jnp.sum(sel_f, axis=1, keepdims=True)
    carry_ref[...] = total
    cnt_ref[...] = jnp.broadcast_to(total, cnt_ref.shape).astype(jnp.int32)


def route(h, w_r, r_bias):
    N, D = h.shape
    E = w_r.shape[1]
    T = min(ROUTE_T, N)
    kn = lambda i: (0, i)
    idx, rank, gate, cnt = pl.pallas_call(
        _route_kernel,
        grid=(N // T,),
        in_specs=[pl.BlockSpec((T, D), lambda i: (i, 0)), pl.BlockSpec((E, D), lambda i: (0, 0)),
                  pl.BlockSpec((E, 1), lambda i: (0, 0))],
        out_specs=[pl.BlockSpec((TOP_K, T), kn), pl.BlockSpec((TOP_K, T), kn), pl.BlockSpec((TOP_K, T), kn),
                   pl.BlockSpec((E, LANES), lambda i: (0, 0))],
        out_shape=[jax.ShapeDtypeStruct((TOP_K, N), jnp.int32), jax.ShapeDtypeStruct((TOP_K, N), jnp.int32),
                   jax.ShapeDtypeStruct((TOP_K, N), F32), jax.ShapeDtypeStruct((E, LANES), jnp.int32)],
        scratch_shapes=[pltpu.VMEM((E, 1), F32)],
        compiler_params=_cparams("arbitrary"),
        name="route",
    )(h, w_r.astype(F32).T, r_bias.astype(F32).reshape(E, 1))
    return idx, rank, gate, cnt[:, 0]


def _pos_kernel(starts_ref, idx_ref, rank_ref, pos_ref):
    idx = idx_ref[...]
    acc = rank_ref[...]
    for e in range(N_EXPERTS):
        acc = acc + jnp.where(idx == e, starts_ref[e], 0)
    pos_ref[...] = acc


def slot_positions(starts, idx, rank):
    K, N = idx.shape
    T = min(2048, N)
    grid_spec = pltpu.PrefetchScalarGridSpec(
        num_scalar_prefetch=1, grid=(N // T,),
        in_specs=[pl.BlockSpec((K, T), lambda i, s: (0, i)), pl.BlockSpec((K, T), lambda i, s: (0, i))],
        out_specs=pl.BlockSpec((K, T), lambda i, s: (0, i)))
    return pl.pallas_call(
        _pos_kernel, grid_spec=grid_spec, out_shape=jax.ShapeDtypeStruct((K, N), jnp.int32),
        compiler_params=_cparams("arbitrary"), name="slot_positions",
    )(starts.astype(jnp.int32), idx, rank)


def _pack_pair(a, b):
    ua = lax.bitcast_convert_type(a.astype(BF16).astype(F32), jnp.uint32)
    ub = lax.bitcast_convert_type(b.astype(BF16).astype(F32), jnp.uint32)
    return ub | (ua >> 16)


def _unpack_pair(w):
    lo = lax.bitcast_convert_type(w << 16, F32)
    hi = lax.bitcast_convert_type(w & jnp.uint32(0xFFFF0000), F32)
    return lo, hi


def _rows_to_slab(x):
    return [_pack_pair(x[:, 2 * j * LANES:(2 * j + 1) * LANES], x[:, (2 * j + 1) * LANES:(2 * j + 2) * LANES])
            for j in range(SLAB_ROWS)]


def _slab_to_rows(pieces):
    return jnp.concatenate([half for w in pieces for half in _unpack_pair(w)], axis=-1)


def _tiles(a, T):
    K, N = a.shape
    return a.reshape(K, N // T, T).transpose(1, 0, 2)


def _swiglu_bf(x, wu, wd):
    f = wd.shape[0]
    h = jnp.dot(x, wu, preferred_element_type=F32)
    a = h[:, :f]
    act = (a * jax.nn.sigmoid(a)) * h[:, f:]
    return jnp.dot(act.astype(BF16), wd, preferred_element_type=F32)


def _dispatch_kernel(pos_ref, hb_ref, su_ref, sd_ref, sh_ref, xs_hbm, slab, su_bf, sd_bf, sem):
    T = hb_ref.shape[0]

    @pl.when(pl.program_id(0) == 0)
    def _():
        su_bf[...] = su_ref[...].astype(BF16)
        sd_bf[...] = sd_ref[...].astype(BF16)

    hb = hb_ref[...]
    for j, piece in enumerate(_rows_to_slab(hb.astype(F32))):
        slab[pl.ds(j, T, stride=SLAB_ROWS), :] = piece

    for t in range(T):
        src = slab.at[pl.ds(t * SLAB_ROWS, SLAB_ROWS)]
        for k in range(TOP_K):
            dst = xs_hbm.at[pl.ds(pl.multiple_of(pos_ref[k, t] * SLAB_ROWS, SLAB_ROWS), SLAB_ROWS)]
            pltpu.make_async_copy(src, dst, sem).start(priority=k % 2)
    sh_ref[...] = _swiglu_bf(hb, su_bf[...], sd_bf[...]).astype(sh_ref.dtype)
    for k in range(TOP_K):
        pltpu.make_async_copy(slab, xs_hbm.at[pl.ds(0, T * SLAB_ROWS)], sem).wait()


def dispatch_shared(pos_t, h_bf, sh_up, sh_down, layer):
    N, D = h_bf.shape
    nt, K, T = pos_t.shape
    F2 = sh_up.shape[-1]
    Fd = sh_down.shape[1]
    return pl.pallas_call(
        _dispatch_kernel,
        grid=(nt,),
        in_specs=[pl.BlockSpec((None, K, T), lambda i: (i, 0, 0), memory_space=pltpu.SMEM),
                  pl.BlockSpec((T, D), lambda i: (i, 0)),
                  pl.BlockSpec((None, D, F2), lambda i: (layer, 0, 0)),
                  pl.BlockSpec((None, Fd, D), lambda i: (layer, 0, 0))],
        out_specs=[pl.BlockSpec((T, D), lambda i: (i, 0)), pl.BlockSpec(memory_space=pl.ANY)],
        out_shape=[jax.ShapeDtypeStruct((N, D), BF16),
                   jax.ShapeDtypeStruct((N * K * SLAB_ROWS, LANES), jnp.uint32)],
        scratch_shapes=[pltpu.VMEM((T * SLAB_ROWS, LANES), jnp.uint32), pltpu.VMEM((D, F2), BF16),
                        pltpu.VMEM((Fd, D), BF16), pltpu.SemaphoreType.DMA(())],
        compiler_params=_cparams("arbitrary"),
        name="dispatch_shared",
    )(pos_t, h_bf, sh_up, sh_down)


def _expert_kernel(layer, tile_ref, be_ref, lo_ref, hi_ref, first_ref, slot_ref, next_ref, unit_ref, x_ref, wu_hbm, wd_hbm,
                   o_ref, wu_f32, wd_f32, wu_bf, wd_bf, sem):
    v = pl.program_id(0)
    tm = x_ref.shape[0] // SLAB_ROWS
    lo = lo_ref[v]
    hi = hi_ref[v]

    def weight_copies(e, slot):
        return (pltpu.make_async_copy(wu_hbm.at[layer, e], wu_f32.at[slot], sem.at[0, slot]),
                pltpu.make_async_copy(wd_hbm.at[layer, e], wd_f32.at[slot], sem.at[1, slot]))

    @pl.when(hi > lo)
    def _():
        changed = jnp.logical_or(v == 0, be_ref[v] != be_ref[jnp.maximum(v - 1, 0)])

        @pl.when(changed)
        def _():
            slot = slot_ref[v]

            @pl.when(v == 0)
            def _():
                for c in weight_copies(be_ref[v], slot):
                    c.start()

            for c in weight_copies(be_ref[v], slot):
                c.wait()

            @pl.when(next_ref[v] >= 0)
            def _():
                for c in weight_copies(next_ref[v], 1 - slot):
                    c.start()

            wu_bf[...] = wu_f32[slot].astype(BF16)
            wd_bf[...] = wd_f32[slot].astype(BF16)

        def ffn_rows(r0, n, may_be_first):
            base = r0 * SLAB_ROWS
            x = _slab_to_rows([x_ref[pl.ds(base + j, n, stride=SLAB_ROWS), :] for j in range(SLAB_ROWS)])
            y = _swiglu_bf(x.astype(BF16), wu_bf[...], wd_bf[...])
            row = r0 + lax.broadcasted_iota(jnp.int32, (n, 1), 0)
            mine = jnp.logical_and(row >= lo, row < hi)
            pieces = _rows_to_slab(y)

            def merge():
                for j in range(SLAB_ROWS):
                    keep = o_ref[pl.ds(base + j, n, stride=SLAB_ROWS), :]
                    o_ref[pl.ds(base + j, n, stride=SLAB_ROWS), :] = jnp.where(mine, pieces[j], keep)

            if not may_be_first:
                merge()
                return

            @pl.when(first_ref[v] == 1)
            def _():
                for j in range(SLAB_ROWS):
                    o_ref[pl.ds(base + j, n, stride=SLAB_ROWS), :] = jnp.where(mine, pieces[j], jnp.uint32(0))

            pl.when(first_ref[v] == 0)(merge)

        unit = unit_ref[v]

        @pl.when(unit == 0)
        def _():
            ffn_rows(0, tm, True)

        @pl.when(jnp.logical_and(unit != 0, first_ref[v] == 1))
        def _():
            o_ref[...] = jnp.zeros_like(o_ref)

        code = 1
        for level in range(1, EXP_SPLIT_LEVELS + 1):
            n = tm >> level
            for part in range(1 << level):
                pl.when(unit == code)(functools.partial(ffn_rows, part * n, n, False))
                code += 1


def _lookup(table, idx):
    n = table.shape[0]
    return jnp.sum(jnp.where(idx[:, None] == jnp.arange(n)[None, :], table[None, :], 0), axis=1)


def _visit_tables(counts, tm, A):
    E = counts.shape[0]
    V = A // tm + E
    starts = jnp.cumsum(counts) - counts
    ends = starts + counts
    first_tile = starts // tm
    nvis = jnp.where(counts > 0, (ends - 1) // tm - first_tile + 1, 0)
    vend = jnp.cumsum(nvis)
    voff = vend - nvis
    total = vend[-1]
    v = jnp.arange(V, dtype=jnp.int32)
    e_v = jnp.minimum(jnp.sum(vend[None, :] <= v[:, None], axis=1), E - 1).astype(jnp.int32)
    valid = v < total
    e_last = jnp.sum(jnp.where(v == total - 1, e_v, 0))
    e_v = jnp.where(valid, e_v, e_last)
    tile_v = jnp.where(valid, _lookup(first_tile, e_v) + v - _lookup(voff, e_v), A // tm - 1)
    base = tile_v * tm
    lo_v = jnp.where(valid, jnp.maximum(_lookup(starts, e_v), base) - base, 0)
    hi_v = jnp.where(valid, jnp.minimum(_lookup(ends, e_v), base + tm) - base, 0)
    prev_tile = jnp.concatenate([jnp.full((1,), -1, tile_v.dtype), tile_v[:-1]])
    first_v = jnp.logical_and(valid, tile_v != prev_tile)
    has = nvis > 0
    ids = jnp.arange(E)
    slot_e = (jnp.cumsum(has) - 1) & 1
    later = jnp.logical_and(ids[None, :] > ids[:, None], has[None, :])
    next_e = jnp.min(jnp.where(later, ids[None, :], E), axis=1)
    next_e = jnp.where(next_e == E, -1, next_e)
    i32 = lambda a: a.astype(jnp.int32)
    unit_v = jnp.zeros_like(lo_v)
    code = 1
    for level in range(1, EXP_SPLIT_LEVELS + 1):
        n = tm >> level
        fits = jnp.logical_and(hi_v > lo_v, lo_v // n == (hi_v - 1) // n)
        unit_v = jnp.where(fits, code + lo_v // n, unit_v)
        code += 1 << level
    return starts, (i32(tile_v), i32(e_v), i32(lo_v), i32(hi_v), i32(first_v),
                    i32(_lookup(slot_e, e_v)), i32(_lookup(next_e, e_v)), i32(unit_v))


def expert_ffn(xs, tables, w_up, w_down, layer):
    V = tables[0].shape[0]
    D, F2 = w_up.shape[-2:]
    Fd = w_down.shape[-2]
    blk = EXP_TM * SLAB_ROWS
    nt = len(tables)
    tile_map = lambda v, *t: (t[0][v], 0)
    grid_spec = pltpu.PrefetchScalarGridSpec(
        num_scalar_prefetch=nt,
        grid=(V,),
        in_specs=[pl.BlockSpec((blk, LANES), tile_map),
                  pl.BlockSpec(memory_space=pl.ANY), pl.BlockSpec(memory_space=pl.ANY)],
        out_specs=pl.BlockSpec((blk, LANES), tile_map),
        scratch_shapes=[pltpu.VMEM((2, D, F2), F32), pltpu.VMEM((2, Fd, D), F32),
                        pltpu.VMEM((D, F2), BF16), pltpu.VMEM((Fd, D), BF16),
                        pltpu.SemaphoreType.DMA((2, 2))],
    )
    return pl.pallas_call(
        functools.partial(_expert_kernel, layer),
        grid_spec=grid_spec,
        out_shape=jax.ShapeDtypeStruct(xs.shape, jnp.uint32),
        compiler_params=_cparams("arbitrary"),
        name="expert_ffn",
    )(*tables, xs, w_up, w_down)


def _combine_ln_kernel(pos_ref, nxt_ref, gate_ref, h_ref, sh_ref, g_ref, b_ref, y_hbm, of_ref, ob_ref,
                       ybuf, acc_lo, acc_hi, gate_rows, sem):
    T = h_ref.shape[0]
    i = pl.program_id(0)
    slot = i % 2

    last = i == pl.num_programs(0) - 1

    def issue_tile(p_ref, s):
        for t in range(T):
            for k in range(TOP_K):
                src0 = pl.multiple_of(p_ref[k, t] * SLAB_ROWS, SLAB_ROWS)
                pltpu.make_async_copy(y_hbm.at[pl.ds(src0, SLAB_ROWS)],
                                      ybuf.at[s, k, pl.ds(t * SLAB_ROWS, SLAB_ROWS)],
                                      sem.at[s]).start(priority=k % 2)

    def wait_tile(s):
        for k in range(TOP_K):
            pltpu.make_async_copy(y_hbm.at[pl.ds(0, T * SLAB_ROWS)], ybuf.at[s, k], sem.at[s]).wait()

    for k in range(TOP_K):
        for c in range(T // LANES):
            gk = gate_ref[k:k + 1, c * LANES:(c + 1) * LANES]
            gate_rows[k, c * LANES:(c + 1) * LANES, :] = jnp.broadcast_to(gk, (LANES, LANES)).T

    @pl.when(i == 0)
    def _():
        issue_tile(pos_ref, slot)

    wait_tile(slot)
    issue_tile(nxt_ref, 1 - slot)
    for t in range(T):
        r0 = t * SLAB_ROWS
        lo, hi = _unpack_pair(ybuf[slot, 0, pl.ds(r0, SLAB_ROWS), :])
        g = gate_rows[0, t:t + 1, :]
        lo, hi = g * lo, g * hi
        for k in range(1, TOP_K):
            l2, h2 = _unpack_pair(ybuf[slot, k, pl.ds(r0, SLAB_ROWS), :])
            g = gate_rows[k, t:t + 1, :]
            lo, hi = lo + g * l2, hi + g * h2
        acc_lo[pl.ds(r0, SLAB_ROWS), :] = lo
        acc_hi[pl.ds(r0, SLAB_ROWS), :] = hi

    @pl.when(last)
    def _():
        wait_tile(1 - slot)

    routed = jnp.concatenate([acc[pl.ds(j, T, stride=SLAB_ROWS), :] for j in range(SLAB_ROWS)
                              for acc in (acc_lo, acc_hi)], axis=-1)
    y = DEEPNORM_ALPHA * h_ref[...] + routed + sh_ref[...].astype(F32)
    y = _ln_rows(y, g_ref[...], b_ref[...])
    of_ref[...] = y
    ob_ref[...] = y.astype(BF16)


def combine_ln(pos_t, gate_t, h, shared, y, g, b):
    N, D = h.shape
    nt, K, T = pos_t.shape
    row = lambda i: (i, 0)
    const = lambda i: (0, 0)
    return pl.pallas_call(
        _combine_ln_kernel,
        grid=(nt,),
        in_specs=[pl.BlockSpec((None, K, T), lambda i: (i, 0, 0), memory_space=pltpu.SMEM),
                  pl.BlockSpec((None, K, T), lambda i: (jnp.minimum(i + 1, nt - 1), 0, 0), memory_space=pltpu.SMEM),
                  pl.BlockSpec((None, K, T), lambda i: (i, 0, 0)),
                  pl.BlockSpec((T, D), row), pl.BlockSpec((T, D), row),
                  pl.BlockSpec((1, D), const), pl.BlockSpec((1, D), const),
                  pl.BlockSpec(memory_space=pl.ANY)],
        out_specs=[pl.BlockSpec((T, D), row), pl.BlockSpec((T, D), row)],
        out_shape=[jax.ShapeDtypeStruct((N, D), F32), jax.ShapeDtypeStruct((N, D), BF16)],
        scratch_shapes=[pltpu.VMEM((2, K, T * SLAB_ROWS, LANES), jnp.uint32), pltpu.VMEM((T * SLAB_ROWS, LANES), F32),
                        pltpu.VMEM((T * SLAB_ROWS, LANES), F32), pltpu.VMEM((K, T, LANES), F32),
                        pltpu.SemaphoreType.DMA((2,))],
        compiler_params=_cparams("arbitrary"),
        name="combine_ln",
    )(pos_t, pos_t, gate_t, h, shared, g.astype(F32).reshape(1, D), b.astype(F32).reshape(1, D), y)


def moe_ffn_ln(h, h_bf, layer, router_w, router_bias, exp_w_up, exp_w_down, shared_w_up, shared_w_down, g, b):
    N, D = h.shape
    idx, rank, gate, counts = route(h, router_w[layer], router_bias[layer])
    starts, tables = _visit_tables(counts, EXP_TM, N * TOP_K)
    pos = slot_positions(starts, idx, rank)
    shared, xs = dispatch_shared(_tiles(pos, min(DISPATCH_T, N)), h_bf, shared_w_up, shared_w_down, layer)
    y = expert_ffn(xs, tables, exp_w_up, exp_w_down, layer)
    tc = min(COMBINE_T, N)
    return combine_ln(_tiles(pos, tc), _tiles(gate, tc), h, shared, y, g, b)


def kernel(x, mem, conv_w_in, conv_dw, conv_dw_b, conv_ln_g, conv_ln_b, fox_w_in, fox_b_f, mem_w_kv, w_out,
           ln_g, ln_b, router_w, router_bias, exp_w_up, exp_w_down, shared_w_up, shared_w_down):
    B, S, D = x.shape
    N = B * S
    h = x.reshape(N, D).astype(F32)
    h_bf = h.astype(BF16)
    mem_bf = mem.reshape(B * MEM_LEN, D).astype(BF16)
    nf = 3 * PRIMARY_WIDTH
    for i in range(DEPTH):
        j = i // N_MIXERS
        if i % N_MIXERS == 0:
            proj = matmul(h_bf, conv_w_in, j, 2 * CONV_CH + MEM_WIDTH, BF16).reshape(B, S, -1)
            prim = conv_mixer(proj, conv_dw[j], conv_dw_b[j], conv_ln_g[j], conv_ln_b[j])
            q_src, q_block = proj, 2 * CONV_CH // MEM_WIDTH
        else:
            proj = matmul(h_bf, fox_w_in, j, nf, BF16).reshape(B, S, nf)
            tail = trailing_columns(fox_w_in, j, nf)
            w_qm = tail[None, :, N_FOX_HEADS:N_FOX_HEADS + MEM_WIDTH]
            q_src, q_block = matmul(h_bf, w_qm, 0, MEM_WIDTH, BF16).reshape(B, S, MEM_WIDTH), 0
            c = forget_cumlog(h_bf.reshape(B, S, D), tail[:, :N_FOX_HEADS], fox_b_f[j])
            prim = fox_attention(proj, c)
        kv = matmul(mem_bf, mem_w_kv, i, 2 * MEM_WIDTH, BF16).reshape(B, MEM_LEN, 2 * MEM_WIDTH)
        memo = memory_attention(q_src, q_block, kv)
        h, h_bf = outproj_ln(prim.reshape(N, PRIMARY_WIDTH), memo.reshape(N, MEM_WIDTH), w_out, i, h,
                             ln_g[i, 0], ln_b[i, 0])
        h, h_bf = moe_ffn_ln(h, h_bf, i, router_w, router_bias, exp_w_up, exp_w_down,
                             shared_w_up, shared_w_down, ln_g[i, 1], ln_b[i, 1])
    return h.reshape(B, S, D)
```

```python
import functools

import jax
import jax.numpy as jnp
from jax import lax
from jax.experimental import pallas as pl
from jax.experimental.pallas import tpu as pltpu

F32 = jnp.float32
BF16 = jnp.bfloat16

D_MODEL = 2048
DEPTH = 4
N_MIXERS = 2
MEM_LEN = 256
HEAD_DIM = 128
N_MEM_HEADS = 4
MEM_WIDTH = N_MEM_HEADS * HEAD_DIM
PRIMARY_WIDTH = D_MODEL - MEM_WIDTH
CONV_CH = PRIMARY_WIDTH
CONV_WIDTH = 31
N_FOX_HEADS = PRIMARY_WIDTH // HEAD_DIM
N_EXPERTS = 64
TOP_K = 8
N_GROUPS = 8
TOPK_GROUPS = 4
D_EXPERT = 512
ROUTED_SCALE = 2.5
LN_EPS = 1e-5
DEEPNORM_ALPHA = (2 * DEPTH) ** 0.25
LOG2E = 1.4426950408889634

LANES = 128
VMEM_LIMIT = 56 * 1024 * 1024
MM_TM, MM_TN = 2048, 512
ROW_TILE = 512
CONV_TS = 256
CONV_HALO = 32
CONV_SUB = 64
FOX_TQ = 2048
FOX_TK = 1024
FOX_HP = 2
MEM_TQ = 512
EXP_TM = 1024
EXP_SPLIT_LEVELS = 3
SLAB_ROWS = D_MODEL // LANES // 2
ROUTE_T = 256
DISPATCH_T = 256
COMBINE_T = 128


def _cparams(*sem):
    return pltpu.CompilerParams(dimension_semantics=sem, vmem_limit_bytes=VMEM_LIMIT)


def _ln_rows(y, g, b):
    mu = jnp.mean(y, axis=-1, keepdims=True)
    d = y - mu
    var = jnp.mean(d * d, axis=-1, keepdims=True)
    return d * lax.rsqrt(var + LN_EPS) * g + b


def _mm_kernel(a_ref, w_ref, o_ref, w_bf):
    @pl.when(pl.program_id(1) == 0)
    def _():
        w_bf[...] = w_ref[...].astype(BF16)

    o_ref[...] = jnp.dot(a_ref[...], w_bf[...], preferred_element_type=F32).astype(o_ref.dtype)


def matmul(a, w, layer, n_cols, out_dtype, tm=MM_TM, tn=MM_TN):
    M, K = a.shape
    tm, tn = min(tm, M), min(tn, n_cols)
    assert M % tm == 0 and n_cols % tn == 0 and w.shape[1] == K
    return pl.pallas_call(
        _mm_kernel,
        grid=(n_cols // tn, M // tm),
        in_specs=[pl.BlockSpec((tm, K), lambda j, i: (i, 0)),
                  pl.BlockSpec((None, K, tn), lambda j, i: (layer, 0, j))],
        out_specs=pl.BlockSpec((tm, tn), lambda j, i: (i, j)),
        out_shape=jax.ShapeDtypeStruct((M, n_cols), out_dtype),
        scratch_shapes=[pltpu.VMEM((K, tn), BF16)],
        compiler_params=_cparams("arbitrary", "arbitrary"),
        name="matmul",
    )(a, w)


def _cols_kernel(c0, w_ref, o_ref):
    n = w_ref.shape[1] - c0
    o_ref[:, :n] = w_ref[:, c0:]
    o_ref[:, n:] = jnp.zeros((o_ref.shape[0], o_ref.shape[1] - n), o_ref.dtype)


def trailing_columns(w, layer, c0):
    _, K, C = w.shape
    n_pad = -(-(C - c0) // LANES) * LANES
    tk = min(256, K)
    return pl.pallas_call(
        functools.partial(_cols_kernel, c0),
        grid=(K // tk,),
        in_specs=[pl.BlockSpec((None, tk, C), lambda i: (layer, i, 0))],
        out_specs=pl.BlockSpec((tk, n_pad), lambda i: (i, 0)),
        out_shape=jax.ShapeDtypeStruct((K, n_pad), w.dtype),
        compiler_params=_cparams("arbitrary"),
        name="trailing_columns",
    )(w)


def _conv_kernel(a_ref, g_ref, ah_ref, gh_ref, dw_ref, dwb_ref, lg_ref, lb_ref, o_ref, u_scr, y_scr):
    ts = a_ref.shape[0]
    nchunk = CONV_CH // LANES
    first = pl.program_id(1) == 0
    uh = ah_ref[...].astype(F32) * jax.nn.sigmoid(gh_ref[...].astype(F32))
    uh = jnp.where(first, 0.0, uh)
    u = a_ref[...].astype(F32) * jax.nn.sigmoid(g_ref[...].astype(F32))
    for c in range(nchunk):
        sl = slice(c * LANES, (c + 1) * LANES)
        u_scr[c, 0:CONV_HALO, :] = uh[:, sl]
        u_scr[c, CONV_HALO:CONV_HALO + ts, :] = u[:, sl]

    off = CONV_HALO - (CONV_WIDTH - 1)

    def chunk_body(c, carry):
        w = dw_ref[c]
        bias = dwb_ref[c]
        for r in range(ts // CONV_SUB):
            acc = jnp.broadcast_to(bias, (CONV_SUB, LANES))
            for j in range(CONV_WIDTH):
                acc = acc + w[j:j + 1, :] * u_scr[c, pl.ds(off + j + r * CONV_SUB, CONV_SUB), :]
            y_scr[c, r * CONV_SUB:(r + 1) * CONV_SUB, :] = acc
        return carry

    lax.fori_loop(0, nchunk, chunk_body, 0)

    s1 = jnp.zeros((ts, 1), F32)
    for c in range(nchunk):
        s1 = s1 + jnp.sum(y_scr[c], axis=-1, keepdims=True)
    mu = s1 * (1.0 / CONV_CH)
    s2 = jnp.zeros((ts, 1), F32)
    for c in range(nchunk):
        d = y_scr[c] - mu
        s2 = s2 + jnp.sum(d * d, axis=-1, keepdims=True)
    rstd = lax.rsqrt(s2 * (1.0 / CONV_CH) + LN_EPS)
    for c in range(nchunk):
        sl = slice(c * LANES, (c + 1) * LANES)
        z = (y_scr[c] - mu) * rstd * lg_ref[:, sl] + lb_ref[:, sl]
        o_ref[:, sl] = (z * jax.nn.sigmoid(z)).astype(o_ref.dtype)


def conv_mixer(proj, dw, dw_b, ln_g, ln_b):
    B, S, _ = proj.shape
    ts = min(CONV_TS, S)
    nchunk = CONV_CH // LANES
    hb = ts // CONV_HALO
    dw_p = jnp.zeros((CONV_HALO, CONV_CH), F32).at[:CONV_WIDTH].set(dw.astype(F32))
    dw_c = dw_p.reshape(CONV_HALO, nchunk, LANES).transpose(1, 0, 2)
    dwb_c = dw_b.astype(F32).reshape(nchunk, 1, LANES)
    halo_idx = lambda b, i: (b, jnp.maximum(i * hb - 1, 0), 0)
    halo_idx_g = lambda b, i: (b, jnp.maximum(i * hb - 1, 0), 1)
    return pl.pallas_call(
        _conv_kernel,
        grid=(B, S // ts),
        in_specs=[
            pl.BlockSpec((None, ts, CONV_CH), lambda b, i: (b, i, 0)),
            pl.BlockSpec((None, ts, CONV_CH), lambda b, i: (b, i, 1)),
            pl.BlockSpec((None, CONV_HALO, CONV_CH), halo_idx),
            pl.BlockSpec((None, CONV_HALO, CONV_CH), halo_idx_g),
            pl.BlockSpec((nchunk, CONV_HALO, LANES), lambda b, i: (0, 0, 0)),
            pl.BlockSpec((nchunk, 1, LANES), lambda b, i: (0, 0, 0)),
            pl.BlockSpec((1, CONV_CH), lambda b, i: (0, 0)),
            pl.BlockSpec((1, CONV_CH), lambda b, i: (0, 0)),
        ],
        out_specs=pl.BlockSpec((None, ts, CONV_CH), lambda b, i: (b, i, 0)),
        out_shape=jax.ShapeDtypeStruct((B, S, CONV_CH), BF16),
        scratch_shapes=[pltpu.VMEM((nchunk, CONV_HALO + ts, LANES), F32),
                        pltpu.VMEM((nchunk, ts, LANES), F32)],
        compiler_params=_cparams("arbitrary", "arbitrary"),
        name="conv_mixer",
    )(proj, proj, proj, proj, dw_c, dwb_c, ln_g.astype(F32).reshape(1, -1), ln_b.astype(F32).reshape(1, -1))


def _fgate_kernel(h_ref, w_ref, b_ref, c_ref, carry_ref):
    ts = h_ref.shape[0]

    @pl.when(pl.program_id(1) == 0)
    def _():
        carry_ref[...] = jnp.zeros_like(carry_ref)

    f = jnp.dot(h_ref[...], w_ref[...], preferred_element_type=F32) + b_ref[...]
    ls = jax.nn.log_sigmoid(f)
    row = lax.broadcasted_iota(jnp.int32, (ts, ts), 0)
    col = lax.broadcasted_iota(jnp.int32, (ts, ts), 1)
    tri = jnp.where(row >= col, 1.0, 0.0).astype(BF16)
    hi = ls.astype(BF16)
    r1 = ls - hi.astype(F32)
    mid = r1.astype(BF16)
    lo = (r1 - mid.astype(F32)).astype(BF16)
    cs = (jnp.dot(tri, hi, preferred_element_type=F32)
          + jnp.dot(tri, mid, preferred_element_type=F32)
          + jnp.dot(tri, lo, preferred_element_type=F32))
    c = cs + carry_ref[...]
    c_ref[...] = c
    carry_ref[...] = c[ts - 1:ts, :]


def forget_cumlog(h_bf, w_f, b_f):
    B, S, D = h_bf.shape
    H = w_f.shape[1]
    ts = min(256, S)
    w_p = jnp.zeros((D, LANES), BF16).at[:, :H].set(w_f.astype(BF16))
    b_p = jnp.zeros((1, LANES), F32).at[0, :H].set(b_f.astype(F32))
    return pl.pallas_call(
        _fgate_kernel,
        grid=(B, S // ts),
        in_specs=[pl.BlockSpec((None, ts, D), lambda b, i: (b, i, 0)),
                  pl.BlockSpec((D, LANES), lambda b, i: (0, 0)),
                  pl.BlockSpec((1, LANES), lambda b, i: (0, 0))],
        out_specs=pl.BlockSpec((None, ts, LANES), lambda b, i: (b, i, 0)),
        out_shape=jax.ShapeDtypeStruct((B, S, LANES), F32),
        scratch_shapes=[pltpu.VMEM((1, LANES), F32)],
        compiler_params=_cparams("arbitrary", "arbitrary"),
        name="forget_cumlog",
    )(h_bf, w_p, b_p)


def _fox_kernel(q_ref, k_ref, v_ref, cq_ref, ck_ref, o_ref):
    tq = q_ref.shape[0]
    tk = ck_ref.shape[-1]
    i = pl.program_id(2)
    qscale = HEAD_DIM ** -0.5 * LOG2E
    heads = range(FOX_HP)
    hs = [slice(h * HEAD_DIM, (h + 1) * HEAD_DIM) for h in heads]
    qs = [(q_ref[:, hs[h]].astype(F32) * qscale).astype(BF16) for h in heads]
    cqs = [jnp.concatenate([jnp.broadcast_to(cq_ref[h, r], (LANES, LANES)).T[:, 0:1] for r in range(tq // LANES)],
                           axis=0) for h in heads]

    def step(j, carry, masked):
        start = pl.multiple_of(j * tk, tk)
        ss = []
        for h in heads:
            k = k_ref[pl.ds(start, tk), hs[h]]
            ss.append(lax.dot_general(qs[h], k, (((1,), (1,)), ((), ())), preferred_element_type=F32))
        if masked:
            row = i * tq + lax.broadcasted_iota(jnp.int32, (tq, tk), 0)
            col = j * tk + lax.broadcasted_iota(jnp.int32, (tq, tk), 1)
            causal = row >= col
        ps, ms, alphas = [], [], []
        for h in heads:
            m = carry[h][0]
            s = ss[h] - ck_ref[h, j]
            if masked:
                s = jnp.where(causal, s, -jnp.inf)
            m_new = jnp.maximum(m, jnp.max(s, axis=-1, keepdims=True) + cqs[h])
            p = jnp.exp2(s - (m_new - cqs[h]))
            alpha = jnp.exp2(m - m_new)
            ps.append(p.astype(BF16))
            ms.append(m_new)
            alphas.append(alpha)
        out = []
        for h in heads:
            v1 = jnp.concatenate([v_ref[pl.ds(start, tk), hs[h]], ones], axis=-1)
            acc = alphas[h] * carry[h][1] + jnp.dot(ps[h], v1, preferred_element_type=F32)
            out.append((ms[h], acc))
        return tuple(out)

    ones = jnp.ones((tk, HEAD_DIM), BF16)
    init = tuple((jnp.full((tq, 1), -jnp.inf, F32), jnp.zeros((tq, 2 * HEAD_DIM), F32)) for _ in heads)
    n_full = (i * tq) // tk
    carry = lax.fori_loop(0, n_full, lambda j, c: step(j, c, False), init)
    for d in range(-(-tq // tk)):
        carry = step(n_full + d, carry, True)
    for h in heads:
        acc = carry[h][1]
        o_ref[:, hs[h]] = (acc[:, :HEAD_DIM] / acc[:, HEAD_DIM:]).astype(o_ref.dtype)


def fox_attention(proj, c):
    B, S, _ = proj.shape
    H = N_FOX_HEADS
    tq, tk = min(FOX_TQ, S), min(FOX_TK, S)
    nq, nk = S // tq, S // tk
    ng = H // FOX_HP
    w = FOX_HP * HEAD_DIM
    c_h = jnp.transpose(c[:, :, :H], (0, 2, 1)) * LOG2E
    c_q = c_h.reshape(B, H, S // LANES, 1, LANES)
    c_row = c_h.reshape(B, H, nk, 1, tk)
    return pl.pallas_call(
        _fox_kernel,
        grid=(B, ng, nq),
        in_specs=[
            pl.BlockSpec((None, tq, w), lambda b, g, i: (b, i, g)),
            pl.BlockSpec((None, S, w), lambda b, g, i: (b, 0, ng + g)),
            pl.BlockSpec((None, S, w), lambda b, g, i: (b, 0, 2 * ng + g)),
            pl.BlockSpec((None, FOX_HP, tq // LANES, 1, LANES), lambda b, g, i: (b, g, i, 0, 0)),
            pl.BlockSpec((None, FOX_HP, nk, 1, tk), lambda b, g, i: (b, g, 0, 0, 0)),
        ],
        out_specs=pl.BlockSpec((None, tq, w), lambda b, g, i: (b, i, g)),
        out_shape=jax.ShapeDtypeStruct((B, S, PRIMARY_WIDTH), BF16),
        compiler_params=_cparams("arbitrary", "arbitrary", "arbitrary"),
        name="fox_attention",
    )(proj, proj, proj, c_q, c_row)


def _mem_attn_kernel(q_ref, kv_ref, o_ref):
    scale = HEAD_DIM ** -0.5
    for h in range(N_MEM_HEADS):
        sl = slice(h * HEAD_DIM, (h + 1) * HEAD_DIM)
        slv = slice(MEM_WIDTH + h * HEAD_DIM, MEM_WIDTH + (h + 1) * HEAD_DIM)
        s = lax.dot_general(q_ref[:, sl], kv_ref[:, sl], (((1,), (1,)), ((), ())),
                            preferred_element_type=F32) * scale
        m = jnp.max(s, axis=-1, keepdims=True)
        p = jnp.exp(s - m)
        l = jnp.sum(p, axis=-1, keepdims=True)
        o = jnp.dot(p.astype(BF16), kv_ref[:, slv], preferred_element_type=F32)
        o_ref[:, sl] = (o / l).astype(o_ref.dtype)


def memory_attention(proj, q_block, kv):
    B, S, _ = proj.shape
    M = kv.shape[1]
    tq = min(MEM_TQ, S)
    return pl.pallas_call(
        _mem_attn_kernel,
        grid=(B, S // tq),
        in_specs=[pl.BlockSpec((None, tq, MEM_WIDTH), lambda b, i: (b, i, q_block)),
                  pl.BlockSpec((None, M, 2 * MEM_WIDTH), lambda b, i: (b, 0, 0))],
        out_specs=pl.BlockSpec((None, tq, MEM_WIDTH), lambda b, i: (b, i, 0)),
        out_shape=jax.ShapeDtypeStruct((B, S, MEM_WIDTH), BF16),
        compiler_params=_cparams("arbitrary", "arbitrary"),
        name="memory_attention",
    )(proj, kv)


def _outproj_ln_kernel(p_ref, m_ref, wt_ref, wb_ref, h_ref, g_ref, b_ref, of_ref, ob_ref, wt_bf, wb_bf):
    @pl.when(pl.program_id(0) == 0)
    def _():
        wt_bf[...] = wt_ref[...].astype(BF16)
        wb_bf[...] = wb_ref[...].astype(BF16)

    mix = (jnp.dot(p_ref[...], wt_bf[...], preferred_element_type=F32)
           + jnp.dot(m_ref[...], wb_bf[...], preferred_element_type=F32))
    y = _ln_rows(DEEPNORM_ALPHA * h_ref[...] + mix, g_ref[...], b_ref[...])
    of_ref[...] = y
    ob_ref[...] = y.astype(BF16)


def outproj_ln(prim, memo, w_out, layer, h, g, b):
    N, D = h.shape
    tm = min(ROW_TILE, N)
    row = lambda i: (i, 0)
    const = lambda i: (0, 0)
    once = pl.Buffered(1)
    return pl.pallas_call(
        _outproj_ln_kernel,
        grid=(N // tm,),
        in_specs=[pl.BlockSpec((tm, PRIMARY_WIDTH), row), pl.BlockSpec((tm, MEM_WIDTH), row),
                  pl.BlockSpec((None, PRIMARY_WIDTH, D), lambda i: (layer, 0, 0), pipeline_mode=once),
                  pl.BlockSpec((None, MEM_WIDTH, D), lambda i: (layer, PRIMARY_WIDTH // MEM_WIDTH, 0),
                               pipeline_mode=once),
                  pl.BlockSpec((tm, D), row), pl.BlockSpec((1, D), const), pl.BlockSpec((1, D), const)],
        out_specs=[pl.BlockSpec((tm, D), row), pl.BlockSpec((tm, D), row)],
        out_shape=[jax.ShapeDtypeStruct((N, D), F32), jax.ShapeDtypeStruct((N, D), BF16)],
        scratch_shapes=[pltpu.VMEM((PRIMARY_WIDTH, D), BF16), pltpu.VMEM((MEM_WIDTH, D), BF16)],
        compiler_params=_cparams("arbitrary"),
        name="outproj_ln",
    )(prim, memo, w_out, w_out, h, g.astype(F32).reshape(1, D), b.astype(F32).reshape(1, D))


def _route_kernel(h_ref, w_ref, b_ref, idx_ref, rank_ref, gate_ref, cnt_ref, carry_ref):
    T = h_ref.shape[0]
    E = w_ref.shape[0]
    gsz = E // N_GROUPS
    neg = -jnp.inf

    @pl.when(pl.program_id(0) == 0)
    def _():
        carry_ref[...] = jnp.zeros_like(carry_ref)

    def split(a):
        hi = a.astype(BF16)
        return hi, (a - hi.astype(F32)).astype(BF16)

    nt = lambda a, b: lax.dot_general(a, b, (((1,), (1,)), ((), ())), preferred_element_type=F32)
    w_hi, w_lo = split(w_ref[...])
    h_hi, h_lo = split(h_ref[...])
    logits = nt(w_hi, h_hi) + (nt(w_hi, h_lo) + nt(w_lo, h_hi))
    scores = jax.nn.sigmoid(logits)
    choice = scores + b_ref[...]

    io8 = lax.broadcasted_iota(jnp.int32, (gsz, T), 0)
    rows = []
    for g in range(N_GROUPS):
        blk = choice[g * gsz:(g + 1) * gsz, :]
        m1 = jnp.max(blk, axis=0, keepdims=True)
        f1 = jnp.min(jnp.where(blk == m1, io8, gsz), axis=0, keepdims=True)
        m2 = jnp.max(jnp.where(io8 == f1, neg, blk), axis=0, keepdims=True)
        rows.append(jnp.broadcast_to(m1 + m2, (gsz, T)))
    gs = jnp.concatenate(rows, axis=0)
    eid = lax.broadcasted_iota(jnp.int32, (E, T), 0)
    gid = eid // gsz
    emask = jnp.zeros((E, T), jnp.bool_)
    for _ in range(TOPK_GROUPS):
        m = jnp.max(gs, axis=0, keepdims=True)
        g = jnp.min(jnp.where(gs == m, gid, N_GROUPS), axis=0, keepdims=True)
        hit = gid == g
        emask = jnp.logical_or(emask, hit)
        gs = jnp.where(hit, neg, gs)

    masked = jnp.where(emask, choice, neg)
    sel = jnp.zeros((E, T), jnp.bool_)
    ids, ws, hits = [], [], []
    for _ in range(TOP_K):
        m = jnp.max(masked, axis=0, keepdims=True)
        i_k = jnp.min(jnp.where(masked == m, eid, E), axis=0, keepdims=True)
        hit = eid == i_k
        ids.append(i_k)
        ws.append(jnp.sum(jnp.where(hit, scores, 0.0), axis=0, keepdims=True))
        hits.append(hit)
        masked = jnp.where(hit, neg, masked)
        sel = jnp.logical_or(sel, hit)
    wsum = ws[0]
    for w in ws[1:]:
        wsum = wsum + w

    sel_f = jnp.where(sel, 1.0, 0.0)
    r = lax.broadcasted_iota(jnp.int32, (T, T), 0)
    c = lax.broadcasted_iota(jnp.int32, (T, T), 1)
    tri = jnp.where(r < c, 1.0, 0.0).astype(BF16)
    before = jnp.dot(sel_f.astype(BF16), tri, preferred_element_type=F32) + carry_ref[...]
    for k in range(TOP_K):
        idx_ref[k:k + 1, :] = ids[k]
        rank_ref[k:k + 1, :] = jnp.sum(jnp.where(hits[k], before, 0.0), axis=0, keepdims=True).astype(jnp.int32)
        gate_ref[k:k + 1, :] = ws[k] / wsum * ROUTED_SCALE
    total = carry_ref[...] + jnp.sum(sel_f, axis=1, keepdims=True)
    carry_ref[...] = total
    cnt_ref[...] = jnp.broadcast_to(total, cnt_ref.shape).astype(jnp.int32)


def route(h, w_r, r_bias):
    N, D = h.shape
    E = w_r.shape[1]
    T = min(ROUTE_T, N)
    kn = lambda i: (0, i)
    idx, rank, gate, cnt = pl.pallas_call(
        _route_kernel,
        grid=(N // T,),
        in_specs=[pl.BlockSpec((T, D), lambda i: (i, 0)), pl.BlockSpec((E, D), lambda i: (0, 0)),
                  pl.BlockSpec((E, 1), lambda i: (0, 0))],
        out_specs=[pl.BlockSpec((TOP_K, T), kn), pl.BlockSpec((TOP_K, T), kn), pl.BlockSpec((TOP_K, T), kn),
                   pl.BlockSpec((E, LANES), lambda i: (0, 0))],
        out_shape=[jax.ShapeDtypeStruct((TOP_K, N), jnp.int32), jax.ShapeDtypeStruct((TOP_K, N), jnp.int32),
                   jax.ShapeDtypeStruct((TOP_K, N), F32), jax.ShapeDtypeStruct((E, LANES), jnp.int32)],
        scratch_shapes=[pltpu.VMEM((E, 1), F32)],
        compiler_params=_cparams("arbitrary"),
        name="route",
    )(h, w_r.astype(F32).T, r_bias.astype(F32).reshape(E, 1))
    return idx, rank, gate, cnt[:, 0]


def _pos_kernel(starts_ref, idx_ref, rank_ref, pos_ref):
    idx = idx_ref[...]
    acc = rank_ref[...]
    for e in range(N_EXPERTS):
        acc = acc + jnp.where(idx == e, starts_ref[e], 0)
    pos_ref[...] = acc


def slot_positions(starts, idx, rank):
    K, N = idx.shape
    T = min(2048, N)
    grid_spec = pltpu.PrefetchScalarGridSpec(
        num_scalar_prefetch=1, grid=(N // T,),
        in_specs=[pl.BlockSpec((K, T), lambda i, s: (0, i)), pl.BlockSpec((K, T), lambda i, s: (0, i))],
        out_specs=pl.BlockSpec((K, T), lambda i, s: (0, i)))
    return pl.pallas_call(
        _pos_kernel, grid_spec=grid_spec, out_shape=jax.ShapeDtypeStruct((K, N), jnp.int32),
        compiler_params=_cparams("arbitrary"), name="slot_positions",
    )(starts.astype(jnp.int32), idx, rank)


def _pack_pair(a, b):
    ua = lax.bitcast_convert_type(a.astype(BF16).astype(F32), jnp.uint32)
    ub = lax.bitcast_convert_type(b.astype(BF16).astype(F32), jnp.uint32)
    return ub | (ua >> 16)


def _unpack_pair(w):
    lo = lax.bitcast_convert_type(w << 16, F32)
    hi = lax.bitcast_convert_type(w & jnp.uint32(0xFFFF0000), F32)
    return lo, hi


def _rows_to_slab(x):
    return [_pack_pair(x[:, 2 * j * LANES:(2 * j + 1) * LANES], x[:, (2 * j + 1) * LANES:(2 * j + 2) * LANES])
            for j in range(SLAB_ROWS)]


def _slab_to_rows(pieces):
    return jnp.concatenate([half for w in pieces for half in _unpack_pair(w)], axis=-1)


def _tiles(a, T):
    K, N = a.shape
    return a.reshape(K, N // T, T).transpose(1, 0, 2)


def _swiglu_bf(x, wu, wd):
    f = wd.shape[0]
    h = jnp.dot(x, wu, preferred_element_type=F32)
    a = h[:, :f]
    act = (a * jax.nn.sigmoid(a)) * h[:, f:]
    return jnp.dot(act.astype(BF16), wd, preferred_element_type=F32)


def _dispatch_kernel(pos_ref, hb_ref, su_ref, sd_ref, sh_ref, xs_hbm, slab, su_bf, sd_bf, sem):
    T = hb_ref.shape[0]

    @pl.when(pl.program_id(0) == 0)
    def _():
        su_bf[...] = su_ref[...].astype(BF16)
        sd_bf[...] = sd_ref[...].astype(BF16)

    hb = hb_ref[...]
    for j, piece in enumerate(_rows_to_slab(hb.astype(F32))):
        slab[pl.ds(j, T, stride=SLAB_ROWS), :] = piece

    for t in range(T):
        src = slab.at[pl.ds(t * SLAB_ROWS, SLAB_ROWS)]
        for k in range(TOP_K):
            dst = xs_hbm.at[pl.ds(pl.multiple_of(pos_ref[k, t] * SLAB_ROWS, SLAB_ROWS), SLAB_ROWS)]
            pltpu.make_async_copy(src, dst, sem).start(priority=k % 2)
    sh_ref[...] = _swiglu_bf(hb, su_bf[...], sd_bf[...]).astype(sh_ref.dtype)
    for k in range(TOP_K):
        pltpu.make_async_copy(slab, xs_hbm.at[pl.ds(0, T * SLAB_ROWS)], sem).wait()


def dispatch_shared(pos_t, h_bf, sh_up, sh_down, layer):
    N, D = h_bf.shape
    nt, K, T = pos_t.shape
    F2 = sh_up.shape[-1]
    Fd = sh_down.shape[1]
    return pl.pallas_call(
        _dispatch_kernel,
        grid=(nt,),
        in_specs=[pl.BlockSpec((None, K, T), lambda i: (i, 0, 0), memory_space=pltpu.SMEM),
                  pl.BlockSpec((T, D), lambda i: (i, 0)),
                  pl.BlockSpec((None, D, F2), lambda i: (layer, 0, 0)),
                  pl.BlockSpec((None, Fd, D), lambda i: (layer, 0, 0))],
        out_specs=[pl.BlockSpec((T, D), lambda i: (i, 0)), pl.BlockSpec(memory_space=pl.ANY)],
        out_shape=[jax.ShapeDtypeStruct((N, D), BF16),
                   jax.ShapeDtypeStruct((N * K * SLAB_ROWS, LANES), jnp.uint32)],
        scratch_shapes=[pltpu.VMEM((T * SLAB_ROWS, LANES), jnp.uint32), pltpu.VMEM((D, F2), BF16),
                        pltpu.VMEM((Fd, D), BF16), pltpu.SemaphoreType.DMA(())],
        compiler_params=_cparams("arbitrary"),
        name="dispatch_shared",
    )(pos_t, h_bf, sh_up, sh_down)


def _expert_kernel(layer, tile_ref, be_ref, lo_ref, hi_ref, first_ref, slot_ref, next_ref, unit_ref, x_ref, wu_hbm, wd_hbm,
                   o_ref, wu_f32, wd_f32, wu_bf, wd_bf, sem):
    v = pl.program_id(0)
    tm = x_ref.shape[0] // SLAB_ROWS
    lo = lo_ref[v]
    hi = hi_ref[v]

    def weight_copies(e, slot):
        return (pltpu.make_async_copy(wu_hbm.at[layer, e], wu_f32.at[slot], sem.at[0, slot]),
                pltpu.make_async_copy(wd_hbm.at[layer, e], wd_f32.at[slot], sem.at[1, slot]))

    @pl.when(hi > lo)
    def _():
        changed = jnp.logical_or(v == 0, be_ref[v] != be_ref[jnp.maximum(v - 1, 0)])

        @pl.when(changed)
        def _():
            slot = slot_ref[v]

            @pl.when(v == 0)
            def _():
                for c in weight_copies(be_ref[v], slot):
                    c.start()

            for c in weight_copies(be_ref[v], slot):
                c.wait()

            @pl.when(next_ref[v] >= 0)
            def _():
                for c in weight_copies(next_ref[v], 1 - slot):
                    c.start()

            wu_bf[...] = wu_f32[slot].astype(BF16)
            wd_bf[...] = wd_f32[slot].astype(BF16)

        def ffn_rows(r0, n, may_be_first):
            base = r0 * SLAB_ROWS
            x = _slab_to_rows([x_ref[pl.ds(base + j, n, stride=SLAB_ROWS), :] for j in range(SLAB_ROWS)])
            y = _swiglu_bf(x.astype(BF16), wu_bf[...], wd_bf[...])
            row = r0 + lax.broadcasted_iota(jnp.int32, (n, 1), 0)
            mine = jnp.logical_and(row >= lo, row < hi)
            pieces = _rows_to_slab(y)

            def merge():
                for j in range(SLAB_ROWS):
                    keep = o_ref[pl.ds(base + j, n, stride=SLAB_ROWS), :]
                    o_ref[pl.ds(base + j, n, stride=SLAB_ROWS), :] = jnp.where(mine, pieces[j], keep)

            if not may_be_first:
                merge()
                return

            @pl.when(first_ref[v] == 1)
            def _():
                for j in range(SLAB_ROWS):
                    o_ref[pl.ds(base + j, n, stride=SLAB_ROWS), :] = jnp.where(mine, pieces[j], jnp.uint32(0))

            pl.when(first_ref[v] == 0)(merge)

        unit = unit_ref[v]

        @pl.when(unit == 0)
        def _():
            ffn_rows(0, tm, True)

        @pl.when(jnp.logical_and(unit != 0, first_ref[v] == 1))
        def _():
            o_ref[...] = jnp.zeros_like(o_ref)

        code = 1
        for level in range(1, EXP_SPLIT_LEVELS + 1):
            n = tm >> level
            for part in range(1 << level):
                pl.when(unit == code)(functools.partial(ffn_rows, part * n, n, False))
                code += 1


def _lookup(table, idx):
    n = table.shape[0]
    return jnp.sum(jnp.where(idx[:, None] == jnp.arange(n)[None, :], table[None, :], 0), axis=1)


def _visit_tables(counts, tm, A):
    E = counts.shape[0]
    V = A // tm + E
    starts = jnp.cumsum(counts) - counts
    ends = starts + counts
    first_tile = starts // tm
    nvis = jnp.where(counts > 0, (ends - 1) // tm - first_tile + 1, 0)
    vend = jnp.cumsum(nvis)
    voff = vend - nvis
    total = vend[-1]
    v = jnp.arange(V, dtype=jnp.int32)
    e_v = jnp.minimum(jnp.sum(vend[None, :] <= v[:, None], axis=1), E - 1).astype(jnp.int32)
    valid = v < total
    e_last = jnp.sum(jnp.where(v == total - 1, e_v, 0))
    e_v = jnp.where(valid, e_v, e_last)
    tile_v = jnp.where(valid, _lookup(first_tile, e_v) + v - _lookup(voff, e_v), A // tm - 1)
    base = tile_v * tm
    lo_v = jnp.where(valid, jnp.maximum(_lookup(starts, e_v), base) - base, 0)
    hi_v = jnp.where(valid, jnp.minimum(_lookup(ends, e_v), base + tm) - base, 0)
    prev_tile = jnp.concatenate([jnp.full((1,), -1, tile_v.dtype), tile_v[:-1]])
    first_v = jnp.logical_and(valid, tile_v != prev_tile)
    has = nvis > 0
    ids = jnp.arange(E)
    slot_e = (jnp.cumsum(has) - 1) & 1
    later = jnp.logical_and(ids[None, :] > ids[:, None], has[None, :])
    next_e = jnp.min(jnp.where(later, ids[None, :], E), axis=1)
    next_e = jnp.where(next_e == E, -1, next_e)
    i32 = lambda a: a.astype(jnp.int32)
    unit_v = jnp.zeros_like(lo_v)
    code = 1
    for level in range(1, EXP_SPLIT_LEVELS + 1):
        n = tm >> level
        fits = jnp.logical_and(hi_v > lo_v, lo_v // n == (hi_v - 1) // n)
        unit_v = jnp.where(fits, code + lo_v // n, unit_v)
        code += 1 << level
    return starts, (i32(tile_v), i32(e_v), i32(lo_v), i32(hi_v), i32(first_v),
                    i32(_lookup(slot_e, e_v)), i32(_lookup(next_e, e_v)), i32(unit_v))


def expert_ffn(xs, tables, w_up, w_down, layer):
    V = tables[0].shape[0]
    D, F2 = w_up.shape[-2:]
    Fd = w_down.shape[-2]
    blk = EXP_TM * SLAB_ROWS
    nt = len(tables)
    tile_map = lambda v, *t: (t[0][v], 0)
    grid_spec = pltpu.PrefetchScalarGridSpec(
        num_scalar_prefetch=nt,
        grid=(V,),
        in_specs=[pl.BlockSpec((blk, LANES), tile_map),
                  pl.BlockSpec(memory_space=pl.ANY), pl.BlockSpec(memory_space=pl.ANY)],
        out_specs=pl.BlockSpec((blk, LANES), tile_map),
        scratch_shapes=[pltpu.VMEM((2, D, F2), F32), pltpu.VMEM((2, Fd, D), F32),
                        pltpu.VMEM((D, F2), BF16), pltpu.VMEM((Fd, D), BF16),
                        pltpu.SemaphoreType.DMA((2, 2))],
    )
    return pl.pallas_call(
        functools.partial(_expert_kernel, layer),
        grid_spec=grid_spec,
        out_shape=jax.ShapeDtypeStruct(xs.shape, jnp.uint32),
        compiler_params=_cparams("arbitrary"),
        name="expert_ffn",
    )(*tables, xs, w_up, w_down)


def _combine_ln_kernel(pos_ref, nxt_ref, gate_ref, h_ref, sh_ref, g_ref, b_ref, y_hbm, of_ref, ob_ref,
                       ybuf, acc_lo, acc_hi, gate_rows, sem):
    T = h_ref.shape[0]
    i = pl.program_id(0)
    slot = i % 2

    last = i == pl.num_programs(0) - 1

    def issue_tile(p_ref, s):
        for t in range(T):
            for k in range(TOP_K):
                src0 = pl.multiple_of(p_ref[k, t] * SLAB_ROWS, SLAB_ROWS)
                pltpu.make_async_copy(y_hbm.at[pl.ds(src0, SLAB_ROWS)],
                                      ybuf.at[s, k, pl.ds(t * SLAB_ROWS, SLAB_ROWS)],
                                      sem.at[s]).start(priority=k % 2)

    def wait_tile(s):
        for k in range(TOP_K):
            pltpu.make_async_copy(y_hbm.at[pl.ds(0, T * SLAB_ROWS)], ybuf.at[s, k], sem.at[s]).wait()

    for k in range(TOP_K):
        for c in range(T // LANES):
            gk = gate_ref[k:k + 1, c * LANES:(c + 1) * LANES]
            gate_rows[k, c * LANES:(c + 1) * LANES, :] = jnp.broadcast_to(gk, (LANES, LANES)).T

    @pl.when(i == 0)
    def _():
        issue_tile(pos_ref, slot)

    wait_tile(slot)
    issue_tile(nxt_ref, 1 - slot)
    for t in range(T):
        r0 = t * SLAB_ROWS
        lo, hi = _unpack_pair(ybuf[slot, 0, pl.ds(r0, SLAB_ROWS), :])
        g = gate_rows[0, t:t + 1, :]
        lo, hi = g * lo, g * hi
        for k in range(1, TOP_K):
            l2, h2 = _unpack_pair(ybuf[slot, k, pl.ds(r0, SLAB_ROWS), :])
            g = gate_rows[k, t:t + 1, :]
            lo, hi = lo + g * l2, hi + g * h2
        acc_lo[pl.ds(r0, SLAB_ROWS), :] = lo
        acc_hi[pl.ds(r0, SLAB_ROWS), :] = hi

    @pl.when(last)
    def _():
        wait_tile(1 - slot)

    routed = jnp.concatenate([acc[pl.ds(j, T, stride=SLAB_ROWS), :] for j in range(SLAB_ROWS)
                              for acc in (acc_lo, acc_hi)], axis=-1)
    y = DEEPNORM_ALPHA * h_ref[...] + routed + sh_ref[...].astype(F32)
    y = _ln_rows(y, g_ref[...], b_ref[...])
    of_ref[...] = y
    ob_ref[...] = y.astype(BF16)


def combine_ln(pos_t, gate_t, h, shared, y, g, b):
    N, D = h.shape
    nt, K, T = pos_t.shape
    row = lambda i: (i, 0)
    const = lambda i: (0, 0)
    return pl.pallas_call(
        _combine_ln_kernel,
        grid=(nt,),
        in_specs=[pl.BlockSpec((None, K, T), lambda i: (i, 0, 0), memory_space=pltpu.SMEM),
                  pl.BlockSpec((None, K, T), lambda i: (jnp.minimum(i + 1, nt - 1), 0, 0), memory_space=pltpu.SMEM),
                  pl.BlockSpec((None, K, T), lambda i: (i, 0, 0)),
                  pl.BlockSpec((T, D), row), pl.BlockSpec((T, D), row),
                  pl.BlockSpec((1, D), const), pl.BlockSpec((1, D), const),
                  pl.BlockSpec(memory_space=pl.ANY)],
        out_specs=[pl.BlockSpec((T, D), row), pl.BlockSpec((T, D), row)],
        out_shape=[jax.ShapeDtypeStruct((N, D), F32), jax.ShapeDtypeStruct((N, D), BF16)],
        scratch_shapes=[pltpu.VMEM((2, K, T * SLAB_ROWS, LANES), jnp.uint32), pltpu.VMEM((T * SLAB_ROWS, LANES), F32),
                        pltpu.VMEM((T * SLAB_ROWS, LANES), F32), pltpu.VMEM((K, T, LANES), F32),
                        pltpu.SemaphoreType.DMA((2,))],
        compiler_params=_cparams("arbitrary"),
        name="combine_ln",
    )(pos_t, pos_t, gate_t, h, shared, g.astype(F32).reshape(1, D), b.astype(F32).reshape(1, D), y)


def moe_ffn_ln(h, h_bf, layer, router_w, router_bias, exp_w_up, exp_w_down, shared_w_up, shared_w_down, g, b):
    N, D = h.shape
    idx, rank, gate, counts = route(h, router_w[layer], router_bias[layer])
    starts, tables = _visit_tables(counts, EXP_TM, N * TOP_K)
    pos = slot_positions(starts, idx, rank)
    shared, xs = dispatch_shared(_tiles(pos, min(DISPATCH_T, N)), h_bf, shared_w_up, shared_w_down, layer)
    y = expert_ffn(xs, tables, exp_w_up, exp_w_down, layer)
    tc = min(COMBINE_T, N)
    return combine_ln(_tiles(pos, tc), _tiles(gate, tc), h, shared, y, g, b)


def kernel(x, mem, conv_w_in, conv_dw, conv_dw_b, conv_ln_g, conv_ln_b, fox_w_in, fox_b_f, mem_w_kv, w_out,
           ln_g, ln_b, router_w, router_bias, exp_w_up, exp_w_down, shared_w_up, shared_w_down):
    B, S, D = x.shape
    N = B * S
    h = x.reshape(N, D).astype(F32)
    h_bf = h.astype(BF16)
    mem_bf = mem.reshape(B * MEM_LEN, D).astype(BF16)
    nf = 3 * PRIMARY_WIDTH
    for i in range(DEPTH):
        j = i // N_MIXERS
        if i % N_MIXERS == 0:
            proj = matmul(h_bf, conv_w_in, j, 2 * CONV_CH + MEM_WIDTH, BF16).reshape(B, S, -1)
            prim = conv_mixer(proj, conv_dw[j], conv_dw_b[j], conv_ln_g[j], conv_ln_b[j])
            q_src, q_block = proj, 2 * CONV_CH // MEM_WIDTH
        else:
            proj = matmul(h_bf, fox_w_in, j, nf, BF16).reshape(B, S, nf)
            tail = trailing_columns(fox_w_in, j, nf)
            w_qm = tail[None, :, N_FOX_HEADS:N_FOX_HEADS + MEM_WIDTH]
            q_src, q_block = matmul(h_bf, w_qm, 0, MEM_WIDTH, BF16).reshape(B, S, MEM_WIDTH), 0
            c = forget_cumlog(h_bf.reshape(B, S, D), tail[:, :N_FOX_HEADS], fox_b_f[j])
            prim = fox_attention(proj, c)
        kv = matmul(mem_bf, mem_w_kv, i, 2 * MEM_WIDTH, BF16).reshape(B, MEM_LEN, 2 * MEM_WIDTH)
        memo = memory_attention(q_src, q_block, kv)
        h, h_bf = outproj_ln(prim.reshape(N, PRIMARY_WIDTH), memo.reshape(N, MEM_WIDTH), w_out, i, h,
                             ln_g[i, 0], ln_b[i, 0])
        h, h_bf = moe_ffn_ln(h, h_bf, i, router_w, router_bias, exp_w_up, exp_w_down,
                             shared_w_up, shared_w_down, ln_g[i, 1], ln_b[i, 1])
    return h.reshape(B, S, D)
```
